```python
import math
import jax, jax.numpy as jnp
from jax import lax
import numpy as np

D_MODEL = 2048
BATCH = 4
SEQ = 4096
DEPTH = 4

GRID_W = 64
CTX_LEN = 256
MLA_HEADS = 8
MLA_Q_LORA = 512
MLA_KV_LORA = 256
MLA_NOPE = 128
MLA_ROPE = 64
MLA_V = 128
NA_HEADS = 16
NA_HEAD_DIM = 64
NA_WIN_ROWS = 8
NA_WIN_COLS = 16
LRU_WIDTH = 1024
LRU_BLOCKS = 16
LRU_BLOCK = LRU_WIDTH // LRU_BLOCKS
LRU_C = 8.0
CONV_WIDTH = 4
N_BRANCH = 3
BRANCH_W = 1024
ROPE_BASE = 10000.0
EPS = 1e-6
Q_BLOCK = 128
N_MIX = MLA_Q_LORA + MLA_KV_LORA + MLA_ROPE + 3 * NA_HEADS * NA_HEAD_DIM + LRU_WIDTH
N_GATE = N_BRANCH * BRANCH_W + N_BRANCH * D_MODEL
N_IN = N_MIX + N_GATE

kernel_name = "hybrid_mla_natten_rglru_dit"


def rms_norm(x, g):
    x32 = x.astype(jnp.float32)
    y = x32 * lax.rsqrt(jnp.mean(x32 * x32, axis=-1, keepdims=True) + EPS)
    return (y * g.astype(jnp.float32)).astype(x.dtype)


def rope_half(x, ang):
    n = x.shape[-1] // 2
    x1, x2 = x[..., :n], x[..., n:]
    cos = jnp.cos(ang).astype(x.dtype)
    sin = jnp.sin(ang).astype(x.dtype)
    return jnp.concatenate([x1 * cos - x2 * sin, x1 * sin + x2 * cos], axis=-1)


def axial_rope_angles(n_tokens, dim):
    t = jnp.arange(n_tokens)
    row = (t // GRID_W).astype(jnp.float32)
    col = (t % GRID_W).astype(jnp.float32)
    half = dim // 2
    inv = 1.0 / (ROPE_BASE ** (jnp.arange(0, half, 2, dtype=jnp.float32) / half))
    return row[:, None] * inv, col[:, None] * inv


def axial_rope(x, ang_row, ang_col):
    half = x.shape[-1] // 2
    return jnp.concatenate([rope_half(x[..., :half], ang_row), rope_half(x[..., half:], ang_col)], axis=-1)


def mla_branch(p_q, p_kv, p_kr, n_ctx, q_norm_g, kv_norm_g, w_uq, w_ukv, need_ctx):
    B, T, _ = p_q.shape
    S = T - n_ctx
    q = (rms_norm(p_q, q_norm_g) @ w_uq).reshape(B, T, MLA_HEADS, MLA_NOPE + MLA_ROPE)
    kv = (rms_norm(p_kv, kv_norm_g) @ w_ukv).reshape(B, T, MLA_HEADS, MLA_NOPE + MLA_V)
    q_nope, q_rope = q[..., :MLA_NOPE], q[..., MLA_NOPE:]
    k_nope, v = kv[..., :MLA_NOPE], kv[..., MLA_NOPE:]
    ang_r, ang_c = axial_rope_angles(S, MLA_ROPE)
    k_rope = jnp.concatenate([p_kr[:, :n_ctx], axial_rope(p_kr[:, n_ctx:], ang_r, ang_c)], axis=1)
    q_rope_lat = axial_rope(q_rope[:, n_ctx:], ang_r[:, None], ang_c[:, None])
    scale = (MLA_NOPE + MLA_ROPE) ** -0.5

    def attend(qn, qr, kn, kr, vv):
        s = jnp.einsum('bqhd,bkhd->bhqk', qn, kn) + jnp.einsum('bqhr,bkr->bhqk', qr, kr)
        p = jax.nn.softmax(s.astype(jnp.float32) * scale, axis=-1).astype(vv.dtype)
        return jnp.einsum('bhqk,bkhv->bqhv', p, vv)

    n_blk = S // Q_BLOCK
    qn_blk = q_nope[:, n_ctx:].reshape(B, n_blk, Q_BLOCK, MLA_HEADS, MLA_NOPE).transpose(1, 0, 2, 3, 4)
    qr_blk = q_rope_lat.reshape(B, n_blk, Q_BLOCK, MLA_HEADS, MLA_ROPE).transpose(1, 0, 2, 3, 4)
    o = lax.map(lambda qb: attend(qb[0], qb[1], k_nope, k_rope, v), (qn_blk, qr_blk))
    o_lat = o.transpose(1, 0, 2, 3, 4).reshape(B, S, MLA_HEADS * MLA_V)
    o_ctx = None
    if need_ctx:
        o_ctx = attend(q_nope[:, :n_ctx], q_rope[:, :n_ctx], k_nope[:, :n_ctx],
                       k_rope[:, :n_ctx], v[:, :n_ctx]).reshape(B, n_ctx, MLA_HEADS * MLA_V)
    return o_lat, o_ctx


def na_branch(p_q, p_k, p_v, n_ctx, rel_bias, need_ctx):
    B, T, _ = p_q.shape
    S = T - n_ctx
    rows = S // GRID_W
    kr = min(NA_WIN_ROWS, rows)
    kc = NA_WIN_COLS
    H, d = NA_HEADS, NA_HEAD_DIM
    q = p_q.reshape(B, T, H, d)
    k = p_k.reshape(B, T, H, d)
    v = p_v.reshape(B, T, H, d)
    q_ctx, k_ctx, v_ctx = q[:, :n_ctx], k[:, :n_ctx], v[:, :n_ctx]
    q_grid = q[:, n_ctx:].reshape(B, rows, GRID_W, H, d)
    k_grid = k[:, n_ctx:].reshape(B, rows, GRID_W, H, d)
    v_grid = v[:, n_ctx:].reshape(B, rows, GRID_W, H, d)
    cols = np.arange(GRID_W)
    col_start = np.clip(cols - kc // 2, 0, GRID_W - kc)
    col_idx = col_start[:, None] + np.arange(kc)[None, :]
    dc_idx = col_idx - cols[:, None] + (NA_WIN_COLS - 1)
    bias_c = rel_bias[:, :, dc_idx]
    scale = d ** -0.5

    def row_step(i):
        r0 = jnp.clip(i - kr // 2, 0, rows - kr)
        q_row = lax.dynamic_index_in_dim(q_grid, i, axis=1, keepdims=False)
        k_rows = lax.dynamic_slice_in_dim(k_grid, r0, kr, axis=1)
        v_rows = lax.dynamic_slice_in_dim(v_grid, r0, kr, axis=1)
        k_win = k_rows[:, :, col_idx]
        v_win = v_rows[:, :, col_idx]
        dr_idx = r0 + jnp.arange(kr) - i + (NA_WIN_ROWS - 1)
        bias = jnp.take(bias_c, dr_idx, axis=1).transpose(0, 2, 1, 3)
        s_loc = (jnp.einsum('bjhd,brjchd->bhjrc', q_row, k_win).astype(jnp.float32) * scale
                 + bias.astype(jnp.float32)[None])
        s_loc = s_loc.reshape(B, H, GRID_W, kr * kc)
        s_ctx = jnp.einsum('bjhd,bkhd->bhjk', q_row, k_ctx).astype(jnp.float32) * scale
        p = jax.nn.softmax(jnp.concatenate([s_loc, s_ctx], axis=-1), axis=-1).astype(v.dtype)
        p_loc = p[..., :kr * kc].reshape(B, H, GRID_W, kr, kc)
        p_ctx = p[..., kr * kc:]
        return (jnp.einsum('bhjrc,brjchd->bjhd', p_loc, v_win)
                + jnp.einsum('bhjk,bkhd->bjhd', p_ctx, v_ctx))

    o = lax.map(row_step, jnp.arange(rows))
    o_lat = o.transpose(1, 0, 2, 3, 4).reshape(B, S, H * d)
    o_ctx = None
    if need_ctx:
        s = jnp.einsum('bqhd,bkhd->bhqk', q_ctx, k_ctx).astype(jnp.float32) * scale
        p = jax.nn.softmax(s, axis=-1).astype(v.dtype)
        o_ctx = jnp.einsum('bhqk,bkhd->bqhd', p, v_ctx).reshape(B, n_ctx, H * d)
    return o_lat, o_ctx


def depthwise_conv_centred(x, w, b):
    L = x.shape[1]
    left = CONV_WIDTH // 2
    right = CONV_WIDTH - 1 - left
    xp = jnp.pad(x, ((0, 0), (left, right), (0, 0)))
    y = b
    for tap in range(CONV_WIDTH):
        y = y + xp[:, tap:tap + L] * w[tap]
    return y


def linear_combine(e1, e2):
    a1, b1 = e1
    a2, b2 = e2
    return a1 * a2, a2 * b1 + b2


def rglru_scan(x, w_gate, b_gate, lam, h0, reverse):
    B, L, W = x.shape
    x32 = x.astype(jnp.float32)
    xb = x32.reshape(B, L, LRU_BLOCKS, LRU_BLOCK)
    g = (jnp.einsum('blnk,nkj->blnj', xb, w_gate.astype(jnp.float32))
         + b_gate.astype(jnp.float32).reshape(LRU_BLOCKS, 2 * LRU_BLOCK))
    r = jax.nn.sigmoid(g[..., :LRU_BLOCK]).reshape(B, L, W)
    i_g = jax.nn.sigmoid(g[..., LRU_BLOCK:]).reshape(B, L, W)
    log_a = -LRU_C * r * jax.nn.softplus(-lam.astype(jnp.float32))
    a = jnp.exp(log_a)
    u = jnp.sqrt(jnp.maximum(-jnp.expm1(2.0 * log_a), 0.0)) * (i_g * x32)
    edge = -1 if reverse else 0
    u = u.at[:, edge].add(a[:, edge] * h0)
    _, h = lax.associative_scan(linear_combine, (a, u), reverse=reverse, axis=1)
    return h


def rglru_branch(p_x, n_ctx, conv_w, conv_b, w_gate, b_gate, lam, need_ctx):
    B = p_x.shape[0]
    xc = depthwise_conv_centred(p_x[:, :n_ctx], conv_w, conv_b)
    xl = depthwise_conv_centred(p_x[:, n_ctx:], conv_w, conv_b)
    zero = jnp.zeros((B, LRU_WIDTH), jnp.float32)
    hc_f = rglru_scan(xc, w_gate[0], b_gate[0], lam[0], zero, False)
    hc_b = rglru_scan(xc, w_gate[1], b_gate[1], lam[1], zero, True)
    hl_f = rglru_scan(xl, w_gate[0], b_gate[0], lam[0], hc_f[:, -1], False)
    hl_b = rglru_scan(xl, w_gate[1], b_gate[1], lam[1], hc_b[:, 0], True)
    o_lat = (hl_f + hl_b).astype(p_x.dtype)
    o_ctx = (hc_f + hc_b).astype(p_x.dtype) if need_ctx else None
    return o_lat, o_ctx


def merge_branches(o_a, o_b, o_c, pg, w_branch, w_out):
    B, L, _ = o_a.shape
    o = jnp.stack([o_a, o_b, o_c], axis=2)
    gp = pg[..., :N_BRANCH * BRANCH_W].reshape(B, L, N_BRANCH, BRANCH_W)
    mg = pg[..., N_BRANCH * BRANCH_W:].reshape(B, L, N_BRANCH, D_MODEL)
    y = jnp.einsum('blnw,nwd->blnd', o * jax.nn.silu(gp), w_branch)
    return jnp.sum(jax.nn.sigmoid(mg) * y, axis=2) @ w_out


def hybrid_mixer(h_lat, h_ctx, w_in, q_norm_g, kv_norm_g, w_uq, w_ukv, rel_bias,
                 conv_w, conv_b, lru_w_gate, lru_b_gate, lru_lambda, w_branch, w_out, need_ctx):
    n_ctx = h_ctx.shape[1]
    h_all = jnp.concatenate([h_ctx, h_lat], axis=1)
    p = h_all @ w_in[:, :N_MIX]
    na_w = NA_HEADS * NA_HEAD_DIM
    o0 = MLA_Q_LORA
    o1 = o0 + MLA_KV_LORA
    o2 = o1 + MLA_ROPE
    o3 = o2 + na_w
    o4 = o3 + na_w
    o5 = o4 + na_w
    p_q, p_kv, p_kr, na_q, na_k, na_v, p_x = jnp.split(p, [o0, o1, o2, o3, o4, o5], axis=-1)
    a_lat, a_ctx = mla_branch(p_q, p_kv, p_kr, n_ctx, q_norm_g, kv_norm_g, w_uq, w_ukv, need_ctx)
    b_lat, b_ctx = na_branch(na_q, na_k, na_v, n_ctx, rel_bias, need_ctx)
    c_lat, c_ctx = rglru_branch(p_x, n_ctx, conv_w, conv_b, lru_w_gate, lru_b_gate, lru_lambda, need_ctx)
    w_g = w_in[:, N_MIX:]
    y_lat = merge_branches(a_lat, b_lat, c_lat, h_lat @ w_g, w_branch, w_out)
    y_ctx = None
    if need_ctx:
        y_ctx = merge_branches(a_ctx, b_ctx, c_ctx, h_ctx @ w_g, w_branch, w_out)
    return y_lat, y_ctx


def setup_inputs(seed: int = 0) -> dict:
    key = jax.random.key(seed)
    ks = jax.random.split(key, 24)
    f32 = jnp.float32

    def nrm(k, shape, scale):
        return jax.random.normal(k, shape, f32) * scale

    u = jax.random.uniform(ks[15], (DEPTH, 2, LRU_WIDTH), f32, minval=0.9, maxval=0.999)
    a0 = u ** (1.0 / LRU_C)
    lru_lambda = jnp.log(a0) - jnp.log1p(-a0)
    return {
        "x": nrm(ks[0], (BATCH, SEQ, D_MODEL), 1.0),
        "c": nrm(ks[1], (BATCH, D_MODEL), 1.0),
        "ctx": nrm(ks[2], (BATCH, CTX_LEN, D_MODEL), 1.0),
        "c_ctx": nrm(ks[3], (D_MODEL,), 1.0),
        "ada_w": nrm(ks[4], (DEPTH, D_MODEL, 3 * D_MODEL), 0.01),
        "ada_b": nrm(ks[5], (DEPTH, 3 * D_MODEL), 0.02),
        "norm_g": 1.0 + nrm(ks[6], (DEPTH, D_MODEL), 0.02),
        "w_in": nrm(ks[7], (DEPTH, D_MODEL, N_IN), D_MODEL ** -0.5),
        "mla_q_norm_g": 1.0 + nrm(ks[8], (DEPTH, MLA_Q_LORA), 0.02),
        "mla_kv_norm_g": 1.0 + nrm(ks[9], (DEPTH, MLA_KV_LORA), 0.02),
        "mla_w_uq": nrm(ks[10], (DEPTH, MLA_Q_LORA, MLA_HEADS * (MLA_NOPE + MLA_ROPE)), MLA_Q_LORA ** -0.5),
        "mla_w_ukv": nrm(ks[11], (DEPTH, MLA_KV_LORA, MLA_HEADS * (MLA_NOPE + MLA_V)), MLA_KV_LORA ** -0.5),
        "na_rel_bias": nrm(ks[12], (DEPTH, NA_HEADS, 2 * NA_WIN_ROWS - 1, 2 * NA_WIN_COLS - 1), 0.1),
        "lru_conv_w": nrm(ks[13], (DEPTH, CONV_WIDTH, LRU_WIDTH), CONV_WIDTH ** -0.5),
        "lru_conv_b": nrm(ks[14], (DEPTH, LRU_WIDTH), 0.02),
        "lru_w_gate": nrm(ks[16], (DEPTH, 2, LRU_BLOCKS, LRU_BLOCK, 2 * LRU_BLOCK), LRU_BLOCK ** -0.5),
        "lru_b_gate": nrm(ks[17], (DEPTH, 2, 2 * LRU_WIDTH), 0.02),
        "lru_lambda": lru_lambda,
        "w_branch": nrm(ks[18], (DEPTH, N_BRANCH, BRANCH_W, D_MODEL), BRANCH_W ** -0.5),
        "w_out": nrm(ks[19], (DEPTH, D_MODEL, D_MODEL), D_MODEL ** -0.5),
        "final_norm_g": 1.0 + nrm(ks[20], (D_MODEL,), 0.02),
    }


def reference(x, c, ctx, c_ctx, ada_w, ada_b, norm_g, w_in, mla_q_norm_g, mla_kv_norm_g,
              mla_w_uq, mla_w_ukv, na_rel_bias, lru_conv_w, lru_conv_b, lru_w_gate, lru_b_gate,
              lru_lambda, w_branch, w_out, final_norm_g):
    silu_c = jax.nn.silu(c)
    silu_cc = jax.nn.silu(c_ctx)
    for layer in range(DEPTH):
        need_ctx = layer < DEPTH - 1
        mod_l = silu_c @ ada_w[layer] + ada_b[layer]
        mod_c = silu_cc @ ada_w[layer] + ada_b[layer]
        sh_l, sc_l, gt_l = jnp.split(mod_l, 3, axis=-1)
        sh_c, sc_c, gt_c = jnp.split(mod_c, 3, axis=-1)
        h_lat = rms_norm(x, norm_g[layer]) * (1.0 + sc_l[:, None]) + sh_l[:, None]
        h_ctx = rms_norm(ctx, norm_g[layer]) * (1.0 + sc_c) + sh_c
        y_lat, y_ctx = hybrid_mixer(h_lat, h_ctx, w_in[layer], mla_q_norm_g[layer], mla_kv_norm_g[layer],
                                    mla_w_uq[layer], mla_w_ukv[layer], na_rel_bias[layer],
                                    lru_conv_w[layer], lru_conv_b[layer], lru_w_gate[layer],
                                    lru_b_gate[layer], lru_lambda[layer], w_branch[layer], w_out[layer],
                                    need_ctx)
        x = x + gt_l[:, None] * y_lat
        if need_ctx:
            ctx = ctx + gt_c * y_ctx
    return rms_norm(x, final_norm_g)
```

```python
import functools

import numpy as np
import jax
import jax.numpy as jnp
from jax import lax
from jax.experimental import pallas as pl
from jax.experimental.pallas import tpu as pltpu

GRID_W = 64
MLA_HEADS = 8
MLA_Q_LORA = 512
MLA_KV_LORA = 256
MLA_NOPE = 128
MLA_ROPE = 64
MLA_V = 128
NA_HEADS = 16
NA_HEAD_DIM = 64
NA_WIN_ROWS = 8
NA_WIN_COLS = 16
LRU_WIDTH = 1024
LRU_BLOCKS = 16
LRU_BLOCK = LRU_WIDTH // LRU_BLOCKS
LRU_C = 8.0
CONV_WIDTH = 4
N_BRANCH = 3
BRANCH_W = 1024
ROPE_BASE = 10000.0
EPS = 1e-6

LANES = 128
SUBLANES = 8
VMEM_LIMIT_BYTES = 56 * 1024 * 1024

MLA_HEAD_W = 2 * LANES
Q_OFF = 0
KV_OFF = Q_OFF + MLA_Q_LORA
KR_OFF = KV_OFF + MLA_KV_LORA
KR_W = MLA_HEAD_W
MLA_IN_W = KR_OFF + KR_W
NA_W = NA_HEADS * NA_HEAD_DIM
NAQ_OFF = MLA_IN_W
NAK_OFF = NAQ_OFF + NA_W
NAV_OFF = NAK_OFF + NA_W
LRU_OFF = NAV_OFF + NA_W
GP_OFF = LRU_OFF + LRU_WIDTH
MG_OFF = GP_OFF + N_BRANCH * BRANCH_W
NEG_INF = -1e30

_Q = MLA_ROPE // 4
ROPE_SWAP = np.concatenate([np.arange(_Q, 2 * _Q), np.arange(0, _Q),
                            np.arange(3 * _Q, 4 * _Q), np.arange(2 * _Q, 3 * _Q)])

NA_Q_ROWS = 4
NA_KEY_ROWS = NA_Q_ROWS + NA_WIN_ROWS - 1
LRU_CHUNK = 256
LRU_COLS = 256


def _cparams(*sem):
    return pltpu.CompilerParams(dimension_semantics=sem, vmem_limit_bytes=VMEM_LIMIT_BYTES)


def _dot(a, b):
    return jnp.dot(a, b, preferred_element_type=jnp.float32)


def _dot_nt(a, b):
    return lax.dot_general(a, b, (((1,), (1,)), ((), ())), preferred_element_type=jnp.float32)


def _rms(x, g):
    return x * lax.rsqrt(jnp.mean(x * x, axis=-1, keepdims=True) + EPS) * g


def _row_tile(t, target):
    best = None
    for cand in range(SUBLANES, min(t, target) + 1, SUBLANES):
        if t % cand == 0:
            best = cand
    assert best is not None
    return best


def _ada_kernel(c_ref, w_ref, b_ref, o_ref):
    c = c_ref[...]
    s = (c * jax.nn.sigmoid(c)).astype(jnp.bfloat16)
    o_ref[0] = _dot(s, w_ref[0].astype(jnp.bfloat16)) + b_ref[0]


def _ada_mod(cond, ada_w, ada_b):
    depth, d, n = ada_w.shape
    rows = cond.shape[0]
    tn = 1024
    return pl.pallas_call(
        _ada_kernel,
        grid=(depth, n // tn),
        in_specs=[pl.BlockSpec((rows, d), lambda l, j: (0, 0)),
                  pl.BlockSpec((1, d, tn), lambda l, j: (l, 0, j)),
                  pl.BlockSpec((1, 1, tn), lambda l, j: (l, 0, j))],
        out_specs=pl.BlockSpec((1, rows, tn), lambda l, j: (l, 0, j)),
        out_shape=jax.ShapeDtypeStruct((depth, rows, n), jnp.float32),
        compiler_params=_cparams("parallel", "parallel"),
        name="ada_mod",
    )(cond, ada_w, ada_b.reshape(depth, 1, n))


def _inproj_kernel(x_ref, g_ref, scl_ref, shl_ref, scc_ref, shc_ref, w_ref, o_ref, h_ref, *, n_ctx):
    i = pl.program_id(1)
    tm = x_ref.shape[1]

    @pl.when(pl.program_id(2) == 0)
    def _():
        y = _rms(x_ref[0], g_ref[...])
        row = i * tm + lax.broadcasted_iota(jnp.int32, (tm, 1), 0)
        is_ctx = row < n_ctx
        sc = jnp.where(is_ctx, scc_ref[...], scl_ref[0])
        sh = jnp.where(is_ctx, shc_ref[...], shl_ref[0])
        h_ref[...] = (y * (1.0 + sc) + sh).astype(h_ref.dtype)

    o_ref[0] = _dot(h_ref[...], w_ref[...]).astype(o_ref.dtype)


def _inproj(xa, g, sc_l, sh_l, sc_c, sh_c, w, n_ctx):
    b, t, d = xa.shape
    n = w.shape[1]
    tm = _row_tile(t, 1088)
    tn = 512
    vec = pl.BlockSpec((1, d), lambda bi, i, j: (0, 0))
    per_b = pl.BlockSpec((1, 1, d), lambda bi, i, j: (bi, 0, 0))
    return pl.pallas_call(
        functools.partial(_inproj_kernel, n_ctx=n_ctx),
        grid=(b, t // tm, n // tn),
        in_specs=[pl.BlockSpec((1, tm, d), lambda bi, i, j: (bi, i, 0)),
                  vec, per_b, per_b, vec, vec,
                  pl.BlockSpec((d, tn), lambda bi, i, j: (0, j))],
        out_specs=pl.BlockSpec((1, tm, tn), lambda bi, i, j: (bi, i, j)),
        out_shape=jax.ShapeDtypeStruct((b, t, n), jnp.float32),
        scratch_shapes=[pltpu.VMEM((tm, d), jnp.bfloat16)],
        compiler_params=_cparams("parallel", "parallel", "arbitrary"),
        name="inproj",
    )(xa, g, sc_l, sh_l, sc_c, sh_c, w)


def _mla_prep_kernel(p_ref, gq_ref, gkv_ref, wq_ref, wkv_ref, cos_ref, sin_ref,
                     q_ref, k_ref, v_ref):
    p = p_ref[0]
    cos = cos_ref[...]
    sin = sin_ref[...]
    shift = MLA_HEAD_W - MLA_ROPE

    def rope(a):
        return a * cos + pltpu.roll(a, shift, 1) * sin

    hq = _rms(p[:, Q_OFF:Q_OFF + MLA_Q_LORA], gq_ref[...]).astype(jnp.bfloat16)
    hkv = _rms(p[:, KV_OFF:KV_OFF + MLA_KV_LORA], gkv_ref[...]).astype(jnp.bfloat16)
    k_rope = rope(p[:, KR_OFF:KR_OFF + KR_W])
    q = _dot(hq, wq_ref[...])
    kv = _dot(hkv, wkv_ref[...])
    k_w = MLA_HEADS * MLA_HEAD_W
    for h in range(MLA_HEADS):
        sl = slice(h * MLA_HEAD_W, (h + 1) * MLA_HEAD_W)
        q_ref[0, :, sl] = rope(q[:, sl]).astype(q_ref.dtype)
        k_ref[0, :, sl] = (kv[:, sl] + k_rope).astype(k_ref.dtype)
    v_ref[0] = kv[:, k_w:].astype(v_ref.dtype)


def _mla_prep(p, gq, gkv, wq, wkv, cos_t, sin_t):
    b, t, _ = p.shape
    tm = _row_tile(t, 272)
    qk_w = MLA_HEADS * MLA_HEAD_W
    v_w = MLA_HEADS * MLA_V
    full = lambda a: pl.BlockSpec(a.shape, lambda bi, i: (0,) * a.ndim)
    return pl.pallas_call(
        _mla_prep_kernel,
        grid=(b, t // tm),
        in_specs=[pl.BlockSpec((1, tm, MLA_IN_W), lambda bi, i: (bi, i, 0)),
                  full(gq), full(gkv), full(wq), full(wkv),
                  pl.BlockSpec((tm, MLA_HEAD_W), lambda bi, i: (i, 0)),
                  pl.BlockSpec((tm, MLA_HEAD_W), lambda bi, i: (i, 0))],
        out_specs=[pl.BlockSpec((1, tm, qk_w), lambda bi, i: (bi, i, 0)),
                   pl.BlockSpec((1, tm, qk_w), lambda bi, i: (bi, i, 0)),
                   pl.BlockSpec((1, tm, v_w), lambda bi, i: (bi, i, 0))],
        out_shape=[jax.ShapeDtypeStruct((b, t, qk_w), jnp.bfloat16),
                   jax.ShapeDtypeStruct((b, t, qk_w), jnp.bfloat16),
                   jax.ShapeDtypeStruct((b, t, v_w), jnp.bfloat16)],
        compiler_params=_cparams("parallel", "parallel"),
        name="mla_prep",
    )(p, gq, gkv, wq, wkv, cos_t, sin_t)


def _mla_attn_kernel(q_ref, k_ref, v_ref, o_ref, *, n_ctx, scale):
    i = pl.program_id(2)
    tq = q_ref.shape[1]
    t = k_ref.shape[1]

    def attend(nk):
        s = _dot_nt(q_ref[0], k_ref[0, :nk, :]) * scale
        m = jnp.max(s, axis=-1, keepdims=True)
        e = jnp.exp(s - m)
        l = jnp.sum(e, axis=-1, keepdims=True)
        o = _dot(e.astype(jnp.bfloat16), v_ref[0, :nk, :])
        o_ref[0] = (o / l).astype(o_ref.dtype)

    @pl.when(i < n_ctx // tq)
    def _():
        attend(n_ctx)

    @pl.when(i >= n_ctx // tq)
    def _():
        attend(t)


def _mla_attn(q, k, v, n_ctx):
    b, t, _ = q.shape
    tq = 256
    assert n_ctx % tq == 0 and t % tq == 0
    scale = float((MLA_NOPE + MLA_ROPE) ** -0.5)
    return pl.pallas_call(
        functools.partial(_mla_attn_kernel, n_ctx=n_ctx, scale=scale),
        grid=(b, MLA_HEADS, t // tq),
        in_specs=[pl.BlockSpec((1, tq, MLA_HEAD_W), lambda bi, h, i: (bi, i, h)),
                  pl.BlockSpec((1, t, MLA_HEAD_W), lambda bi, h, i: (bi, 0, h)),
                  pl.BlockSpec((1, t, MLA_V), lambda bi, h, i: (bi, 0, h))],
        out_specs=pl.BlockSpec((1, tq, MLA_V), lambda bi, h, i: (bi, i, h)),
        out_shape=jax.ShapeDtypeStruct((b, t, MLA_HEADS * MLA_V), jnp.float32),
        compiler_params=_cparams("parallel", "parallel", "arbitrary"),
        name="mla_attn",
    )(q, k, v)


def _na_plan(rows):
    kr = min(NA_WIN_ROWS, rows)
    assert rows % NA_Q_ROWS == 0 and rows >= NA_KEY_ROWS
    starts, patterns, var_of_block = [], [], []
    for r in range(rows // NA_Q_ROWS):
        start = int(np.clip(NA_Q_ROWS * r - kr // 2, 0, rows - NA_KEY_ROWS))
        valid = np.zeros((NA_Q_ROWS, NA_KEY_ROWS), bool)
        dr = np.zeros((NA_Q_ROWS, NA_KEY_ROWS), np.int32)
        for a in range(NA_Q_ROWS):
            i = NA_Q_ROWS * r + a
            r0 = int(np.clip(i - kr // 2, 0, rows - kr))
            assert start <= r0 and r0 + kr <= start + NA_KEY_ROWS
            for kb in range(NA_KEY_ROWS):
                krow = start + kb
                valid[a, kb] = r0 <= krow < r0 + kr
                dr[a, kb] = krow - i + (NA_WIN_ROWS - 1)
        dr = np.where(valid, dr, 0)
        key = (valid.tobytes(), dr.tobytes())
        for v, (pk, _, _) in enumerate(patterns):
            if pk == key:
                var_of_block.append(v)
                break
        else:
            var_of_block.append(len(patterns))
            patterns.append((key, valid, dr))
        starts.append(start)
    valid_r = np.stack([p[1] for p in patterns])
    dr_idx = np.stack([p[2] for p in patterns])
    cols = np.arange(GRID_W)
    col_start = np.clip(cols - NA_WIN_COLS // 2, 0, GRID_W - NA_WIN_COLS)
    valid_c = (cols[None, :] >= col_start[:, None]) & (cols[None, :] < col_start[:, None] + NA_WIN_COLS)
    dc_idx = np.where(valid_c, cols[None, :] - cols[:, None] + (NA_WIN_COLS - 1), 0)
    return np.array(starts), np.array(var_of_block), valid_r, dr_idx, valid_c, dc_idx


def _na_bias_tables(rel_bias, rows):
    _, _, valid_r, dr_idx, valid_c, dc_idx = _na_plan(rows)
    depth, heads = rel_bias.shape[:2]
    n_var = valid_r.shape[0]
    n_dc = rel_bias.shape[-1]
    onehot = (dc_idx.reshape(-1)[None, :] == np.arange(n_dc)[:, None]).astype(np.float32)
    e1 = jnp.einsum('lhrd,dx->lhrx', rel_bias, jnp.asarray(onehot), precision=lax.Precision.HIGHEST)
    e1 = e1.reshape(depth, heads, -1, GRID_W, GRID_W)
    e1 = jnp.where(jnp.asarray(valid_c), e1, NEG_INF)
    e2 = jnp.take(e1, jnp.asarray(dr_idx.reshape(-1)), axis=2)
    e2 = e2.reshape(depth, heads, n_var, NA_Q_ROWS, NA_KEY_ROWS, GRID_W, GRID_W)
    e2 = jnp.where(jnp.asarray(valid_r)[None, None, :, :, :, None, None], e2, NEG_INF)
    e2 = e2.transpose(0, 2, 1, 3, 5, 4, 6)
    return e2.reshape(depth, n_var, heads, NA_Q_ROWS * GRID_W, NA_KEY_ROWS * GRID_W)


def _na_attn_kernel(var_ref, ks_ref, q_ref, k_ref, v_ref, bm_ref, o_ref, kb_ref, vb_ref,
                    *, n_ctx, scale):
    del var_ref
    i = pl.program_id(2)
    tq = q_ref.shape[1]
    d = NA_HEAD_DIM
    n_pair = LANES // d
    n_win = bm_ref.shape[-1]

    @pl.when(i == 0)
    def _():
        kk = k_ref[0]
        vv = v_ref[0]
        for hh in range(n_pair):
            kb_ref[hh] = kk[:, hh * d:(hh + 1) * d].astype(kb_ref.dtype)
            vb_ref[hh] = vv[:, hh * d:(hh + 1) * d].astype(vb_ref.dtype)

    q = q_ref[0]

    @pl.when(i < n_ctx // tq)
    def _():
        outs = []
        for hh in range(n_pair):
            qh = q[:, hh * d:(hh + 1) * d].astype(jnp.bfloat16)
            s = _dot_nt(qh, kb_ref[hh, :n_ctx, :]) * scale
            m = jnp.max(s, axis=-1, keepdims=True)
            e = jnp.exp(s - m)
            l = jnp.sum(e, axis=-1, keepdims=True)
            outs.append(_dot(e.astype(jnp.bfloat16), vb_ref[hh, :n_ctx, :]) / l)
        o_ref[0] = jnp.concatenate(outs, axis=1).astype(o_ref.dtype)

    @pl.when(i >= n_ctx // tq)
    def _():
        ks = pl.multiple_of(ks_ref[i], GRID_W)
        outs = []
        for hh in range(n_pair):
            qh = q[:, hh * d:(hh + 1) * d].astype(jnp.bfloat16)
            s_c = _dot_nt(qh, kb_ref[hh, :n_ctx, :]) * scale
            s_w = _dot_nt(qh, kb_ref[hh, pl.ds(ks, n_win), :]) * scale + bm_ref[0, hh]
            m = jnp.maximum(jnp.max(s_c, axis=-1, keepdims=True), jnp.max(s_w, axis=-1, keepdims=True))
            e_c = jnp.exp(s_c - m)
            e_w = jnp.exp(s_w - m)
            l = jnp.sum(e_c, axis=-1, keepdims=True) + jnp.sum(e_w, axis=-1, keepdims=True)
            o = (_dot(e_w.astype(jnp.bfloat16), vb_ref[hh, pl.ds(ks, n_win), :])
                 + _dot(e_c.astype(jnp.bfloat16), vb_ref[hh, :n_ctx, :]))
            outs.append(o / l)
        o_ref[0] = jnp.concatenate(outs, axis=1).astype(o_ref.dtype)


def _na_attn(p, bm, n_ctx):
    b, t, _ = p.shape
    tq = NA_Q_ROWS * GRID_W
    assert n_ctx % tq == 0 and (t - n_ctx) % tq == 0
    rows = (t - n_ctx) // GRID_W
    starts, var_of_block, *_ = _na_plan(rows)
    n_cb = n_ctx // tq
    var_tab = jnp.asarray(np.concatenate([np.zeros(n_cb, np.int32), var_of_block]).astype(np.int32))
    ks_tab = jnp.asarray(np.concatenate([np.zeros(n_cb, np.int32), n_ctx + starts * GRID_W]).astype(np.int32))
    n_pair = LANES // NA_HEAD_DIM
    n_win = NA_KEY_ROWS * GRID_W
    qc, kc, vc = NAQ_OFF // LANES, NAK_OFF // LANES, NAV_OFF // LANES
    grid_spec = pltpu.PrefetchScalarGridSpec(
        num_scalar_prefetch=2,
        grid=(b, NA_HEADS // n_pair, t // tq),
        in_specs=[pl.BlockSpec((1, tq, LANES), lambda bi, h, i, var, ks: (bi, i, qc + h)),
                  pl.BlockSpec((1, t, LANES), lambda bi, h, i, var, ks: (bi, 0, kc + h)),
                  pl.BlockSpec((1, t, LANES), lambda bi, h, i, var, ks: (bi, 0, vc + h)),
                  pl.BlockSpec((1, n_pair, tq, n_win), lambda bi, h, i, var, ks: (var[i], h, 0, 0))],
        out_specs=pl.BlockSpec((1, tq, LANES), lambda bi, h, i, var, ks: (bi, i, h)),
        scratch_shapes=[pltpu.VMEM((n_pair, t, NA_HEAD_DIM), jnp.bfloat16),
                        pltpu.VMEM((n_pair, t, NA_HEAD_DIM), jnp.bfloat16)],
    )
    return pl.pallas_call(
        functools.partial(_na_attn_kernel, n_ctx=n_ctx, scale=float(NA_HEAD_DIM ** -0.5)),
        grid_spec=grid_spec,
        out_shape=jax.ShapeDtypeStruct((b, t, NA_W), jnp.float32),
        compiler_params=_cparams("parallel", "parallel", "arbitrary"),
        name="na_attn",
    )(var_tab, ks_tab, p, p, p, bm)


def _lru_kernel(x_ref, cw_ref, cb_ref, wg_ref, bg_ref, lam_ref, o_ref, xc_ref, a_ref, u_ref, *, n_ctx):
    t, wc = xc_ref.shape
    r = LRU_CHUNK
    n_chunk = t // r
    n_cc = n_ctx // r
    left = CONV_WIDTH // 2
    row = lax.broadcasted_iota(jnp.int32, (r, wc), 0)

    def conv_chunk(c, carry):
        t0 = pl.multiple_of(c * r, r)
        first = jnp.logical_or(c == 0, c == n_cc)
        last = jnp.logical_or(c == n_cc - 1, c == n_chunk - 1)
        main = x_ref[0, pl.ds(t0, r), :]
        prev = x_ref[0, pl.ds(pl.multiple_of(jnp.maximum(t0 - SUBLANES, 0), SUBLANES), SUBLANES), :]
        nxt = x_ref[0, pl.ds(pl.multiple_of(jnp.minimum(t0 + r, t - SUBLANES), SUBLANES), SUBLANES), :]
        prev = jnp.where(first, 0.0, prev)
        nxt = jnp.where(last, 0.0, nxt)
        y = cb_ref[...] + jnp.zeros((r, wc), jnp.float32)
        for tap in range(CONV_WIDTH):
            off = tap - left
            if off == 0:
                xs = main
            elif off < 0:
                xs = pltpu.roll(main, -off, 0)
                for k in range(-off):
                    xs = jnp.where(row == k, prev[SUBLANES + off + k:SUBLANES + off + k + 1, :], xs)
            else:
                xs = pltpu.roll(main, r - off, 0)
                for k in range(off):
                    xs = jnp.where(row == r - off + k, nxt[k:k + 1, :], xs)
            y = y + xs * cw_ref[tap:tap + 1, :]
        xc_ref[pl.ds(t0, r), :] = y
        return carry

    lax.fori_loop(0, n_chunk, conv_chunk, 0)

    grow = lax.broadcasted_iota(jnp.int32, (SUBLANES, wc), 0)
    n_grp = r // SUBLANES

    def scan_direction(direction):
        reverse = direction == 1
        lam = lam_ref[direction]
        neg = -lam
        softplus = jnp.maximum(neg, 0.0) + jnp.log1p(jnp.exp(-jnp.abs(neg)))
        decay = -LRU_C * softplus

        def chunk_step(step, h):
            if reverse:
                c = jnp.where(step < n_cc, n_cc - 1 - step, n_chunk - 1 - (step - n_cc))
            else:
                c = step
            t0 = pl.multiple_of(c * r, r)
            xc = xc_ref[pl.ds(t0, r), :]
            g = _dot(xc.astype(jnp.bfloat16), wg_ref[direction, 0]) + bg_ref[direction, 0]
            rg = jax.nn.sigmoid(g[:, :wc])
            ig = jax.nn.sigmoid(g[:, wc:])
            log_a = decay * rg
            a = jnp.exp(log_a)
            a_ref[...] = a
            u_ref[...] = jnp.sqrt(jnp.maximum(1.0 - a * a, 0.0)) * (ig * xc)

            def group_step(gi, h):
                g0 = (n_grp - 1 - gi) if reverse else gi
                off = pl.multiple_of(g0 * SUBLANES, SUBLANES)
                a = a_ref[pl.ds(off, SUBLANES), :]
                u = u_ref[pl.ds(off, SUBLANES), :]
                for s in (1, 2, 4):
                    if reverse:
                        keep = grow < SUBLANES - s
                        a_s = pltpu.roll(a, SUBLANES - s, 0)
                        u_s = pltpu.roll(u, SUBLANES - s, 0)
                    else:
                        keep = grow >= s
                        a_s = pltpu.roll(a, s, 0)
                        u_s = pltpu.roll(u, s, 0)
                    u = jnp.where(keep, a * u_s + u, u)
                    a = jnp.where(keep, a * a_s, a)
                hh = a * h + u
                dst = pl.ds(pl.multiple_of(t0 + off, SUBLANES), SUBLANES)
                if reverse:
                    o_ref[0, dst, :] = o_ref[0, dst, :] + hh
                    return hh[0:1, :]
                o_ref[0, dst, :] = hh
                return hh[SUBLANES - 1:SUBLANES, :]

            return lax.fori_loop(0, n_grp, group_step, h)

        lax.fori_loop(0, n_chunk, chunk_step, jnp.zeros((1, wc), jnp.float32))

    scan_direction(0)
    scan_direction(1)


def _lru(p, conv_w, conv_b, wg, bg, lam, n_ctx):
    b, t, _ = p.shape
    wc = LRU_COLS
    assert t % LRU_CHUNK == 0 and n_ctx % LRU_CHUNK == 0 and n_ctx > 0 and t > n_ctx
    xcol = LRU_OFF // wc
    return pl.pallas_call(
        functools.partial(_lru_kernel, n_ctx=n_ctx),
        grid=(b, LRU_WIDTH // wc),
        in_specs=[pl.BlockSpec((1, t, wc), lambda bi, c: (bi, 0, xcol + c)),
                  pl.BlockSpec((CONV_WIDTH, wc), lambda bi, c: (0, c)),
                  pl.BlockSpec((1, wc), lambda bi, c: (0, c)),
                  pl.BlockSpec((2, 1, wc, 2 * wc), lambda bi, c: (0, c, 0, 0)),
                  pl.BlockSpec((2, 1, 1, 2 * wc), lambda bi, c: (0, c, 0, 0)),
                  pl.BlockSpec((2, 1, wc), lambda bi, c: (0, 0, c))],
        out_specs=pl.BlockSpec((1, t, wc), lambda bi, c: (bi, 0, c)),
        out_shape=jax.ShapeDtypeStruct((b, t, LRU_WIDTH), jnp.float32),
        scratch_shapes=[pltpu.VMEM((t, wc), jnp.float32),
                        pltpu.VMEM((LRU_CHUNK, wc), jnp.float32),
                        pltpu.VMEM((LRU_CHUNK, wc), jnp.float32)],
        compiler_params=_cparams("parallel", "parallel"),
        name="rglru",
    )(p, conv_w, conv_b, wg, bg, lam)


def _merge_kernel(oa_ref, ob_ref, oc_ref, gpa_ref, gpb_ref, gpc_ref, mga_ref, mgb_ref, mgc_ref,
                  wb_ref, z_ref):
    z = None
    for n, (o_ref, gp_ref, mg_ref) in enumerate(((oa_ref, gpa_ref, mga_ref),
                                                 (ob_ref, gpb_ref, mgb_ref),
                                                 (oc_ref, gpc_ref, mgc_ref))):
        gp = gp_ref[0]
        tkn = (o_ref[0].astype(jnp.float32) * (gp * jax.nn.sigmoid(gp))).astype(jnp.bfloat16)
        y = jax.nn.sigmoid(mg_ref[0]) * _dot(tkn, wb_ref[n])
        z = y if z is None else z + y
    z_ref[0] = z.astype(z_ref.dtype)


def _merge(oa, ob, oc, p, wb):
    b, t, _ = oa.shape
    d = wb.shape[-1]
    tm = _row_tile(t, 256)
    gpc, mgc = GP_OFF // BRANCH_W, MG_OFF // d
    assert GP_OFF % BRANCH_W == 0 and MG_OFF % d == 0
    o_spec = pl.BlockSpec((1, tm, BRANCH_W), lambda bi, i: (bi, i, 0))
    gp_specs = [pl.BlockSpec((1, tm, BRANCH_W), functools.partial(lambda bi, i, n: (bi, i, gpc + n), n=n))
                for n in range(N_BRANCH)]
    mg_specs = [pl.BlockSpec((1, tm, d), functools.partial(lambda bi, i, n: (bi, i, mgc + n), n=n))
                for n in range(N_BRANCH)]
    return pl.pallas_call(
        _merge_kernel,
        grid=(b, t // tm),
        in_specs=[o_spec, o_spec, o_spec, *gp_specs, *mg_specs,
                  pl.BlockSpec(wb.shape, lambda bi, i: (0, 0, 0), pipeline_mode=pl.Buffered(1))],
        out_specs=pl.BlockSpec((1, tm, d), lambda bi, i: (bi, i, 0)),
        out_shape=jax.ShapeDtypeStruct((b, t, d), jnp.bfloat16),
        compiler_params=_cparams("parallel", "parallel"),
        name="merge",
    )(oa, ob, oc, p, p, p, p, p, p, wb)


def _outproj_kernel(z_ref, w_ref, x_ref, gl_ref, gc_ref, o_ref, *, n_ctx):
    i = pl.program_id(1)
    tm = x_ref.shape[1]
    row = i * tm + lax.broadcasted_iota(jnp.int32, (tm, 1), 0)
    gate = jnp.where(row < n_ctx, gc_ref[...], gl_ref[0])
    o_ref[0] = x_ref[0] + gate * _dot(z_ref[0], w_ref[...])


def _outproj(z, w, xa, gt_l, gt_c, n_ctx):
    b, t, d = xa.shape
    tm = _row_tile(t, 544)
    return pl.pallas_call(
        functools.partial(_outproj_kernel, n_ctx=n_ctx),
        grid=(b, t // tm),
        in_specs=[pl.BlockSpec((1, tm, d), lambda bi, i: (bi, i, 0)),
                  pl.BlockSpec((d, d), lambda bi, i: (0, 0), pipeline_mode=pl.Buffered(1)),
                  pl.BlockSpec((1, tm, d), lambda bi, i: (bi, i, 0)),
                  pl.BlockSpec((1, 1, d), lambda bi, i: (bi, 0, 0)),
                  pl.BlockSpec((1, d), lambda bi, i: (0, 0))],
        out_specs=pl.BlockSpec((1, tm, d), lambda bi, i: (bi, i, 0)),
        out_shape=jax.ShapeDtypeStruct((b, t, d), jnp.float32),
        compiler_params=_cparams("parallel", "parallel"),
        name="outproj",
    )(z, w, xa, gt_l, gt_c)


def _final_norm_kernel(x_ref, g_ref, o_ref):
    o_ref[0] = _rms(x_ref[0], g_ref[...])


def _final_norm(xa, g, n_ctx):
    b, t, d = xa.shape
    s = t - n_ctx
    tm = _row_tile(np.gcd(s, n_ctx), 512)
    skip = n_ctx // tm
    return pl.pallas_call(
        _final_norm_kernel,
        grid=(b, s // tm),
        in_specs=[pl.BlockSpec((1, tm, d), lambda bi, i: (bi, i + skip, 0)),
                  pl.BlockSpec((1, d), lambda bi, i: (0, 0))],
        out_specs=pl.BlockSpec((1, tm, d), lambda bi, i: (bi, i, 0)),
        out_shape=jax.ShapeDtypeStruct((b, s, d), jnp.float32),
        compiler_params=_cparams("parallel", "parallel"),
        name="final_norm",
    )(xa, g)


def _prep_w_in(w_in):
    kr0 = KV_OFF + MLA_KV_LORA
    kr1 = kr0 + MLA_ROPE
    w_kr = w_in[..., kr0:kr1]
    pad = jnp.zeros(w_in.shape[:-1] + (KR_W - 2 * MLA_ROPE,), w_in.dtype)
    parts = [w_in[..., :kr0], pad, w_kr, w_kr[..., ROPE_SWAP], w_in[..., kr1:]]
    return jnp.concatenate(parts, axis=-1).astype(jnp.bfloat16)


def _prep_w_uq(w_uq):
    depth, r, _ = w_uq.shape
    w = w_uq.reshape(depth, r, MLA_HEADS, MLA_NOPE + MLA_ROPE)
    rope = w[..., MLA_NOPE:]
    w = jnp.concatenate([w[..., :MLA_NOPE], rope, rope[..., ROPE_SWAP]], axis=-1)
    return w.reshape(depth, r, MLA_HEADS * MLA_HEAD_W).astype(jnp.bfloat16)


def _prep_w_ukv(w_ukv):
    depth, r, _ = w_ukv.shape
    w = w_ukv.reshape(depth, r, MLA_HEADS, MLA_NOPE + MLA_V)
    k = jnp.concatenate([w[..., :MLA_NOPE], jnp.zeros_like(w[..., :MLA_HEAD_W - MLA_NOPE])], axis=-1)
    k = k.reshape(depth, r, MLA_HEADS * MLA_HEAD_W)
    v = w[..., MLA_NOPE:].reshape(depth, r, MLA_HEADS * MLA_V)
    return jnp.concatenate([k, v], axis=-1).astype(jnp.bfloat16)


def _rope_tables(n_ctx, s):
    tok = jnp.arange(s)
    row = (tok // GRID_W).astype(jnp.float32)
    col = (tok % GRID_W).astype(jnp.float32)
    half = MLA_ROPE // 2
    inv = 1.0 / (ROPE_BASE ** (jnp.arange(0, half, 2, dtype=jnp.float32) / half))
    ang_r = row[:, None] * inv
    ang_c = col[:, None] * inv
    cos = jnp.concatenate([jnp.cos(ang_r), jnp.cos(ang_r), jnp.cos(ang_c), jnp.cos(ang_c)], axis=-1)
    sin = jnp.concatenate([-jnp.sin(ang_r), jnp.sin(ang_r), -jnp.sin(ang_c), jnp.sin(ang_c)], axis=-1)
    cos = jnp.concatenate([jnp.ones((n_ctx, MLA_ROPE), jnp.float32), cos], axis=0)
    sin = jnp.concatenate([jnp.zeros((n_ctx, MLA_ROPE), jnp.float32), sin], axis=0)
    t = n_ctx + s
    tail = MLA_HEAD_W - MLA_NOPE - MLA_ROPE
    cos_t = jnp.concatenate([jnp.ones((t, MLA_NOPE), jnp.float32), cos, jnp.zeros((t, tail), jnp.float32)], axis=-1)
    sin_t = jnp.concatenate([jnp.zeros((t, MLA_NOPE), jnp.float32), sin, jnp.zeros((t, tail), jnp.float32)], axis=-1)
    return cos_t, sin_t


def _prep_lru_gates(w_gate, b_gate):
    depth = w_gate.shape[0]
    per = LRU_COLS // LRU_BLOCK
    n_col = LRU_BLOCKS // per
    eye = jnp.eye(per, dtype=w_gate.dtype)

    def block_diag(w):
        w = w.reshape(depth, 2, n_col, per, LRU_BLOCK, LRU_BLOCK)
        w = w[:, :, :, :, :, None, :] * eye[None, None, None, :, None, :, None]
        return w.reshape(depth, 2, n_col, LRU_COLS, LRU_COLS)

    wg = jnp.concatenate([block_diag(w_gate[..., :LRU_BLOCK]), block_diag(w_gate[..., LRU_BLOCK:])], axis=-1)
    bg = b_gate.reshape(depth, 2, LRU_BLOCKS, 2 * LRU_BLOCK)
    bg = jnp.concatenate([bg[..., :LRU_BLOCK].reshape(depth, 2, n_col, 1, LRU_COLS),
                          bg[..., LRU_BLOCK:].reshape(depth, 2, n_col, 1, LRU_COLS)], axis=-1)
    return wg.astype(jnp.bfloat16), bg


def kernel(x, c, ctx, c_ctx, ada_w, ada_b, norm_g, w_in, mla_q_norm_g, mla_kv_norm_g, mla_w_uq, mla_w_ukv, na_rel_bias, lru_conv_w, lru_conv_b, lru_w_gate, lru_b_gate, lru_lambda, w_branch, w_out, final_norm_g):
    b, s, d = x.shape
    n_ctx = ctx.shape[1]
    depth = ada_w.shape[0]
    rows = s // GRID_W

    w_in_p = _prep_w_in(w_in)
    wq_p = _prep_w_uq(mla_w_uq)
    wkv_p = _prep_w_ukv(mla_w_ukv)
    cos_t, sin_t = _rope_tables(n_ctx, s)
    bm = _na_bias_tables(na_rel_bias, rows)
    wg_p, bg_p = _prep_lru_gates(lru_w_gate, lru_b_gate)
    wb_p = w_branch.astype(jnp.bfloat16)
    wo_p = w_out.astype(jnp.bfloat16)

    n_cond = -(-(b + 1) // SUBLANES) * SUBLANES
    cond = jnp.concatenate([c, c_ctx[None, :], jnp.zeros((n_cond - b - 1, d), c.dtype)], axis=0)
    mod = _ada_mod(cond, ada_w, ada_b)

    xa = jnp.concatenate([ctx, x], axis=1)
    for l in range(depth):
        sh_l, sc_l, gt_l = (mod[l, :b, k * d:(k + 1) * d][:, None, :] for k in range(3))
        sh_c, sc_c, gt_c = (mod[l, b:b + 1, k * d:(k + 1) * d] for k in range(3))
        p = _inproj(xa, norm_g[l][None, :], sc_l, sh_l, sc_c, sh_c, w_in_p[l], n_ctx)
        q, k, v = _mla_prep(p, mla_q_norm_g[l][None, :], mla_kv_norm_g[l][None, :],
                            wq_p[l], wkv_p[l], cos_t, sin_t)
        o_a = _mla_attn(q, k, v, n_ctx)
        o_b = _na_attn(p, bm[l], n_ctx)
        o_c = _lru(p, lru_conv_w[l], lru_conv_b[l][None, :], wg_p[l], bg_p[l],
                   lru_lambda[l][:, None, :], n_ctx)
        z = _merge(o_a, o_b, o_c, p, wb_p[l])
        xa = _outproj(z, wo_p[l], xa, gt_l, gt_c, n_ctx)
    return _final_norm(xa, final_norm_g[None, :], n_ctx)
```

```python
import functools

import numpy as np
import jax
import jax.numpy as jnp
from jax import lax
from jax.experimental import pallas as pl
from jax.experimental.pallas import tpu as pltpu

GRID_W = 64
MLA_HEADS = 8
MLA_Q_LORA = 512
MLA_KV_LORA = 256
MLA_NOPE = 128
MLA_ROPE = 64
MLA_V = 128
NA_HEADS = 16
NA_HEAD_DIM = 64
NA_WIN_ROWS = 8
NA_WIN_COLS = 16
LRU_WIDTH = 1024
LRU_BLOCKS = 16
LRU_BLOCK = LRU_WIDTH // LRU_BLOCKS
LRU_C = 8.0
CONV_WIDTH = 4
N_BRANCH = 3
BRANCH_W = 1024
ROPE_BASE = 10000.0
EPS = 1e-6

LANES = 128
SUBLANES = 8
VMEM_LIMIT_BYTES = 56 * 1024 * 1024

MLA_HEAD_W = 2 * LANES
Q_OFF = 0
KV_OFF = Q_OFF + MLA_Q_LORA
KR_OFF = KV_OFF + MLA_KV_LORA
KR_W = MLA_HEAD_W
MLA_IN_W = KR_OFF + KR_W
NA_W = NA_HEADS * NA_HEAD_DIM
NAQ_OFF = MLA_IN_W
NAK_OFF = NAQ_OFF + NA_W
NAV_OFF = NAK_OFF + NA_W
LRU_OFF = NAV_OFF + NA_W
GP_OFF = LRU_OFF + LRU_WIDTH
MG_OFF = GP_OFF + N_BRANCH * BRANCH_W
NEG_INF = -1e30

_Q = MLA_ROPE // 4
ROPE_SWAP = np.concatenate([np.arange(_Q, 2 * _Q), np.arange(0, _Q),
                            np.arange(3 * _Q, 4 * _Q), np.arange(2 * _Q, 3 * _Q)])

MLA_HEADS_PER_STEP = 2
MLA_KEY_CHUNK = 1024
NA_HEADS_PER_STEP = 4
NA_Q_ROWS = 4
NA_KEY_ROWS = NA_Q_ROWS + NA_WIN_ROWS - 1
LRU_CHUNK = 256
LRU_COLS = 256
LRU_GROUP_UNROLL = 8


def _cparams(*sem):
    return pltpu.CompilerParams(dimension_semantics=sem, vmem_limit_bytes=VMEM_LIMIT_BYTES)


def _dot(a, b):
    return jnp.dot(a, b, preferred_element_type=jnp.float32)


def _dot_nt(a, b):
    return lax.dot_general(a, b, (((1,), (1,)), ((), ())), preferred_element_type=jnp.float32)


def _rms(x, g):
    return x * lax.rsqrt(jnp.mean(x * x, axis=-1, keepdims=True) + EPS) * g


def _row_tile(t, target):
    best = None
    for cand in range(2 * SUBLANES, min(t, target) + 1, 2 * SUBLANES):
        if t % cand == 0:
            best = cand
    assert best is not None
    return best


def _ada_kernel(c_ref, w_ref, b_ref, o_ref):
    c = c_ref[...]
    s = (c * jax.nn.sigmoid(c)).astype(jnp.bfloat16)
    o_ref[0] = _dot(s, w_ref[0].astype(jnp.bfloat16)) + b_ref[0]


def _ada_mod(cond, ada_w, ada_b):
    depth, d, n = ada_w.shape
    rows = cond.shape[0]
    tn = 1024
    return pl.pallas_call(
        _ada_kernel,
        grid=(depth, n // tn),
        in_specs=[pl.BlockSpec((rows, d), lambda l, j: (0, 0)),
                  pl.BlockSpec((1, d, tn), lambda l, j: (l, 0, j)),
                  pl.BlockSpec((1, 1, tn), lambda l, j: (l, 0, j))],
        out_specs=pl.BlockSpec((1, rows, tn), lambda l, j: (l, 0, j)),
        out_shape=jax.ShapeDtypeStruct((depth, rows, n), jnp.float32),
        compiler_params=_cparams("parallel", "parallel"),
        name="ada_mod",
    )(cond, ada_w, ada_b.reshape(depth, 1, n))


def _inproj_kernel(x_ref, g_ref, scl_ref, shl_ref, scc_ref, shc_ref, wa_ref, wb_ref, o_ref, h_ref,
                   *, n_ctx):
    i = pl.program_id(1)
    j = pl.program_id(2)
    tm = x_ref.shape[1]

    @pl.when(j == 0)
    def _():
        y = _rms(x_ref[0], g_ref[...])
        row = i * tm + lax.broadcasted_iota(jnp.int32, (tm, 1), 0)
        is_ctx = row < n_ctx
        sc = jnp.where(is_ctx, scc_ref[...], scl_ref[0])
        sh = jnp.where(is_ctx, shc_ref[...], shl_ref[0])
        h_ref[...] = (y * (1.0 + sc) + sh).astype(h_ref.dtype)
        o_ref[0] = _dot(h_ref[...], wa_ref[...]).astype(o_ref.dtype)

    @pl.when(j > 0)
    def _():
        o_ref[0] = _dot(h_ref[...], wb_ref[...]).astype(o_ref.dtype)


def _inproj(xa, g, sc_l, sh_l, sc_c, sh_c, w_mla, w_rest, n_ctx):
    b, t, d = xa.shape
    tn = w_mla.shape[1]
    n_rest = w_rest.shape[1]
    assert n_rest % tn == 0
    tm = _row_tile(t, 1088)
    vec = pl.BlockSpec((1, d), lambda bi, i, j: (0, 0))
    per_b = pl.BlockSpec((1, 1, d), lambda bi, i, j: (bi, 0, 0))
    return pl.pallas_call(
        functools.partial(_inproj_kernel, n_ctx=n_ctx),
        grid=(b, t // tm, 1 + n_rest // tn),
        in_specs=[pl.BlockSpec((1, tm, d), lambda bi, i, j: (bi, i, 0)),
                  vec, per_b, per_b, vec, vec,
                  pl.BlockSpec((d, tn), lambda bi, i, j: (0, 0)),
                  pl.BlockSpec((d, tn), lambda bi, i, j: (0, jnp.maximum(j - 1, 0)))],
        out_specs=pl.BlockSpec((1, tm, tn), lambda bi, i, j: (bi, i, j)),
        out_shape=jax.ShapeDtypeStruct((b, t, tn + n_rest), jnp.bfloat16),
        scratch_shapes=[pltpu.VMEM((tm, d), jnp.bfloat16)],
        compiler_params=_cparams("parallel", "parallel", "arbitrary"),
        name="inproj",
    )(xa, g, sc_l, sh_l, sc_c, sh_c, w_mla, w_rest)


def _mla_prep_kernel(p_ref, gq_ref, gkv_ref, wq_ref, wkv_ref, cos_ref, sin_ref,
                     q_ref, k_ref, v_ref):
    p = p_ref[0].astype(jnp.float32)
    cos = cos_ref[...]
    sin = sin_ref[...]
    shift = MLA_HEAD_W - MLA_ROPE

    def rope(a):
        return a * cos + pltpu.roll(a, shift, 1) * sin

    hq = _rms(p[:, Q_OFF:Q_OFF + MLA_Q_LORA], gq_ref[...]).astype(jnp.bfloat16)
    hkv = _rms(p[:, KV_OFF:KV_OFF + MLA_KV_LORA], gkv_ref[...]).astype(jnp.bfloat16)
    k_rope = rope(p[:, KR_OFF:KR_OFF + KR_W])
    q = _dot(hq, wq_ref[...])
    kv = _dot(hkv, wkv_ref[...])
    k_w = MLA_HEADS * MLA_HEAD_W
    ones = jnp.ones((p.shape[0], MLA_HEAD_W - MLA_V), v_ref.dtype)
    for h in range(MLA_HEADS):
        sl = slice(h * MLA_HEAD_W, (h + 1) * MLA_HEAD_W)
        q_ref[0, :, sl] = rope(q[:, sl]).astype(q_ref.dtype)
        k_ref[0, :, sl] = (kv[:, sl] + k_rope).astype(k_ref.dtype)
        v_ref[0, :, h * MLA_HEAD_W:h * MLA_HEAD_W + MLA_V] = (
            kv[:, k_w + h * MLA_V:k_w + (h + 1) * MLA_V].astype(v_ref.dtype))
        v_ref[0, :, h * MLA_HEAD_W + MLA_V:(h + 1) * MLA_HEAD_W] = ones


def _mla_prep(p, gq, gkv, wq, wkv, cos_t, sin_t):
    b, t, _ = p.shape
    tm = _row_tile(t, 272)
    qk_w = MLA_HEADS * MLA_HEAD_W
    v_w = MLA_HEADS * MLA_HEAD_W
    full = lambda a: pl.BlockSpec(a.shape, lambda bi, i: (0,) * a.ndim)
    return pl.pallas_call(
        _mla_prep_kernel,
        grid=(b, t // tm),
        in_specs=[pl.BlockSpec((1, tm, MLA_IN_W), lambda bi, i: (bi, i, 0)),
                  full(gq), full(gkv), full(wq), full(wkv),
                  pl.BlockSpec((tm, MLA_HEAD_W), lambda bi, i: (i, 0)),
                  pl.BlockSpec((tm, MLA_HEAD_W), lambda bi, i: (i, 0))],
        out_specs=[pl.BlockSpec((1, tm, qk_w), lambda bi, i: (bi, i, 0)),
                   pl.BlockSpec((1, tm, qk_w), lambda bi, i: (bi, i, 0)),
                   pl.BlockSpec((1, tm, v_w), lambda bi, i: (bi, i, 0))],
        out_shape=[jax.ShapeDtypeStruct((b, t, qk_w), jnp.bfloat16),
                   jax.ShapeDtypeStruct((b, t, qk_w), jnp.bfloat16),
                   jax.ShapeDtypeStruct((b, t, v_w), jnp.bfloat16)],
        compiler_params=_cparams("parallel", "parallel"),
        name="mla_prep",
    )(p, gq, gkv, wq, wkv, cos_t, sin_t)


def _mla_attn_kernel(q_ref, k_ref, v_ref, o_ref, *, n_ctx, log2_scale):
    i = pl.program_id(2)
    tq = q_ref.shape[1]
    t = k_ref.shape[1]
    hw = MLA_HEAD_W

    def attend(nk):
        outs = []
        for hh in range(MLA_HEADS_PER_STEP):
            lanes = slice(hh * hw, (hh + 1) * hw)
            q = q_ref[0, :, lanes]
            m = acc = None
            for j0 in range(0, nk, MLA_KEY_CHUNK):
                j1 = min(j0 + MLA_KEY_CHUNK, nk)
                s = _dot_nt(q, k_ref[0, j0:j1, lanes])
                m_j = jnp.max(s, axis=-1, keepdims=True)
                m_new = m_j if m is None else jnp.maximum(m, m_j)
                e = jnp.exp2((s - m_new) * log2_scale).astype(jnp.bfloat16)
                pv = _dot(e, v_ref[0, j0:j1, lanes])
                acc = pv if acc is None else acc * jnp.exp2((m - m_new) * log2_scale) + pv
                m = m_new
            outs.append(acc[:, :MLA_V] / acc[:, MLA_V:])
        o_ref[0] = jnp.concatenate(outs, axis=1).astype(o_ref.dtype)

    @pl.when(i < n_ctx // tq)
    def _():
        attend(n_ctx)

    @pl.when(i >= n_ctx // tq)
    def _():
        attend(t)


def _mla_attn(q, k, v, n_ctx):
    b, t, _ = q.shape
    tq = 256
    hps = MLA_HEADS_PER_STEP
    assert n_ctx % tq == 0 and t % tq == 0 and MLA_HEADS % hps == 0 and MLA_HEAD_W == 2 * MLA_V
    log2_scale = float((MLA_NOPE + MLA_ROPE) ** -0.5 * np.log2(np.e))
    return pl.pallas_call(
        functools.partial(_mla_attn_kernel, n_ctx=n_ctx, log2_scale=log2_scale),
        grid=(b, MLA_HEADS // hps, t // tq),
        in_specs=[pl.BlockSpec((1, tq, hps * MLA_HEAD_W), lambda bi, h, i: (bi, i, h)),
                  pl.BlockSpec((1, t, hps * MLA_HEAD_W), lambda bi, h, i: (bi, 0, h)),
                  pl.BlockSpec((1, t, hps * MLA_HEAD_W), lambda bi, h, i: (bi, 0, h))],
        out_specs=pl.BlockSpec((1, tq, hps * MLA_V), lambda bi, h, i: (bi, i, h)),
        out_shape=jax.ShapeDtypeStruct((b, t, MLA_HEADS * MLA_V), jnp.bfloat16),
        compiler_params=_cparams("parallel", "parallel", "arbitrary"),
        name="mla_attn",
    )(q, k, v)


def _na_plan(rows):
    kr = min(NA_WIN_ROWS, rows)
    assert rows % NA_Q_ROWS == 0 and rows >= NA_KEY_ROWS
    starts, patterns, var_of_block = [], [], []
    for r in range(rows // NA_Q_ROWS):
        start = int(np.clip(NA_Q_ROWS * r - kr // 2, 0, rows - NA_KEY_ROWS))
        valid = np.zeros((NA_Q_ROWS, NA_KEY_ROWS), bool)
        dr0 = np.zeros((NA_Q_ROWS,), np.int64)
        for a in range(NA_Q_ROWS):
            i = NA_Q_ROWS * r + a
            r0 = int(np.clip(i - kr // 2, 0, rows - kr))
            assert start <= r0 and r0 + kr <= start + NA_KEY_ROWS
            krow = start + np.arange(NA_KEY_ROWS)
            valid[a] = (r0 <= krow) & (krow < r0 + kr)
            dr0[a] = start - i + (NA_WIN_ROWS - 1)
        key = (valid.tobytes(), dr0.tobytes())
        for v, (pk, _, _) in enumerate(patterns):
            if pk == key:
                var_of_block.append(v)
                break
        else:
            var_of_block.append(len(patterns))
            patterns.append((key, valid, dr0))
        starts.append(start)
    valid_r = np.stack([p[1] for p in patterns])
    dr0 = np.stack([p[2] for p in patterns])
    return np.array(starts), np.array(var_of_block), valid_r, dr0


def _na_bias_tables(rel_bias, rows):
    _, _, valid_r, dr0 = _na_plan(rows)
    depth, heads, n_dr, n_dc = rel_bias.shape
    cols = np.arange(GRID_W)
    col_start = np.clip(cols - NA_WIN_COLS // 2, 0, GRID_W - NA_WIN_COLS)
    valid_c = (cols[None, :] >= col_start[:, None]) & (cols[None, :] < col_start[:, None] + NA_WIN_COLS)
    dc = cols[None, :] - cols[:, None] + (NA_WIN_COLS - 1)
    onehot = (valid_c[:, :, None] & (dc[:, :, None] == np.arange(n_dc))).astype(np.float32)
    e1 = jnp.einsum('lhrd,jcd->lhjrc', rel_bias, jnp.asarray(onehot), precision=lax.Precision.HIGHEST)
    e1 = jnp.where(jnp.asarray(valid_c)[None, None, :, None, :], e1, NEG_INF)
    flat = e1.reshape(depth, heads, GRID_W, n_dr * GRID_W)
    n_win = NA_KEY_ROWS * GRID_W
    pad = jnp.full((depth, heads, GRID_W, n_win), NEG_INF, flat.dtype)
    flat = jnp.concatenate([pad, flat, pad], axis=-1)
    assert dr0.min() > -NA_KEY_ROWS and dr0.max() < n_dr
    blocks = []
    for v in range(valid_r.shape[0]):
        for a in range(NA_Q_ROWS):
            off = n_win + int(dr0[v, a]) * GRID_W
            keep = np.repeat(valid_r[v, a], GRID_W)
            blocks.append(jnp.where(jnp.asarray(keep), flat[..., off:off + n_win], NEG_INF))
    bm = jnp.stack(blocks, axis=2)
    return bm.reshape(depth, heads, valid_r.shape[0], NA_Q_ROWS * GRID_W, n_win)


def _na_attn_kernel(var_ref, ks_ref, q_ref, k_ref, v_ref, bm_ref, o_ref, *, n_ctx, scale):
    del var_ref
    i = pl.program_id(2)
    tq = q_ref.shape[1]
    n_win = bm_ref.shape[-1]
    per_group = LANES // NA_HEAD_DIM
    low = lax.broadcasted_iota(jnp.int32, (1, LANES), 1) < NA_HEAD_DIM
    zero = jnp.zeros((), q_ref.dtype)

    def softmax_pv(parts):
        m = None
        for s, _ in parts:
            m_s = jnp.max(s, axis=-1, keepdims=True)
            m = m_s if m is None else jnp.maximum(m, m_s)
        l = pv = None
        for s, vals in parts:
            e = jnp.exp(s - m)
            l_s = jnp.sum(e, axis=-1, keepdims=True)
            pv_s = _dot(e.astype(jnp.bfloat16), vals)
            l = l_s if l is None else l + l_s
            pv = pv_s if pv is None else pv + pv_s
        return pv / l

    def run(window):
        if window:
            ks = pl.multiple_of(ks_ref[i], GRID_W)
        for g in range(NA_HEADS_PER_STEP // per_group):
            lanes = slice(g * LANES, (g + 1) * LANES)
            q = q_ref[0, :, lanes] * scale
            k_ctx = k_ref[0, :n_ctx, lanes]
            v_ctx = v_ref[0, :n_ctx, lanes]
            outs = []
            for hh in range(per_group):
                qh = jnp.where(low if hh == 0 else jnp.logical_not(low), q, zero)
                parts = [(_dot_nt(qh, k_ctx), v_ctx)]
                if window:
                    s_w = _dot_nt(qh, k_ref[0, pl.ds(ks, n_win), lanes]) + bm_ref[g * per_group + hh, 0]
                    parts.append((s_w, v_ref[0, pl.ds(ks, n_win), lanes]))
                outs.append(softmax_pv(parts))
            o_ref[0, :, lanes] = jnp.where(low, outs[0], outs[1]).astype(o_ref.dtype)

    @pl.when(i < n_ctx // tq)
    def _():
        run(False)

    @pl.when(i >= n_ctx // tq)
    def _():
        run(True)


def _na_attn(p, bm, n_ctx):
    b, t, _ = p.shape
    tq = NA_Q_ROWS * GRID_W
    hps = NA_HEADS_PER_STEP
    gw = hps * NA_HEAD_DIM
    assert n_ctx % tq == 0 and (t - n_ctx) % tq == 0 and LANES == 2 * NA_HEAD_DIM
    assert float(np.log2(NA_HEAD_DIM ** -0.5)).is_integer()
    rows = (t - n_ctx) // GRID_W
    starts, var_of_block, _, _ = _na_plan(rows)
    n_cb = n_ctx // tq
    var_tab = jnp.asarray(np.concatenate([np.zeros(n_cb, np.int32), var_of_block]).astype(np.int32))
    ks_tab = jnp.asarray(np.concatenate([np.zeros(n_cb, np.int32), n_ctx + starts * GRID_W]).astype(np.int32))
    n_win = NA_KEY_ROWS * GRID_W
    qc, kc, vc = NAQ_OFF // gw, NAK_OFF // gw, NAV_OFF // gw
    grid_spec = pltpu.PrefetchScalarGridSpec(
        num_scalar_prefetch=2,
        grid=(b, NA_HEADS // hps, t // tq),
        in_specs=[pl.BlockSpec((1, tq, gw), lambda bi, h, i, var, ks: (bi, i, qc + h)),
                  pl.BlockSpec((1, t, gw), lambda bi, h, i, var, ks: (bi, 0, kc + h)),
                  pl.BlockSpec((1, t, gw), lambda bi, h, i, var, ks: (bi, 0, vc + h)),
                  pl.BlockSpec((hps, 1, tq, n_win), lambda bi, h, i, var, ks: (h, var[i], 0, 0))],
        out_specs=pl.BlockSpec((1, tq, gw), lambda bi, h, i, var, ks: (bi, i, h)),
    )
    return pl.pallas_call(
        functools.partial(_na_attn_kernel, n_ctx=n_ctx, scale=float(NA_HEAD_DIM ** -0.5)),
        grid_spec=grid_spec,
        out_shape=jax.ShapeDtypeStruct((b, t, NA_W), jnp.bfloat16),
        compiler_params=_cparams("parallel", "parallel", "arbitrary"),
        name="na_attn",
    )(var_tab, ks_tab, p, p, p, bm)


def _lru_kernel(x_ref, cw_ref, cb_ref, wg_ref, bg_ref, lam_ref, o_ref, xc_ref, a_ref, u_ref, *, n_ctx):
    t, wc = xc_ref.shape
    r = LRU_CHUNK
    n_chunk = t // r
    n_cc = n_ctx // r
    left = CONV_WIDTH // 2
    row = lax.broadcasted_iota(jnp.int32, (r, wc), 0)

    def conv_chunk(c, carry):
        t0 = pl.multiple_of(c * r, r)
        first = jnp.logical_or(c == 0, c == n_cc)
        last = jnp.logical_or(c == n_cc - 1, c == n_chunk - 1)
        halo = 2 * SUBLANES
        main = x_ref[0, pl.ds(t0, r), :].astype(jnp.float32)
        prev = x_ref[0, pl.ds(pl.multiple_of(jnp.maximum(t0 - halo, 0), halo), halo), :]
        nxt = x_ref[0, pl.ds(pl.multiple_of(jnp.minimum(t0 + r, t - halo), halo), halo), :]
        prev = jnp.where(first, 0.0, prev.astype(jnp.float32))
        nxt = jnp.where(last, 0.0, nxt.astype(jnp.float32))
        y = cb_ref[...] + jnp.zeros((r, wc), jnp.float32)
        for tap in range(CONV_WIDTH):
            off = tap - left
            if off == 0:
                xs = main
            elif off < 0:
                xs = pltpu.roll(main, -off, 0)
                for k in range(-off):
                    xs = jnp.where(row == k, prev[halo + off + k:halo + off + k + 1, :], xs)
            else:
                xs = pltpu.roll(main, r - off, 0)
                for k in range(off):
                    xs = jnp.where(row == r - off + k, nxt[k:k + 1, :], xs)
            y = y + xs * cw_ref[tap:tap + 1, :]
        xc_ref[pl.ds(t0, r), :] = y
        return carry

    lax.fori_loop(0, n_chunk, conv_chunk, 0)

    grow = lax.broadcasted_iota(jnp.int32, (SUBLANES, wc), 0)
    n_grp = r // SUBLANES

    def scan_direction(direction):
        reverse = direction == 1
        lam = lam_ref[direction]
        neg = -lam
        softplus = jnp.maximum(neg, 0.0) + jnp.log1p(jnp.exp(-jnp.abs(neg)))
        decay = -LRU_C * softplus

        def chunk_step(step, h):
            if reverse:
                c = jnp.where(step < n_cc, n_cc - 1 - step, n_chunk - 1 - (step - n_cc))
            else:
                c = step
            t0 = pl.multiple_of(c * r, r)
            xc = xc_ref[pl.ds(t0, r), :]
            g = _dot(xc.astype(jnp.bfloat16), wg_ref[direction, 0]) + bg_ref[direction, 0]
            rg = jax.nn.sigmoid(g[:, :wc])
            ig = jax.nn.sigmoid(g[:, wc:])
            log_a = decay * rg
            a = jnp.exp(log_a)
            a_ref[...] = a
            u_ref[...] = jnp.sqrt(jnp.maximum(1.0 - a * a, 0.0)) * (ig * xc)

            def group_step(gi, h):
                g0 = (n_grp - 1 - gi) if reverse else gi
                off = pl.multiple_of(g0 * SUBLANES, SUBLANES)
                a = a_ref[pl.ds(off, SUBLANES), :]
                u = u_ref[pl.ds(off, SUBLANES), :]
                for s in (1, 2, 4):
                    if reverse:
                        keep = grow < SUBLANES - s
                        a_s = pltpu.roll(a, SUBLANES - s, 0)
                        u_s = pltpu.roll(u, SUBLANES - s, 0)
                    else:
                        keep = grow >= s
                        a_s = pltpu.roll(a, s, 0)
                        u_s = pltpu.roll(u, s, 0)
                    u = jnp.where(keep, a * u_s + u, u)
                    a = jnp.where(keep, a * a_s, a)
                hh = a * h + u
                dst = pl.ds(pl.multiple_of(t0 + off, SUBLANES), SUBLANES)
                if reverse:
                    o_ref[0, dst, :] = o_ref[0, dst, :] + hh
                    return hh[0:1, :]
                o_ref[0, dst, :] = hh
                return hh[SUBLANES - 1:SUBLANES, :]

            return lax.fori_loop(0, n_grp, group_step, h, unroll=LRU_GROUP_UNROLL)

        lax.fori_loop(0, n_chunk, chunk_step, jnp.zeros((1, wc), jnp.float32))

    scan_direction(0)
    scan_direction(1)


def _lru(p, conv_w, conv_b, wg, bg, lam, n_ctx):
    b, t, _ = p.shape
    wc = LRU_COLS
    assert t % LRU_CHUNK == 0 and n_ctx % LRU_CHUNK == 0 and n_ctx > 0 and t > n_ctx
    xcol = LRU_OFF // wc
    return pl.pallas_call(
        functools.partial(_lru_kernel, n_ctx=n_ctx),
        grid=(b, LRU_WIDTH // wc),
        in_specs=[pl.BlockSpec((1, t, wc), lambda bi, c: (bi, 0, xcol + c)),
                  pl.BlockSpec((CONV_WIDTH, wc), lambda bi, c: (0, c)),
                  pl.BlockSpec((1, wc), lambda bi, c: (0, c)),
                  pl.BlockSpec((2, 1, wc, 2 * wc), lambda bi, c: (0, c, 0, 0)),
                  pl.BlockSpec((2, 1, 1, 2 * wc), lambda bi, c: (0, c, 0, 0)),
                  pl.BlockSpec((2, 1, wc), lambda bi, c: (0, 0, c))],
        out_specs=pl.BlockSpec((1, t, wc), lambda bi, c: (bi, 0, c)),
        out_shape=jax.ShapeDtypeStruct((b, t, LRU_WIDTH), jnp.float32),
        scratch_shapes=[pltpu.VMEM((t, wc), jnp.float32),
                        pltpu.VMEM((LRU_CHUNK, wc), jnp.float32),
                        pltpu.VMEM((LRU_CHUNK, wc), jnp.float32)],
        compiler_params=_cparams("parallel", "parallel"),
        name="rglru",
    )(p, conv_w, conv_b, wg, bg, lam)


def _merge_kernel(oa_ref, ob_ref, oc_ref, gpa_ref, gpb_ref, gpc_ref, mga_ref, mgb_ref, mgc_ref,
                  wb_ref, z_ref):
    z = None
    for n, (o_ref, gp_ref, mg_ref) in enumerate(((oa_ref, gpa_ref, mga_ref),
                                                 (ob_ref, gpb_ref, mgb_ref),
                                                 (oc_ref, gpc_ref, mgc_ref))):
        gp = gp_ref[0].astype(jnp.float32)
        tkn = (o_ref[0].astype(jnp.float32) * (gp * jax.nn.sigmoid(gp))).astype(jnp.bfloat16)
        y = jax.nn.sigmoid(mg_ref[0].astype(jnp.float32)) * _dot(tkn, wb_ref[n])
        z = y if z is None else z + y
    z_ref[0] = z.astype(z_ref.dtype)


def _merge(oa, ob, oc, p, wb):
    b, t, _ = oa.shape
    d = wb.shape[-1]
    tm = _row_tile(t, 544)
    gpc, mgc = GP_OFF // BRANCH_W, MG_OFF // d
    assert GP_OFF % BRANCH_W == 0 and MG_OFF % d == 0
    o_spec = pl.BlockSpec((1, tm, BRANCH_W), lambda bi, i: (bi, i, 0))
    gp_specs = [pl.BlockSpec((1, tm, BRANCH_W), functools.partial(lambda bi, i, n: (bi, i, gpc + n), n=n))
                for n in range(N_BRANCH)]
    mg_specs = [pl.BlockSpec((1, tm, d), functools.partial(lambda bi, i, n: (bi, i, mgc + n), n=n))
                for n in range(N_BRANCH)]
    return pl.pallas_call(
        _merge_kernel,
        grid=(b, t // tm),
        in_specs=[o_spec, o_spec, o_spec, *gp_specs, *mg_specs,
                  pl.BlockSpec(wb.shape, lambda bi, i: (0, 0, 0), pipeline_mode=pl.Buffered(1))],
        out_specs=pl.BlockSpec((1, tm, d), lambda bi, i: (bi, i, 0)),
        out_shape=jax.ShapeDtypeStruct((b, t, d), jnp.bfloat16),
        compiler_params=_cparams("parallel", "parallel"),
        name="merge",
    )(oa, ob, oc, p, p, p, p, p, p, wb)


def _outproj_kernel(z_ref, w_ref, x_ref, gl_ref, gc_ref, o_ref, *, n_ctx):
    i = pl.program_id(1)
    tm = x_ref.shape[1]
    row = i * tm + lax.broadcasted_iota(jnp.int32, (tm, 1), 0)
    gate = jnp.where(row < n_ctx, gc_ref[...], gl_ref[0])
    o_ref[0] = x_ref[0] + gate * _dot(z_ref[0], w_ref[...])


def _outproj(z, w, xa, gt_l, gt_c, n_ctx):
    b, t, d = xa.shape
    tm = _row_tile(t, 544)
    return pl.pallas_call(
        functools.partial(_outproj_kernel, n_ctx=n_ctx),
        grid=(b, t // tm),
        in_specs=[pl.BlockSpec((1, tm, d), lambda bi, i: (bi, i, 0)),
                  pl.BlockSpec((d, d), lambda bi, i: (0, 0), pipeline_mode=pl.Buffered(1)),
                  pl.BlockSpec((1, tm, d), lambda bi, i: (bi, i, 0)),
                  pl.BlockSpec((1, 1, d), lambda bi, i: (bi, 0, 0)),
                  pl.BlockSpec((1, d), lambda bi, i: (0, 0))],
        out_specs=pl.BlockSpec((1, tm, d), lambda bi, i: (bi, i, 0)),
        out_shape=jax.ShapeDtypeStruct((b, t, d), jnp.float32),
        compiler_params=_cparams("parallel", "parallel"),
        name="outproj",
    )(z, w, xa, gt_l, gt_c)


def _final_norm_kernel(x_ref, g_ref, o_ref):
    o_ref[0] = _rms(x_ref[0], g_ref[...])


def _final_norm(xa, g, n_ctx):
    b, t, d = xa.shape
    s = t - n_ctx
    tm = _row_tile(np.gcd(s, n_ctx), 512)
    skip = n_ctx // tm
    return pl.pallas_call(
        _final_norm_kernel,
        grid=(b, s // tm),
        in_specs=[pl.BlockSpec((1, tm, d), lambda bi, i: (bi, i + skip, 0)),
                  pl.BlockSpec((1, d), lambda bi, i: (0, 0))],
        out_specs=pl.BlockSpec((1, tm, d), lambda bi, i: (bi, i, 0)),
        out_shape=jax.ShapeDtypeStruct((b, s, d), jnp.float32),
        compiler_params=_cparams("parallel", "parallel"),
        name="final_norm",
    )(xa, g)


def _prep_w_in(w_in):
    kr0 = KV_OFF + MLA_KV_LORA
    kr1 = kr0 + MLA_ROPE
    w_kr = w_in[..., kr0:kr1]
    pad = jnp.zeros(w_in.shape[:-1] + (KR_W - 2 * MLA_ROPE,), w_in.dtype)
    w_mla = jnp.concatenate([w_in[..., :kr0], pad, w_kr, w_kr[..., ROPE_SWAP]], axis=-1)
    return w_mla.astype(jnp.bfloat16), w_in[..., kr1:].astype(jnp.bfloat16)


def _prep_w_uq(w_uq):
    depth, r, _ = w_uq.shape
    w = w_uq.reshape(depth, r, MLA_HEADS, MLA_NOPE + MLA_ROPE)
    rope = w[..., MLA_NOPE:]
    w = jnp.concatenate([w[..., :MLA_NOPE], rope, rope[..., ROPE_SWAP]], axis=-1)
    return w.reshape(depth, r, MLA_HEADS * MLA_HEAD_W).astype(jnp.bfloat16)


def _prep_w_ukv(w_ukv):
    depth, r, _ = w_ukv.shape
    w = w_ukv.reshape(depth, r, MLA_HEADS, MLA_NOPE + MLA_V)
    k = jnp.concatenate([w[..., :MLA_NOPE], jnp.zeros_like(w[..., :MLA_HEAD_W - MLA_NOPE])], axis=-1)
    k = k.reshape(depth, r, MLA_HEADS * MLA_HEAD_W)
    v = w[..., MLA_NOPE:].reshape(depth, r, MLA_HEADS * MLA_V)
    return jnp.concatenate([k, v], axis=-1).astype(jnp.bfloat16)


def _rope_tables(n_ctx, s):
    tok = jnp.arange(s)
    row = (tok // GRID_W).astype(jnp.float32)
    col = (tok % GRID_W).astype(jnp.float32)
    half = MLA_ROPE // 2
    inv = 1.0 / (ROPE_BASE ** (jnp.arange(0, half, 2, dtype=jnp.float32) / half))
    ang_r = row[:, None] * inv
    ang_c = col[:, None] * inv
    cos = jnp.concatenate([jnp.cos(ang_r), jnp.cos(ang_r), jnp.cos(ang_c), jnp.cos(ang_c)], axis=-1)
    sin = jnp.concatenate([-jnp.sin(ang_r), jnp.sin(ang_r), -jnp.sin(ang_c), jnp.sin(ang_c)], axis=-1)
    cos = jnp.concatenate([jnp.ones((n_ctx, MLA_ROPE), jnp.float32), cos], axis=0)
    sin = jnp.concatenate([jnp.zeros((n_ctx, MLA_ROPE), jnp.float32), sin], axis=0)
    t = n_ctx + s
    tail = MLA_HEAD_W - MLA_NOPE - MLA_ROPE
    cos_t = jnp.concatenate([jnp.ones((t, MLA_NOPE), jnp.float32), cos, jnp.zeros((t, tail), jnp.float32)], axis=-1)
    sin_t = jnp.concatenate([jnp.zeros((t, MLA_NOPE), jnp.float32), sin, jnp.zeros((t, tail), jnp.float32)], axis=-1)
    return cos_t, sin_t


def _prep_lru_gates(w_gate, b_gate):
    depth = w_gate.shape[0]
    per = LRU_COLS // LRU_BLOCK
    n_col = LRU_BLOCKS // per
    eye = jnp.eye(per, dtype=w_gate.dtype)

    def block_diag(w):
        w = w.reshape(depth, 2, n_col, per, LRU_BLOCK, LRU_BLOCK)
        w = w[:, :, :, :, :, None, :] * eye[None, None, None, :, None, :, None]
        return w.reshape(depth, 2, n_col, LRU_COLS, LRU_COLS)

    wg = jnp.concatenate([block_diag(w_gate[..., :LRU_BLOCK]), block_diag(w_gate[..., LRU_BLOCK:])], axis=-1)
    bg = b_gate.reshape(depth, 2, LRU_BLOCKS, 2 * LRU_BLOCK)
    bg = jnp.concatenate([bg[..., :LRU_BLOCK].reshape(depth, 2, n_col, 1, LRU_COLS),
                          bg[..., LRU_BLOCK:].reshape(depth, 2, n_col, 1, LRU_COLS)], axis=-1)
    return wg.astype(jnp.bfloat16), bg


def kernel(x, c, ctx, c_ctx, ada_w, ada_b, norm_g, w_in, mla_q_norm_g, mla_kv_norm_g, mla_w_uq, mla_w_ukv, na_rel_bias, lru_conv_w, lru_conv_b, lru_w_gate, lru_b_gate, lru_lambda, w_branch, w_out, final_norm_g):
    b, s, d = x.shape
    n_ctx = ctx.shape[1]
    depth = ada_w.shape[0]
    rows = s // GRID_W

    w_mla_p, w_rest_p = _prep_w_in(w_in)
    wq_p = _prep_w_uq(mla_w_uq)
    wkv_p = _prep_w_ukv(mla_w_ukv)
    cos_t, sin_t = _rope_tables(n_ctx, s)
    bm = _na_bias_tables(na_rel_bias, rows)
    wg_p, bg_p = _prep_lru_gates(lru_w_gate, lru_b_gate)
    wb_p = w_branch.astype(jnp.bfloat16)
    wo_p = w_out.astype(jnp.bfloat16)

    n_cond = -(-(b + 1) // SUBLANES) * SUBLANES
    cond = jnp.concatenate([c, c_ctx[None, :], jnp.zeros((n_cond - b - 1, d), c.dtype)], axis=0)
    mod = _ada_mod(cond, ada_w, ada_b)

    xa = jnp.concatenate([ctx, x], axis=1)
    for l in range(depth):
        sh_l, sc_l, gt_l = (mod[l, :b, k * d:(k + 1) * d][:, None, :] for k in range(3))
        sh_c, sc_c, gt_c = (mod[l, b:b + 1, k * d:(k + 1) * d] for k in range(3))
        p = _inproj(xa, norm_g[l][None, :], sc_l, sh_l, sc_c, sh_c, w_mla_p[l], w_rest_p[l], n_ctx)
        q, k, v = _mla_prep(p, mla_q_norm_g[l][None, :], mla_kv_norm_g[l][None, :],
                            wq_p[l], wkv_p[l], cos_t, sin_t)
        o_a = _mla_attn(q, k, v, n_ctx)
        o_b = _na_attn(p, bm[l], n_ctx)
        o_c = _lru(p, lru_conv_w[l], lru_conv_b[l][None, :], wg_p[l], bg_p[l],
                   lru_lambda[l][:, None, :], n_ctx)
        z = _merge(o_a, o_b, o_c, p, wb_p[l])
        xa = _outproj(z, wo_p[l], xa, gt_l, gt_c, n_ctx)
    return _final_norm(xa, final_norm_g[None, :], n_ctx)
```

```python
import functools

import numpy as np
import jax
import jax.numpy as jnp
from jax import lax
from jax.experimental import pallas as pl
from jax.experimental.pallas import tpu as pltpu

GRID_W = 64
MLA_HEADS = 8
MLA_Q_LORA = 512
MLA_KV_LORA = 256
MLA_NOPE = 128
MLA_ROPE = 64
MLA_V = 128
NA_HEADS = 16
NA_HEAD_DIM = 64
NA_WIN_ROWS = 8
NA_WIN_COLS = 16
LRU_WIDTH = 1024
LRU_BLOCKS = 16
LRU_BLOCK = LRU_WIDTH // LRU_BLOCKS
LRU_C = 8.0
CONV_WIDTH = 4
N_BRANCH = 3
BRANCH_W = 1024
ROPE_BASE = 10000.0
EPS = 1e-6

LANES = 128
SUBLANES = 8
VMEM_LIMIT_BYTES = 56 * 1024 * 1024

MLA_HEAD_W = 2 * LANES
Q_OFF = 0
KV_OFF = Q_OFF + MLA_Q_LORA
KR_OFF = KV_OFF + MLA_KV_LORA
KR_W = MLA_HEAD_W
MLA_IN_W = KR_OFF + KR_W
NA_W = NA_HEADS * NA_HEAD_DIM
NAQ_OFF = MLA_IN_W
NAK_OFF = NAQ_OFF + NA_W
NAV_OFF = NAK_OFF + NA_W
LRU_OFF = NAV_OFF + NA_W
GP_OFF = LRU_OFF + LRU_WIDTH
MG_OFF = GP_OFF + N_BRANCH * BRANCH_W
NEG_INF = -1e30

_Q = MLA_ROPE // 4
ROPE_SWAP = np.concatenate([np.arange(_Q, 2 * _Q), np.arange(0, _Q),
                            np.arange(3 * _Q, 4 * _Q), np.arange(2 * _Q, 3 * _Q)])

MLA_HEADS_PER_STEP = 2
MLA_KEY_CHUNK = 512
NA_HEADS_PER_STEP = 8
NA_Q_ROWS = 4
NA_KEY_ROWS = NA_Q_ROWS + NA_WIN_ROWS - 1
LRU_CHUNK = 256
LRU_COLS = 256
LRU_GROUP_UNROLL = 8


def _cparams(*sem):
    return pltpu.CompilerParams(dimension_semantics=sem, vmem_limit_bytes=VMEM_LIMIT_BYTES)


def _dot(a, b):
    return jnp.dot(a, b, preferred_element_type=jnp.float32)


def _dot_nt(a, b):
    return lax.dot_general(a, b, (((1,), (1,)), ((), ())), preferred_element_type=jnp.float32)


def _rms(x, g):
    return x * lax.rsqrt(jnp.mean(x * x, axis=-1, keepdims=True) + EPS) * g


def _row_tile(t, target):
    best = None
    for cand in range(2 * SUBLANES, min(t, target) + 1, 2 * SUBLANES):
        if t % cand == 0:
            best = cand
    assert best is not None
    return best


def _ada_kernel(c_ref, w_ref, b_ref, o_ref):
    c = c_ref[...]
    s = (c * jax.nn.sigmoid(c)).astype(jnp.bfloat16)
    o_ref[0] = _dot(s, w_ref[0].astype(jnp.bfloat16)) + b_ref[0]


def _ada_mod(cond, ada_w, ada_b):
    depth, d, n = ada_w.shape
    rows = cond.shape[0]
    tn = 1024
    return pl.pallas_call(
        _ada_kernel,
        grid=(depth, n // tn),
        in_specs=[pl.BlockSpec((rows, d), lambda l, j: (0, 0)),
                  pl.BlockSpec((1, d, tn), lambda l, j: (l, 0, j)),
                  pl.BlockSpec((1, 1, tn), lambda l, j: (l, 0, j))],
        out_specs=pl.BlockSpec((1, rows, tn), lambda l, j: (l, 0, j)),
        out_shape=jax.ShapeDtypeStruct((depth, rows, n), jnp.float32),
        compiler_params=_cparams("parallel", "parallel"),
        name="ada_mod",
    )(cond, ada_w, ada_b.reshape(depth, 1, n))


def _inproj_kernel(x_ref, g_ref, scl_ref, shl_ref, scc_ref, shc_ref, wa_ref, wb_ref, o_ref, h_ref,
                   *, n_ctx):
    i = pl.program_id(1)
    j = pl.program_id(2)
    tm = x_ref.shape[1]

    @pl.when(j == 0)
    def _():
        y = _rms(x_ref[0], g_ref[...])
        row = i * tm + lax.broadcasted_iota(jnp.int32, (tm, 1), 0)
        is_ctx = row < n_ctx
        sc = jnp.where(is_ctx, scc_ref[...], scl_ref[0])
        sh = jnp.where(is_ctx, shc_ref[...], shl_ref[0])
        h_ref[...] = (y * (1.0 + sc) + sh).astype(h_ref.dtype)
        o_ref[0] = _dot(h_ref[...], wa_ref[...]).astype(o_ref.dtype)

    @pl.when(j > 0)
    def _():
        o_ref[0] = _dot(h_ref[...], wb_ref[...]).astype(o_ref.dtype)


def _inproj(xa, g, sc_l, sh_l, sc_c, sh_c, w_mla, w_rest, layer, n_ctx):
    b, t, d = xa.shape
    tn = w_mla.shape[2]
    n_rest = w_rest.shape[2]
    assert n_rest % tn == 0
    tm = _row_tile(t, 1088)
    vec = pl.BlockSpec((1, d), lambda bi, i, j: (0, 0))
    per_b = pl.BlockSpec((1, 1, d), lambda bi, i, j: (bi, 0, 0))
    return pl.pallas_call(
        functools.partial(_inproj_kernel, n_ctx=n_ctx),
        grid=(b, t // tm, 1 + n_rest // tn),
        in_specs=[pl.BlockSpec((1, tm, d), lambda bi, i, j: (bi, i, 0)),
                  vec, per_b, per_b, vec, vec,
                  pl.BlockSpec((None, d, tn), lambda bi, i, j: (layer, 0, 0)),
                  pl.BlockSpec((None, d, tn), lambda bi, i, j: (layer, 0, jnp.maximum(j - 1, 0)))],
        out_specs=pl.BlockSpec((1, tm, tn), lambda bi, i, j: (bi, i, j)),
        out_shape=jax.ShapeDtypeStruct((b, t, tn + n_rest), jnp.bfloat16),
        scratch_shapes=[pltpu.VMEM((tm, d), jnp.bfloat16)],
        compiler_params=_cparams("parallel", "parallel", "arbitrary"),
        name="inproj",
    )(xa, g, sc_l, sh_l, sc_c, sh_c, w_mla, w_rest)


def _mla_prep_kernel(p_ref, gq_ref, gkv_ref, wq_ref, wkv_ref, cos_ref, sin_ref,
                     q_ref, k_ref, v_ref):
    p = p_ref[0].astype(jnp.float32)
    cos = cos_ref[...]
    sin = sin_ref[...]
    shift = MLA_HEAD_W - MLA_ROPE

    def rope(a):
        return a * cos + pltpu.roll(a, shift, 1) * sin

    hq = _rms(p[:, Q_OFF:Q_OFF + MLA_Q_LORA], gq_ref[...]).astype(jnp.bfloat16)
    hkv = _rms(p[:, KV_OFF:KV_OFF + MLA_KV_LORA], gkv_ref[...]).astype(jnp.bfloat16)
    k_rope = rope(p[:, KR_OFF:KR_OFF + KR_W])
    q = _dot(hq, wq_ref[...])
    kv = _dot(hkv, wkv_ref[...])
    k_w = MLA_HEADS * MLA_HEAD_W
    ones = jnp.ones((p.shape[0], MLA_HEAD_W - MLA_V), v_ref.dtype)
    for h in range(MLA_HEADS):
        sl = slice(h * MLA_HEAD_W, (h + 1) * MLA_HEAD_W)
        q_ref[0, :, sl] = rope(q[:, sl]).astype(q_ref.dtype)
        k_ref[0, :, sl] = (kv[:, sl] + k_rope).astype(k_ref.dtype)
        v_ref[0, :, h * MLA_HEAD_W:h * MLA_HEAD_W + MLA_V] = (
            kv[:, k_w + h * MLA_V:k_w + (h + 1) * MLA_V].astype(v_ref.dtype))
        v_ref[0, :, h * MLA_HEAD_W + MLA_V:(h + 1) * MLA_HEAD_W] = ones


def _mla_prep(p, gq, gkv, wq, wkv, cos_t, sin_t):
    b, t, _ = p.shape
    tm = _row_tile(t, 272)
    qk_w = MLA_HEADS * MLA_HEAD_W
    v_w = MLA_HEADS * MLA_HEAD_W
    full = lambda a: pl.BlockSpec(a.shape, lambda bi, i: (0,) * a.ndim)
    return pl.pallas_call(
        _mla_prep_kernel,
        grid=(b, t // tm),
        in_specs=[pl.BlockSpec((1, tm, MLA_IN_W), lambda bi, i: (bi, i, 0)),
                  full(gq), full(gkv), full(wq), full(wkv),
                  pl.BlockSpec((tm, MLA_HEAD_W), lambda bi, i: (i, 0)),
                  pl.BlockSpec((tm, MLA_HEAD_W), lambda bi, i: (i, 0))],
        out_specs=[pl.BlockSpec((1, tm, qk_w), lambda bi, i: (bi, i, 0)),
                   pl.BlockSpec((1, tm, qk_w), lambda bi, i: (bi, i, 0)),
                   pl.BlockSpec((1, tm, v_w), lambda bi, i: (bi, i, 0))],
        out_shape=[jax.ShapeDtypeStruct((b, t, qk_w), jnp.bfloat16),
                   jax.ShapeDtypeStruct((b, t, qk_w), jnp.bfloat16),
                   jax.ShapeDtypeStruct((b, t, v_w), jnp.bfloat16)],
        compiler_params=_cparams("parallel", "parallel"),
        name="mla_prep",
    )(p, gq, gkv, wq, wkv, cos_t, sin_t)


def _mla_attn_kernel(q_ref, k_ref, v_ref, o_ref, *, n_ctx, log2_scale):
    i = pl.program_id(2)
    tq = q_ref.shape[1]
    t = k_ref.shape[1]
    hw = MLA_HEAD_W

    def attend(nk):
        outs = []
        for hh in range(MLA_HEADS_PER_STEP):
            lanes = slice(hh * hw, (hh + 1) * hw)
            q = q_ref[0, :, lanes]
            m = acc = None
            for j0 in range(0, nk, MLA_KEY_CHUNK):
                j1 = min(j0 + MLA_KEY_CHUNK, nk)
                s = _dot_nt(q, k_ref[0, j0:j1, lanes])
                m_j = jnp.max(s, axis=-1, keepdims=True)
                m_new = m_j if m is None else jnp.maximum(m, m_j)
                e = jnp.exp2((s - m_new) * log2_scale).astype(jnp.bfloat16)
                pv = _dot(e, v_ref[0, j0:j1, lanes])
                acc = pv if acc is None else acc * jnp.exp2((m - m_new) * log2_scale) + pv
                m = m_new
            outs.append(acc[:, :MLA_V] / acc[:, MLA_V:])
        o_ref[0] = jnp.concatenate(outs, axis=1).astype(o_ref.dtype)

    @pl.when(i < n_ctx // tq)
    def _():
        attend(n_ctx)

    @pl.when(i >= n_ctx // tq)
    def _():
        attend(t)


def _mla_attn(q, k, v, n_ctx):
    b, t, _ = q.shape
    tq = 256
    hps = MLA_HEADS_PER_STEP
    assert n_ctx % tq == 0 and t % tq == 0 and MLA_HEADS % hps == 0 and MLA_HEAD_W == 2 * MLA_V
    log2_scale = float((MLA_NOPE + MLA_ROPE) ** -0.5 * np.log2(np.e))
    return pl.pallas_call(
        functools.partial(_mla_attn_kernel, n_ctx=n_ctx, log2_scale=log2_scale),
        grid=(b, MLA_HEADS // hps, t // tq),
        in_specs=[pl.BlockSpec((1, tq, hps * MLA_HEAD_W), lambda bi, h, i: (bi, i, h)),
                  pl.BlockSpec((1, t, hps * MLA_HEAD_W), lambda bi, h, i: (bi, 0, h)),
                  pl.BlockSpec((1, t, hps * MLA_HEAD_W), lambda bi, h, i: (bi, 0, h))],
        out_specs=pl.BlockSpec((1, tq, hps * MLA_V), lambda bi, h, i: (bi, i, h)),
        out_shape=jax.ShapeDtypeStruct((b, t, MLA_HEADS * MLA_V), jnp.bfloat16),
        compiler_params=_cparams("parallel", "parallel", "arbitrary"),
        name="mla_attn",
    )(q, k, v)


def _na_plan(rows):
    kr = min(NA_WIN_ROWS, rows)
    assert rows % NA_Q_ROWS == 0 and rows >= NA_KEY_ROWS
    starts, patterns, var_of_block = [], [], []
    for r in range(rows // NA_Q_ROWS):
        start = int(np.clip(NA_Q_ROWS * r - kr // 2, 0, rows - NA_KEY_ROWS))
        valid = np.zeros((NA_Q_ROWS, NA_KEY_ROWS), bool)
        dr0 = np.zeros((NA_Q_ROWS,), np.int64)
        for a in range(NA_Q_ROWS):
            i = NA_Q_ROWS * r + a
            r0 = int(np.clip(i - kr // 2, 0, rows - kr))
            assert start <= r0 and r0 + kr <= start + NA_KEY_ROWS
            krow = start + np.arange(NA_KEY_ROWS)
            valid[a] = (r0 <= krow) & (krow < r0 + kr)
            dr0[a] = start - i + (NA_WIN_ROWS - 1)
        key = (valid.tobytes(), dr0.tobytes())
        for v, (pk, _, _) in enumerate(patterns):
            if pk == key:
                var_of_block.append(v)
                break
        else:
            var_of_block.append(len(patterns))
            patterns.append((key, valid, dr0))
        starts.append(start)
    valid_r = np.stack([p[1] for p in patterns])
    dr0 = np.stack([p[2] for p in patterns])
    return np.array(starts), np.array(var_of_block), valid_r, dr0


def _na_bias_tables(rel_bias, rows):
    _, _, valid_r, dr0 = _na_plan(rows)
    depth, heads, n_dr, n_dc = rel_bias.shape
    cols = np.arange(GRID_W)
    col_start = np.clip(cols - NA_WIN_COLS // 2, 0, GRID_W - NA_WIN_COLS)
    valid_c = (cols[None, :] >= col_start[:, None]) & (cols[None, :] < col_start[:, None] + NA_WIN_COLS)
    dc = cols[None, :] - cols[:, None] + (NA_WIN_COLS - 1)
    onehot = (valid_c[:, :, None] & (dc[:, :, None] == np.arange(n_dc))).astype(np.float32)
    e1 = jnp.einsum('lhrd,jcd->lhjrc', rel_bias, jnp.asarray(onehot), precision=lax.Precision.HIGHEST)
    e1 = jnp.where(jnp.asarray(valid_c)[None, None, :, None, :], e1, NEG_INF)
    flat = e1.reshape(depth, heads, GRID_W, n_dr * GRID_W)
    n_win = NA_KEY_ROWS * GRID_W
    pad = jnp.full((depth, heads, GRID_W, n_win), NEG_INF, flat.dtype)
    flat = jnp.concatenate([pad, flat, pad], axis=-1)
    assert dr0.min() > -NA_KEY_ROWS and dr0.max() < n_dr
    blocks = []
    for v in range(valid_r.shape[0]):
        for a in range(NA_Q_ROWS):
            off = n_win + int(dr0[v, a]) * GRID_W
            keep = np.repeat(valid_r[v, a], GRID_W)
            blocks.append(jnp.where(jnp.asarray(keep), flat[..., off:off + n_win], NEG_INF))
    bm = jnp.stack(blocks, axis=2)
    return bm.reshape(depth, heads, valid_r.shape[0], NA_Q_ROWS * GRID_W, n_win)


def _na_attn_kernel(var_ref, ks_ref, q_ref, k_ref, v_ref, bm_ref, o_ref, *, n_ctx, scale):
    del var_ref
    i = pl.program_id(2)
    tq = q_ref.shape[1]
    n_win = bm_ref.shape[-1]
    per_group = LANES // NA_HEAD_DIM
    low = lax.broadcasted_iota(jnp.int32, (1, LANES), 1) < NA_HEAD_DIM
    zero = jnp.zeros((), q_ref.dtype)

    def softmax_pv(parts):
        m = None
        for s, _ in parts:
            m_s = jnp.max(s, axis=-1, keepdims=True)
            m = m_s if m is None else jnp.maximum(m, m_s)
        l = pv = None
        for s, vals in parts:
            e = jnp.exp(s - m)
            l_s = jnp.sum(e, axis=-1, keepdims=True)
            pv_s = _dot(e.astype(jnp.bfloat16), vals)
            l = l_s if l is None else l + l_s
            pv = pv_s if pv is None else pv + pv_s
        return pv / l

    def run(window):
        if window:
            ks = pl.multiple_of(ks_ref[i], GRID_W)
        for g in range(NA_HEADS_PER_STEP // per_group):
            lanes = slice(g * LANES, (g + 1) * LANES)
            q = q_ref[0, :, lanes] * scale
            k_ctx = k_ref[0, :n_ctx, lanes]
            v_ctx = v_ref[0, :n_ctx, lanes]
            outs = []
            for hh in range(per_group):
                qh = jnp.where(low if hh == 0 else jnp.logical_not(low), q, zero)
                parts = [(_dot_nt(qh, k_ctx), v_ctx)]
                if window:
                    s_w = _dot_nt(qh, k_ref[0, pl.ds(ks, n_win), lanes]) + bm_ref[g * per_group + hh, 0]
                    parts.append((s_w, v_ref[0, pl.ds(ks, n_win), lanes]))
                outs.append(softmax_pv(parts))
            o_ref[0, :, lanes] = jnp.where(low, outs[0], outs[1]).astype(o_ref.dtype)

    @pl.when(i < n_ctx // tq)
    def _():
        run(False)

    @pl.when(i >= n_ctx // tq)
    def _():
        run(True)


def _na_attn(p, bm, layer, n_ctx):
    b, t, _ = p.shape
    tq = NA_Q_ROWS * GRID_W
    hps = NA_HEADS_PER_STEP
    gw = hps * NA_HEAD_DIM
    assert n_ctx % tq == 0 and (t - n_ctx) % tq == 0 and LANES == 2 * NA_HEAD_DIM
    assert float(np.log2(NA_HEAD_DIM ** -0.5)).is_integer()
    rows = (t - n_ctx) // GRID_W
    starts, var_of_block, _, _ = _na_plan(rows)
    n_cb = n_ctx // tq
    var_tab = jnp.asarray(np.concatenate([np.zeros(n_cb, np.int32), var_of_block]).astype(np.int32))
    ks_tab = jnp.asarray(np.concatenate([np.zeros(n_cb, np.int32), n_ctx + starts * GRID_W]).astype(np.int32))
    n_win = NA_KEY_ROWS * GRID_W
    qc, kc, vc = NAQ_OFF // gw, NAK_OFF // gw, NAV_OFF // gw
    grid_spec = pltpu.PrefetchScalarGridSpec(
        num_scalar_prefetch=2,
        grid=(b, NA_HEADS // hps, t // tq),
        in_specs=[pl.BlockSpec((1, tq, gw), lambda bi, h, i, var, ks: (bi, i, qc + h)),
                  pl.BlockSpec((1, t, gw), lambda bi, h, i, var, ks: (bi, 0, kc + h)),
                  pl.BlockSpec((1, t, gw), lambda bi, h, i, var, ks: (bi, 0, vc + h)),
                  pl.BlockSpec((None, hps, 1, tq, n_win),
                               lambda bi, h, i, var, ks: (layer, h, var[i], 0, 0))],
        out_specs=pl.BlockSpec((1, tq, gw), lambda bi, h, i, var, ks: (bi, i, h)),
    )
    return pl.pallas_call(
        functools.partial(_na_attn_kernel, n_ctx=n_ctx, scale=float(NA_HEAD_DIM ** -0.5)),
        grid_spec=grid_spec,
        out_shape=jax.ShapeDtypeStruct((b, t, NA_W), jnp.bfloat16),
        compiler_params=_cparams("parallel", "parallel", "arbitrary"),
        name="na_attn",
    )(var_tab, ks_tab, p, p, p, bm)


def _lru_kernel(x_ref, cw_ref, cb_ref, wg_ref, bg_ref, lam_ref, o_ref, xc_ref, a_ref, u_ref, *, n_ctx):
    t, wc = xc_ref.shape
    r = LRU_CHUNK
    n_chunk = t // r
    n_cc = n_ctx // r
    left = CONV_WIDTH // 2
    row = lax.broadcasted_iota(jnp.int32, (r, wc), 0)

    def conv_chunk(c, carry):
        t0 = pl.multiple_of(c * r, r)
        first = jnp.logical_or(c == 0, c == n_cc)
        last = jnp.logical_or(c == n_cc - 1, c == n_chunk - 1)
        halo = 2 * SUBLANES
        main = x_ref[0, pl.ds(t0, r), :].astype(jnp.float32)
        prev = x_ref[0, pl.ds(pl.multiple_of(jnp.maximum(t0 - halo, 0), halo), halo), :]
        nxt = x_ref[0, pl.ds(pl.multiple_of(jnp.minimum(t0 + r, t - halo), halo), halo), :]
        prev = jnp.where(first, 0.0, prev.astype(jnp.float32))
        nxt = jnp.where(last, 0.0, nxt.astype(jnp.float32))
        y = cb_ref[...] + jnp.zeros((r, wc), jnp.float32)
        for tap in range(CONV_WIDTH):
            off = tap - left
            if off == 0:
                xs = main
            elif off < 0:
                xs = pltpu.roll(main, -off, 0)
                for k in range(-off):
                    xs = jnp.where(row == k, prev[halo + off + k:halo + off + k + 1, :], xs)
            else:
                xs = pltpu.roll(main, r - off, 0)
                for k in range(off):
                    xs = jnp.where(row == r - off + k, nxt[k:k + 1, :], xs)
            y = y + xs * cw_ref[tap:tap + 1, :]
        xc_ref[pl.ds(t0, r), :] = y
        return carry

    lax.fori_loop(0, n_chunk, conv_chunk, 0)

    grow = lax.broadcasted_iota(jnp.int32, (SUBLANES, wc), 0)
    n_grp = r // SUBLANES

    def scan_direction(direction):
        reverse = direction == 1
        lam = lam_ref[direction]
        neg = -lam
        softplus = jnp.maximum(neg, 0.0) + jnp.log1p(jnp.exp(-jnp.abs(neg)))
        decay = -LRU_C * softplus

        def chunk_step(step, h):
            if reverse:
                c = jnp.where(step < n_cc, n_cc - 1 - step, n_chunk - 1 - (step - n_cc))
            else:
                c = step
            t0 = pl.multiple_of(c * r, r)
            xc = xc_ref[pl.ds(t0, r), :]
            g = _dot(xc.astype(jnp.bfloat16), wg_ref[direction, 0]) + bg_ref[direction, 0]
            rg = jax.nn.sigmoid(g[:, :wc])
            ig = jax.nn.sigmoid(g[:, wc:])
            log_a = decay * rg
            a = jnp.exp(log_a)
            a_ref[...] = a
            v = jnp.maximum(1.0 - a * a, 0.0)
            u_ref[...] = jnp.where(v > 0.0, v * lax.rsqrt(v), 0.0) * (ig * xc)

            def group_step(gi, h):
                g0 = (n_grp - 1 - gi) if reverse else gi
                off = pl.multiple_of(g0 * SUBLANES, SUBLANES)
                a = a_ref[pl.ds(off, SUBLANES), :]
                u = u_ref[pl.ds(off, SUBLANES), :]
                for s in (1, 2, 4):
                    if reverse:
                        keep = grow < SUBLANES - s
                        a_s = pltpu.roll(a, SUBLANES - s, 0)
                        u_s = pltpu.roll(u, SUBLANES - s, 0)
                    else:
                        keep = grow >= s
                        a_s = pltpu.roll(a, s, 0)
                        u_s = pltpu.roll(u, s, 0)
                    u = jnp.where(keep, a * u_s + u, u)
                    a = jnp.where(keep, a * a_s, a)
                hh = a * h + u
                dst = pl.ds(pl.multiple_of(t0 + off, SUBLANES), SUBLANES)
                if reverse:
                    o_ref[0, dst, :] = o_ref[0, dst, :] + hh
                    return hh[0:1, :]
                o_ref[0, dst, :] = hh
                return hh[SUBLANES - 1:SUBLANES, :]

            return lax.fori_loop(0, n_grp, group_step, h, unroll=LRU_GROUP_UNROLL)

        lax.fori_loop(0, n_chunk, chunk_step, jnp.zeros((1, wc), jnp.float32))

    scan_direction(0)
    scan_direction(1)


def _lru(p, conv_w, conv_b, wg, bg, lam, n_ctx):
    b, t, _ = p.shape
    wc = LRU_COLS
    assert t % LRU_CHUNK == 0 and n_ctx % LRU_CHUNK == 0 and n_ctx > 0 and t > n_ctx
    xcol = LRU_OFF // wc
    return pl.pallas_call(
        functools.partial(_lru_kernel, n_ctx=n_ctx),
        grid=(b, LRU_WIDTH // wc),
        in_specs=[pl.BlockSpec((1, t, wc), lambda bi, c: (bi, 0, xcol + c)),
                  pl.BlockSpec((CONV_WIDTH, wc), lambda bi, c: (0, c)),
                  pl.BlockSpec((1, wc), lambda bi, c: (0, c)),
                  pl.BlockSpec((2, 1, wc, 2 * wc), lambda bi, c: (0, c, 0, 0)),
                  pl.BlockSpec((2, 1, 1, 2 * wc), lambda bi, c: (0, c, 0, 0)),
                  pl.BlockSpec((2, 1, wc), lambda bi, c: (0, 0, c))],
        out_specs=pl.BlockSpec((1, t, wc), lambda bi, c: (bi, 0, c)),
        out_shape=jax.ShapeDtypeStruct((b, t, LRU_WIDTH), jnp.float32),
        scratch_shapes=[pltpu.VMEM((t, wc), jnp.float32),
                        pltpu.VMEM((LRU_CHUNK, wc), jnp.float32),
                        pltpu.VMEM((LRU_CHUNK, wc), jnp.float32)],
        compiler_params=_cparams("parallel", "parallel"),
        name="rglru",
    )(p, conv_w, conv_b, wg, bg, lam)


def _merge_kernel(oa_ref, ob_ref, oc_ref, gpa_ref, gpb_ref, gpc_ref, mga_ref, mgb_ref, mgc_ref,
                  wb_ref, z_ref):
    z = None
    for n, (o_ref, gp_ref, mg_ref) in enumerate(((oa_ref, gpa_ref, mga_ref),
                                                 (ob_ref, gpb_ref, mgb_ref),
                                                 (oc_ref, gpc_ref, mgc_ref))):
        gp = gp_ref[0].astype(jnp.float32)
        tkn = (o_ref[0].astype(jnp.float32) * (gp * jax.nn.sigmoid(gp))).astype(jnp.bfloat16)
        y = jax.nn.sigmoid(mg_ref[0].astype(jnp.float32)) * _dot(tkn, wb_ref[n])
        z = y if z is None else z + y
    z_ref[0] = z.astype(z_ref.dtype)


def _merge(oa, ob, oc, p, wb, layer):
    b, t, _ = oa.shape
    d = wb.shape[-1]
    tm = _row_tile(t, 544)
    gpc, mgc = GP_OFF // BRANCH_W, MG_OFF // d
    assert GP_OFF % BRANCH_W == 0 and MG_OFF % d == 0
    o_spec = pl.BlockSpec((1, tm, BRANCH_W), lambda bi, i: (bi, i, 0))
    gp_specs = [pl.BlockSpec((1, tm, BRANCH_W), functools.partial(lambda bi, i, n: (bi, i, gpc + n), n=n))
                for n in range(N_BRANCH)]
    mg_specs = [pl.BlockSpec((1, tm, d), functools.partial(lambda bi, i, n: (bi, i, mgc + n), n=n))
                for n in range(N_BRANCH)]
    return pl.pallas_call(
        _merge_kernel,
        grid=(b, t // tm),
        in_specs=[o_spec, o_spec, o_spec, *gp_specs, *mg_specs,
                  pl.BlockSpec((None,) + wb.shape[1:], lambda bi, i: (layer, 0, 0, 0),
                               pipeline_mode=pl.Buffered(1))],
        out_specs=pl.BlockSpec((1, tm, d), lambda bi, i: (bi, i, 0)),
        out_shape=jax.ShapeDtypeStruct((b, t, d), jnp.bfloat16),
        compiler_params=_cparams("parallel", "parallel"),
        name="merge",
    )(oa, ob, oc, p, p, p, p, p, p, wb)


def _outproj_kernel(z_ref, w_ref, x_ref, gl_ref, gc_ref, o_ref, *, n_ctx):
    i = pl.program_id(1)
    tm = x_ref.shape[1]
    row = i * tm + lax.broadcasted_iota(jnp.int32, (tm, 1), 0)
    gate = jnp.where(row < n_ctx, gc_ref[...], gl_ref[0])
    o_ref[0] = x_ref[0] + gate * _dot(z_ref[0], w_ref[...])


def _outproj(z, w, xa, gt_l, gt_c, layer, n_ctx):
    b, t, d = xa.shape
    tm = _row_tile(t, 544)
    return pl.pallas_call(
        functools.partial(_outproj_kernel, n_ctx=n_ctx),
        grid=(b, t // tm),
        in_specs=[pl.BlockSpec((1, tm, d), lambda bi, i: (bi, i, 0)),
                  pl.BlockSpec((None, d, d), lambda bi, i: (layer, 0, 0), pipeline_mode=pl.Buffered(1)),
                  pl.BlockSpec((1, tm, d), lambda bi, i: (bi, i, 0)),
                  pl.BlockSpec((1, 1, d), lambda bi, i: (bi, 0, 0)),
                  pl.BlockSpec((1, d), lambda bi, i: (0, 0))],
        out_specs=pl.BlockSpec((1, tm, d), lambda bi, i: (bi, i, 0)),
        out_shape=jax.ShapeDtypeStruct((b, t, d), jnp.float32),
        compiler_params=_cparams("parallel", "parallel"),
        name="outproj",
    )(z, w, xa, gt_l, gt_c)


def _outproj_final_kernel(z_ref, w_ref, x_ref, gl_ref, g_ref, o_ref):
    o_ref[0] = _rms(x_ref[0] + gl_ref[0] * _dot(z_ref[0], w_ref[...]), g_ref[...])


def _outproj_final(z, w, xa, gt_l, g, layer, n_ctx):
    b, t, d = xa.shape
    s = t - n_ctx
    tm = _row_tile(int(np.gcd(s, n_ctx)), 512)
    skip = n_ctx // tm
    return pl.pallas_call(
        _outproj_final_kernel,
        grid=(b, s // tm),
        in_specs=[pl.BlockSpec((1, tm, d), lambda bi, i: (bi, i + skip, 0)),
                  pl.BlockSpec((None, d, d), lambda bi, i: (layer, 0, 0), pipeline_mode=pl.Buffered(1)),
                  pl.BlockSpec((1, tm, d), lambda bi, i: (bi, i + skip, 0)),
                  pl.BlockSpec((1, 1, d), lambda bi, i: (bi, 0, 0)),
                  pl.BlockSpec((1, d), lambda bi, i: (0, 0))],
        out_specs=pl.BlockSpec((1, tm, d), lambda bi, i: (bi, i, 0)),
        out_shape=jax.ShapeDtypeStruct((b, s, d), jnp.float32),
        compiler_params=_cparams("parallel", "parallel"),
        name="outproj_final",
    )(z, w, xa, gt_l, g)


def _prep_w_in(w_in):
    kr0 = KV_OFF + MLA_KV_LORA
    kr1 = kr0 + MLA_ROPE
    w_kr = w_in[..., kr0:kr1]
    pad = jnp.zeros(w_in.shape[:-1] + (KR_W - 2 * MLA_ROPE,), w_in.dtype)
    w_mla = jnp.concatenate([w_in[..., :kr0], pad, w_kr, w_kr[..., ROPE_SWAP]], axis=-1)
    return w_mla.astype(jnp.bfloat16), w_in[..., kr1:].astype(jnp.bfloat16)


def _prep_w_uq(w_uq):
    depth, r, _ = w_uq.shape
    w = w_uq.reshape(depth, r, MLA_HEADS, MLA_NOPE + MLA_ROPE)
    rope = w[..., MLA_NOPE:]
    w = jnp.concatenate([w[..., :MLA_NOPE], rope, rope[..., ROPE_SWAP]], axis=-1)
    return w.reshape(depth, r, MLA_HEADS * MLA_HEAD_W).astype(jnp.bfloat16)


def _prep_w_ukv(w_ukv):
    depth, r, _ = w_ukv.shape
    w = w_ukv.reshape(depth, r, MLA_HEADS, MLA_NOPE + MLA_V)
    k = jnp.concatenate([w[..., :MLA_NOPE], jnp.zeros_like(w[..., :MLA_HEAD_W - MLA_NOPE])], axis=-1)
    k = k.reshape(depth, r, MLA_HEADS * MLA_HEAD_W)
    v = w[..., MLA_NOPE:].reshape(depth, r, MLA_HEADS * MLA_V)
    return jnp.concatenate([k, v], axis=-1).astype(jnp.bfloat16)


def _rope_tables(n_ctx, s):
    tok = jnp.arange(s)
    row = (tok // GRID_W).astype(jnp.float32)
    col = (tok % GRID_W).astype(jnp.float32)
    half = MLA_ROPE // 2
    inv = 1.0 / (ROPE_BASE ** (jnp.arange(0, half, 2, dtype=jnp.float32) / half))
    ang_r = row[:, None] * inv
    ang_c = col[:, None] * inv
    cos = jnp.concatenate([jnp.cos(ang_r), jnp.cos(ang_r), jnp.cos(ang_c), jnp.cos(ang_c)], axis=-1)
    sin = jnp.concatenate([-jnp.sin(ang_r), jnp.sin(ang_r), -jnp.sin(ang_c), jnp.sin(ang_c)], axis=-1)
    cos = jnp.concatenate([jnp.ones((n_ctx, MLA_ROPE), jnp.float32), cos], axis=0)
    sin = jnp.concatenate([jnp.zeros((n_ctx, MLA_ROPE), jnp.float32), sin], axis=0)
    t = n_ctx + s
    tail = MLA_HEAD_W - MLA_NOPE - MLA_ROPE
    cos_t = jnp.concatenate([jnp.ones((t, MLA_NOPE), jnp.float32), cos, jnp.zeros((t, tail), jnp.float32)], axis=-1)
    sin_t = jnp.concatenate([jnp.zeros((t, MLA_NOPE), jnp.float32), sin, jnp.zeros((t, tail), jnp.float32)], axis=-1)
    return cos_t, sin_t


def _prep_lru_gates(w_gate, b_gate):
    depth = w_gate.shape[0]
    per = LRU_COLS // LRU_BLOCK
    n_col = LRU_BLOCKS // per
    eye = jnp.eye(per, dtype=w_gate.dtype)

    def block_diag(w):
        w = w.reshape(depth, 2, n_col, per, LRU_BLOCK, LRU_BLOCK)
        w = w[:, :, :, :, :, None, :] * eye[None, None, None, :, None, :, None]
        return w.reshape(depth, 2, n_col, LRU_COLS, LRU_COLS)

    wg = jnp.concatenate([block_diag(w_gate[..., :LRU_BLOCK]), block_diag(w_gate[..., LRU_BLOCK:])], axis=-1)
    bg = b_gate.reshape(depth, 2, LRU_BLOCKS, 2 * LRU_BLOCK)
    bg = jnp.concatenate([bg[..., :LRU_BLOCK].reshape(depth, 2, n_col, 1, LRU_COLS),
                          bg[..., LRU_BLOCK:].reshape(depth, 2, n_col, 1, LRU_COLS)], axis=-1)
    return wg.astype(jnp.bfloat16), bg


def kernel(x, c, ctx, c_ctx, ada_w, ada_b, norm_g, w_in, mla_q_norm_g, mla_kv_norm_g, mla_w_uq, mla_w_ukv, na_rel_bias, lru_conv_w, lru_conv_b, lru_w_gate, lru_b_gate, lru_lambda, w_branch, w_out, final_norm_g):
    b, s, d = x.shape
    n_ctx = ctx.shape[1]
    depth = ada_w.shape[0]
    rows = s // GRID_W

    w_mla_p, w_rest_p = _prep_w_in(w_in)
    wq_p = _prep_w_uq(mla_w_uq)
    wkv_p = _prep_w_ukv(mla_w_ukv)
    cos_t, sin_t = _rope_tables(n_ctx, s)
    bm = _na_bias_tables(na_rel_bias, rows)
    wg_p, bg_p = _prep_lru_gates(lru_w_gate, lru_b_gate)
    wb_p = w_branch.astype(jnp.bfloat16)
    wo_p = w_out.astype(jnp.bfloat16)

    n_cond = -(-(b + 1) // SUBLANES) * SUBLANES
    cond = jnp.concatenate([c, c_ctx[None, :], jnp.zeros((n_cond - b - 1, d), c.dtype)], axis=0)
    mod = _ada_mod(cond, ada_w, ada_b)

    xa = jnp.concatenate([ctx, x], axis=1)
    for l in range(depth):
        sh_l, sc_l, gt_l = (mod[l, :b, k * d:(k + 1) * d][:, None, :] for k in range(3))
        sh_c, sc_c, gt_c = (mod[l, b:b + 1, k * d:(k + 1) * d] for k in range(3))
        p = _inproj(xa, norm_g[l][None, :], sc_l, sh_l, sc_c, sh_c, w_mla_p, w_rest_p, l, n_ctx)
        q, k, v = _mla_prep(p, mla_q_norm_g[l][None, :], mla_kv_norm_g[l][None, :],
                            wq_p[l], wkv_p[l], cos_t, sin_t)
        o_a = _mla_attn(q, k, v, n_ctx)
        o_b = _na_attn(p, bm, l, n_ctx)
        o_c = _lru(p, lru_conv_w[l], lru_conv_b[l][None, :], wg_p[l], bg_p[l],
                   lru_lambda[l][:, None, :], n_ctx)
        z = _merge(o_a, o_b, o_c, p, wb_p, l)
        if l < depth - 1:
            xa = _outproj(z, wo_p, xa, gt_l, gt_c, l, n_ctx)
    return _outproj_final(z, wo_p, xa, gt_l, final_norm_g[None, :], depth - 1, n_ctx)
```

```python
import functools

import numpy as np
import jax
import jax.numpy as jnp
from jax import lax
from jax.experimental import pallas as pl
from jax.experimental.pallas import tpu as pltpu

GRID_W = 64
MLA_HEADS = 8
MLA_Q_LORA = 512
MLA_KV_LORA = 256
MLA_NOPE = 128
MLA_ROPE = 64
MLA_V = 128
NA_HEADS = 16
NA_HEAD_DIM = 64
NA_WIN_ROWS = 8
NA_WIN_COLS = 16
LRU_WIDTH = 1024
LRU_BLOCKS = 16
LRU_BLOCK = LRU_WIDTH // LRU_BLOCKS
LRU_C = 8.0
CONV_WIDTH = 4
N_BRANCH = 3
BRANCH_W = 1024
ROPE_BASE = 10000.0
EPS = 1e-6

LANES = 128
SUBLANES = 8
VMEM_LIMIT_BYTES = 56 * 1024 * 1024

MLA_HEAD_W = 2 * LANES
Q_OFF = 0
KV_OFF = Q_OFF + MLA_Q_LORA
KR_OFF = KV_OFF + MLA_KV_LORA
KR_W = MLA_HEAD_W
MLA_IN_W = KR_OFF + KR_W
NA_W = NA_HEADS * NA_HEAD_DIM
NAQ_OFF = MLA_IN_W
NAK_OFF = NAQ_OFF + NA_W
NAV_OFF = NAK_OFF + NA_W
LRU_OFF = NAV_OFF + NA_W
GP_OFF = LRU_OFF + LRU_WIDTH
MG_OFF = GP_OFF + N_BRANCH * BRANCH_W
NEG_INF = -1e30

_Q = MLA_ROPE // 4
ROPE_SWAP = np.concatenate([np.arange(_Q, 2 * _Q), np.arange(0, _Q),
                            np.arange(3 * _Q, 4 * _Q), np.arange(2 * _Q, 3 * _Q)])

MLA_LOG2_SCALE = float((MLA_NOPE + MLA_ROPE) ** -0.5 * np.log2(np.e))
MLA_HEADS_PER_STEP = 4
MLA_KEY_CHUNK = 512
NA_HEADS_PER_STEP = 8
NA_Q_ROWS = 4
NA_KEY_ROWS = NA_Q_ROWS + NA_WIN_ROWS - 1
LRU_CHUNK = 256
LRU_COLS = 512
LRU_GROUP_UNROLL = 8


def _cparams(*sem):
    return pltpu.CompilerParams(dimension_semantics=sem, vmem_limit_bytes=VMEM_LIMIT_BYTES)


def _dot(a, b):
    return jnp.dot(a, b, preferred_element_type=jnp.float32)


def _dot_nt(a, b):
    return lax.dot_general(a, b, (((1,), (1,)), ((), ())), preferred_element_type=jnp.float32)


def _rms(x, g):
    return x * lax.rsqrt(jnp.mean(x * x, axis=-1, keepdims=True) + EPS) * g


def _row_tile(t, target):
    best = None
    for cand in range(2 * SUBLANES, min(t, target) + 1, 2 * SUBLANES):
        if t % cand == 0:
            best = cand
    assert best is not None
    return best


def _ada_kernel(c_ref, w_ref, b_ref, o_ref):
    c = c_ref[...]
    s = (c * jax.nn.sigmoid(c)).astype(jnp.bfloat16)
    o_ref[0] = _dot(s, w_ref[0].astype(jnp.bfloat16)) + b_ref[0]


def _ada_mod(cond, ada_w, ada_b):
    depth, d, n = ada_w.shape
    rows = cond.shape[0]
    tn = 1024
    return pl.pallas_call(
        _ada_kernel,
        grid=(depth, n // tn),
        in_specs=[pl.BlockSpec((rows, d), lambda l, j: (0, 0)),
                  pl.BlockSpec((1, d, tn), lambda l, j: (l, 0, j)),
                  pl.BlockSpec((1, 1, tn), lambda l, j: (l, 0, j))],
        out_specs=pl.BlockSpec((1, rows, tn), lambda l, j: (l, 0, j)),
        out_shape=jax.ShapeDtypeStruct((depth, rows, n), jnp.float32),
        compiler_params=_cparams("parallel", "parallel"),
        name="ada_mod",
    )(cond, ada_w, ada_b.reshape(depth, 1, n))


def _inproj_kernel(x_ref, g_ref, scl_ref, shl_ref, scc_ref, shc_ref, wa_ref, wb_ref, o_ref, h_ref,
                   *, n_ctx):
    i = pl.program_id(1)
    j = pl.program_id(2)
    tm = x_ref.shape[1]

    @pl.when(j == 0)
    def _():
        y = _rms(x_ref[0], g_ref[...])
        row = i * tm + lax.broadcasted_iota(jnp.int32, (tm, 1), 0)
        is_ctx = row < n_ctx
        sc = jnp.where(is_ctx, scc_ref[...], scl_ref[0])
        sh = jnp.where(is_ctx, shc_ref[...], shl_ref[0])
        h_ref[...] = (y * (1.0 + sc) + sh).astype(h_ref.dtype)
        o_ref[0] = _dot(h_ref[...], wa_ref[...]).astype(o_ref.dtype)

    @pl.when(j > 0)
    def _():
        o_ref[0] = _dot(h_ref[...], wb_ref[...]).astype(o_ref.dtype)


def _inproj(xa, g, sc_l, sh_l, sc_c, sh_c, w_mla, w_rest, layer, n_ctx):
    b, t, d = xa.shape
    tn = w_mla.shape[2]
    n_rest = w_rest.shape[2]
    assert n_rest % tn == 0
    tm = _row_tile(t, 1088)
    vec = pl.BlockSpec((1, d), lambda bi, i, j: (0, 0))
    per_b = pl.BlockSpec((1, 1, d), lambda bi, i, j: (bi, 0, 0))
    return pl.pallas_call(
        functools.partial(_inproj_kernel, n_ctx=n_ctx),
        grid=(b, t // tm, 1 + n_rest // tn),
        in_specs=[pl.BlockSpec((1, tm, d), lambda bi, i, j: (bi, i, 0)),
                  vec, per_b, per_b, vec, vec,
                  pl.BlockSpec((None, d, tn), lambda bi, i, j: (layer, 0, 0)),
                  pl.BlockSpec((None, d, tn), lambda bi, i, j: (layer, 0, jnp.maximum(j - 1, 0)))],
        out_specs=pl.BlockSpec((1, tm, tn), lambda bi, i, j: (bi, i, j)),
        out_shape=jax.ShapeDtypeStruct((b, t, tn + n_rest), jnp.bfloat16),
        scratch_shapes=[pltpu.VMEM((tm, d), jnp.bfloat16)],
        compiler_params=_cparams("parallel", "parallel", "arbitrary"),
        name="inproj",
    )(xa, g, sc_l, sh_l, sc_c, sh_c, w_mla, w_rest)


def _mla_prep_kernel(p_ref, gq_ref, gkv_ref, wq_ref, wkv_ref, cos_ref, sin_ref,
                     q_ref, k_ref, v_ref):
    p = p_ref[0].astype(jnp.float32)
    cos = cos_ref[...]
    sin = sin_ref[...]
    shift = MLA_HEAD_W - MLA_ROPE

    def rope(a):
        return a * cos + pltpu.roll(a, shift, 1) * sin

    hq = _rms(p[:, Q_OFF:Q_OFF + MLA_Q_LORA], gq_ref[...]).astype(jnp.bfloat16)
    hkv = _rms(p[:, KV_OFF:KV_OFF + MLA_KV_LORA], gkv_ref[...]).astype(jnp.bfloat16)
    k_rope = rope(p[:, KR_OFF:KR_OFF + KR_W])
    q = _dot(hq, wq_ref[...])
    kv = _dot(hkv, wkv_ref[...])
    k_w = MLA_HEADS * MLA_HEAD_W
    ones = jnp.ones((p.shape[0], MLA_HEAD_W - MLA_V), v_ref.dtype)
    for h in range(MLA_HEADS):
        sl = slice(h * MLA_HEAD_W, (h + 1) * MLA_HEAD_W)
        q_ref[0, :, sl] = (rope(q[:, sl]) * MLA_LOG2_SCALE).astype(q_ref.dtype)
        k_ref[0, :, sl] = (kv[:, sl] + k_rope).astype(k_ref.dtype)
        v_ref[0, :, h * MLA_HEAD_W:h * MLA_HEAD_W + MLA_V] = (
            kv[:, k_w + h * MLA_V:k_w + (h + 1) * MLA_V].astype(v_ref.dtype))
        v_ref[0, :, h * MLA_HEAD_W + MLA_V:(h + 1) * MLA_HEAD_W] = ones


def _mla_prep(p, gq, gkv, wq, wkv, cos_t, sin_t):
    b, t, _ = p.shape
    tm = _row_tile(t, 272)
    qk_w = MLA_HEADS * MLA_HEAD_W
    v_w = MLA_HEADS * MLA_HEAD_W
    full = lambda a: pl.BlockSpec(a.shape, lambda bi, i: (0,) * a.ndim)
    return pl.pallas_call(
        _mla_prep_kernel,
        grid=(b, t // tm),
        in_specs=[pl.BlockSpec((1, tm, MLA_IN_W), lambda bi, i: (bi, i, 0)),
                  full(gq), full(gkv), full(wq), full(wkv),
                  pl.BlockSpec((tm, MLA_HEAD_W), lambda bi, i: (i, 0)),
                  pl.BlockSpec((tm, MLA_HEAD_W), lambda bi, i: (i, 0))],
        out_specs=[pl.BlockSpec((1, tm, qk_w), lambda bi, i: (bi, i, 0)),
                   pl.BlockSpec((1, tm, qk_w), lambda bi, i: (bi, i, 0)),
                   pl.BlockSpec((1, tm, v_w), lambda bi, i: (bi, i, 0))],
        out_shape=[jax.ShapeDtypeStruct((b, t, qk_w), jnp.bfloat16),
                   jax.ShapeDtypeStruct((b, t, qk_w), jnp.bfloat16),
                   jax.ShapeDtypeStruct((b, t, v_w), jnp.bfloat16)],
        compiler_params=_cparams("parallel", "parallel"),
        name="mla_prep",
    )(p, gq, gkv, wq, wkv, cos_t, sin_t)


def _mla_attend(q_ref, k_ref, v_ref, o_ref, nk):
    hw = MLA_HEAD_W
    outs = []
    for hh in range(MLA_HEADS_PER_STEP):
        lanes = slice(hh * hw, (hh + 1) * hw)
        q = q_ref[0, :, lanes]
        m = acc = None
        for j0 in range(0, nk, MLA_KEY_CHUNK):
            j1 = min(j0 + MLA_KEY_CHUNK, nk)
            s = _dot_nt(q, k_ref[0, j0:j1, lanes])
            m_j = jnp.max(s, axis=-1, keepdims=True)
            m_new = m_j if m is None else jnp.maximum(m, m_j)
            e = jnp.exp2(s - m_new).astype(jnp.bfloat16)
            pv = _dot(e, v_ref[0, j0:j1, lanes])
            acc = pv if acc is None else acc * jnp.exp2(m - m_new) + pv
            m = m_new
        outs.append(acc[:, :MLA_V] / acc[:, MLA_V:])
    o_ref[0] = jnp.concatenate(outs, axis=1).astype(o_ref.dtype)


def _mla_attn_kernel(q_ref, k_ref, v_ref, o_ref, *, n_ctx):
    i = pl.program_id(2)
    tq = q_ref.shape[1]
    t = k_ref.shape[1]

    @pl.when(i < n_ctx // tq)
    def _():
        _mla_attend(q_ref, k_ref, v_ref, o_ref, n_ctx)

    @pl.when(i >= n_ctx // tq)
    def _():
        _mla_attend(q_ref, k_ref, v_ref, o_ref, t)


def _mla_attn(q, k, v, n_ctx):
    b, t, _ = q.shape
    tq = 2 * LANES
    hps = MLA_HEADS_PER_STEP
    assert n_ctx % tq == 0 and t % tq == 0 and MLA_HEADS % hps == 0 and MLA_HEAD_W == 2 * MLA_V
    return pl.pallas_call(
        functools.partial(_mla_attn_kernel, n_ctx=n_ctx),
        grid=(b, MLA_HEADS // hps, t // tq),
        in_specs=[pl.BlockSpec((1, tq, hps * MLA_HEAD_W), lambda bi, h, i: (bi, i, h)),
                  pl.BlockSpec((1, t, hps * MLA_HEAD_W), lambda bi, h, i: (bi, 0, h)),
                  pl.BlockSpec((1, t, hps * MLA_HEAD_W), lambda bi, h, i: (bi, 0, h))],
        out_specs=pl.BlockSpec((1, tq, hps * MLA_V), lambda bi, h, i: (bi, i, h)),
        out_shape=jax.ShapeDtypeStruct((b, t, MLA_HEADS * MLA_V), jnp.bfloat16),
        compiler_params=_cparams("parallel", "parallel", "arbitrary"),
        name="mla_attn",
    )(q, k, v)


def _na_plan(rows):
    kr = min(NA_WIN_ROWS, rows)
    assert rows % NA_Q_ROWS == 0 and rows >= NA_KEY_ROWS
    starts, patterns, var_of_block = [], [], []
    for r in range(rows // NA_Q_ROWS):
        start = int(np.clip(NA_Q_ROWS * r - kr // 2, 0, rows - NA_KEY_ROWS))
        valid = np.zeros((NA_Q_ROWS, NA_KEY_ROWS), bool)
        dr0 = np.zeros((NA_Q_ROWS,), np.int64)
        for a in range(NA_Q_ROWS):
            i = NA_Q_ROWS * r + a
            r0 = int(np.clip(i - kr // 2, 0, rows - kr))
            assert start <= r0 and r0 + kr <= start + NA_KEY_ROWS
            krow = start + np.arange(NA_KEY_ROWS)
            valid[a] = (r0 <= krow) & (krow < r0 + kr)
            dr0[a] = start - i + (NA_WIN_ROWS - 1)
        key = (valid.tobytes(), dr0.tobytes())
        for v, (pk, _, _) in enumerate(patterns):
            if pk == key:
                var_of_block.append(v)
                break
        else:
            var_of_block.append(len(patterns))
            patterns.append((key, valid, dr0))
        starts.append(start)
    valid_r = np.stack([p[1] for p in patterns])
    dr0 = np.stack([p[2] for p in patterns])
    return np.array(starts), np.array(var_of_block), valid_r, dr0


def _na_bias_tables(rel_bias, rows):
    _, _, valid_r, dr0 = _na_plan(rows)
    depth, heads, n_dr, n_dc = rel_bias.shape
    cols = np.arange(GRID_W)
    col_start = np.clip(cols - NA_WIN_COLS // 2, 0, GRID_W - NA_WIN_COLS)
    valid_c = (cols[None, :] >= col_start[:, None]) & (cols[None, :] < col_start[:, None] + NA_WIN_COLS)
    dc = cols[None, :] - cols[:, None] + (NA_WIN_COLS - 1)
    onehot = (valid_c[:, :, None] & (dc[:, :, None] == np.arange(n_dc))).astype(np.float32)
    e1 = jnp.einsum('lhrd,jcd->lhjrc', rel_bias, jnp.asarray(onehot), precision=lax.Precision.HIGHEST)
    e1 = jnp.where(jnp.asarray(valid_c)[None, None, :, None, :], e1, NEG_INF)
    flat = e1.reshape(depth, heads, GRID_W, n_dr * GRID_W)
    n_win = NA_KEY_ROWS * GRID_W
    pad = jnp.full((depth, heads, GRID_W, n_win), NEG_INF, flat.dtype)
    flat = jnp.concatenate([pad, flat, pad], axis=-1)
    assert dr0.min() > -NA_KEY_ROWS and dr0.max() < n_dr
    blocks = []
    for v in range(valid_r.shape[0]):
        for a in range(NA_Q_ROWS):
            off = n_win + int(dr0[v, a]) * GRID_W
            keep = np.repeat(valid_r[v, a], GRID_W)
            blocks.append(jnp.where(jnp.asarray(keep), flat[..., off:off + n_win], NEG_INF))
    bm = jnp.stack(blocks, axis=2)
    return bm.reshape(depth, heads, valid_r.shape[0], NA_Q_ROWS * GRID_W, n_win)


def _na_attn_kernel(var_ref, ks_ref, q_ref, k_ref, v_ref, bm_ref, o_ref, *, n_ctx, scale):
    del var_ref
    i = pl.program_id(2)
    tq = q_ref.shape[1]
    n_win = bm_ref.shape[-1]
    per_group = LANES // NA_HEAD_DIM
    low = lax.broadcasted_iota(jnp.int32, (1, LANES), 1) < NA_HEAD_DIM
    zero = jnp.zeros((), q_ref.dtype)

    def softmax_pv(parts):
        m = None
        for s, _ in parts:
            m_s = jnp.max(s, axis=-1, keepdims=True)
            m = m_s if m is None else jnp.maximum(m, m_s)
        l = pv = None
        for s, vals in parts:
            e = jnp.exp(s - m)
            l_s = jnp.sum(e, axis=-1, keepdims=True)
            pv_s = _dot(e.astype(jnp.bfloat16), vals)
            l = l_s if l is None else l + l_s
            pv = pv_s if pv is None else pv + pv_s
        return pv / l

    def run(window):
        if window:
            ks = pl.multiple_of(ks_ref[i], GRID_W)
        for g in range(NA_HEADS_PER_STEP // per_group):
            lanes = slice(g * LANES, (g + 1) * LANES)
            q = q_ref[0, :, lanes] * scale
            k_ctx = k_ref[0, :n_ctx, lanes]
            v_ctx = v_ref[0, :n_ctx, lanes]
            outs = []
            for hh in range(per_group):
                qh = jnp.where(low if hh == 0 else jnp.logical_not(low), q, zero)
                parts = [(_dot_nt(qh, k_ctx), v_ctx)]
                if window:
                    s_w = _dot_nt(qh, k_ref[0, pl.ds(ks, n_win), lanes]) + bm_ref[g * per_group + hh, 0]
                    parts.append((s_w, v_ref[0, pl.ds(ks, n_win), lanes]))
                outs.append(softmax_pv(parts))
            o_ref[0, :, lanes] = jnp.where(low, outs[0], outs[1]).astype(o_ref.dtype)

    @pl.when(i < n_ctx // tq)
    def _():
        run(False)

    @pl.when(i >= n_ctx // tq)
    def _():
        run(True)


def _na_attn(p, bm, layer, n_ctx):
    b, t, _ = p.shape
    tq = NA_Q_ROWS * GRID_W
    hps = NA_HEADS_PER_STEP
    gw = hps * NA_HEAD_DIM
    assert n_ctx % tq == 0 and (t - n_ctx) % tq == 0 and LANES == 2 * NA_HEAD_DIM
    assert float(np.log2(NA_HEAD_DIM ** -0.5)).is_integer()
    rows = (t - n_ctx) // GRID_W
    starts, var_of_block, _, _ = _na_plan(rows)
    n_cb = n_ctx // tq
    var_tab = jnp.asarray(np.concatenate([np.zeros(n_cb, np.int32), var_of_block]).astype(np.int32))
    ks_tab = jnp.asarray(np.concatenate([np.zeros(n_cb, np.int32), n_ctx + starts * GRID_W]).astype(np.int32))
    n_win = NA_KEY_ROWS * GRID_W
    qc, kc, vc = NAQ_OFF // gw, NAK_OFF // gw, NAV_OFF // gw
    grid_spec = pltpu.PrefetchScalarGridSpec(
        num_scalar_prefetch=2,
        grid=(b, NA_HEADS // hps, t // tq),
        in_specs=[pl.BlockSpec((1, tq, gw), lambda bi, h, i, var, ks: (bi, i, qc + h)),
                  pl.BlockSpec((1, t, gw), lambda bi, h, i, var, ks: (bi, 0, kc + h)),
                  pl.BlockSpec((1, t, gw), lambda bi, h, i, var, ks: (bi, 0, vc + h)),
                  pl.BlockSpec((None, hps, 1, tq, n_win),
                               lambda bi, h, i, var, ks: (layer, h, var[i], 0, 0))],
        out_specs=pl.BlockSpec((1, tq, gw), lambda bi, h, i, var, ks: (bi, i, h)),
    )
    return pl.pallas_call(
        functools.partial(_na_attn_kernel, n_ctx=n_ctx, scale=float(NA_HEAD_DIM ** -0.5)),
        grid_spec=grid_spec,
        out_shape=jax.ShapeDtypeStruct((b, t, NA_W), jnp.bfloat16),
        compiler_params=_cparams("parallel", "parallel", "arbitrary"),
        name="na_attn",
    )(var_tab, ks_tab, p, p, p, bm)


def _lru_kernel(x_ref, cw_ref, cb_ref, wg_ref, bg_ref, lam_ref, o_ref, xc_ref, a_ref, u_ref, *, n_ctx):
    t, wc = xc_ref.shape
    r = LRU_CHUNK
    n_chunk = t // r
    n_cc = n_ctx // r
    left = CONV_WIDTH // 2
    row = lax.broadcasted_iota(jnp.int32, (r, wc), 0)

    def conv_chunk(c, carry):
        t0 = pl.multiple_of(c * r, r)
        first = jnp.logical_or(c == 0, c == n_cc)
        last = jnp.logical_or(c == n_cc - 1, c == n_chunk - 1)
        halo = 2 * SUBLANES
        main = x_ref[0, pl.ds(t0, r), :].astype(jnp.float32)
        prev = x_ref[0, pl.ds(pl.multiple_of(jnp.maximum(t0 - halo, 0), halo), halo), :]
        nxt = x_ref[0, pl.ds(pl.multiple_of(jnp.minimum(t0 + r, t - halo), halo), halo), :]
        prev = jnp.where(first, 0.0, prev.astype(jnp.float32))
        nxt = jnp.where(last, 0.0, nxt.astype(jnp.float32))
        y = cb_ref[...] + jnp.zeros((r, wc), jnp.float32)
        for tap in range(CONV_WIDTH):
            off = tap - left
            if off == 0:
                xs = main
            elif off < 0:
                xs = pltpu.roll(main, -off, 0)
                for k in range(-off):
                    xs = jnp.where(row == k, prev[halo + off + k:halo + off + k + 1, :], xs)
            else:
                xs = pltpu.roll(main, r - off, 0)
                for k in range(off):
                    xs = jnp.where(row == r - off + k, nxt[k:k + 1, :], xs)
            y = y + xs * cw_ref[tap:tap + 1, :]
        xc_ref[pl.ds(t0, r), :] = y
        return carry

    lax.fori_loop(0, n_chunk, conv_chunk, 0)

    grow = lax.broadcasted_iota(jnp.int32, (SUBLANES, wc), 0)
    n_grp = r // SUBLANES

    def scan_direction(direction):
        reverse = direction == 1
        lam = lam_ref[direction]
        neg = -lam
        softplus = jnp.maximum(neg, 0.0) + jnp.log1p(jnp.exp(-jnp.abs(neg)))
        decay = -LRU_C * softplus

        def chunk_step(step, h):
            if reverse:
                c = jnp.where(step < n_cc, n_cc - 1 - step, n_chunk - 1 - (step - n_cc))
            else:
                c = step
            t0 = pl.multiple_of(c * r, r)
            xc = xc_ref[pl.ds(t0, r), :]
            g = _dot(xc.astype(jnp.bfloat16), wg_ref[direction, 0]) + bg_ref[direction, 0]
            rg = jax.nn.sigmoid(g[:, :wc])
            ig = jax.nn.sigmoid(g[:, wc:])
            log_a = decay * rg
            a = jnp.exp(log_a)
            a_ref[...] = a
            v = jnp.maximum(1.0 - a * a, 0.0)
            u_ref[...] = jnp.where(v > 0.0, v * lax.rsqrt(v), 0.0) * (ig * xc)

            def group_step(gi, h):
                g0 = (n_grp - 1 - gi) if reverse else gi
                off = pl.multiple_of(g0 * SUBLANES, SUBLANES)
                a = a_ref[pl.ds(off, SUBLANES), :]
                u = u_ref[pl.ds(off, SUBLANES), :]
                for s in (1, 2, 4):
                    if reverse:
                        keep = grow < SUBLANES - s
                        a_s = pltpu.roll(a, SUBLANES - s, 0)
                        u_s = pltpu.roll(u, SUBLANES - s, 0)
                    else:
                        keep = grow >= s
                        a_s = pltpu.roll(a, s, 0)
                        u_s = pltpu.roll(u, s, 0)
                    u = jnp.where(keep, a * u_s + u, u)
                    a = jnp.where(keep, a * a_s, a)
                hh = a * h + u
                dst = pl.ds(pl.multiple_of(t0 + off, SUBLANES), SUBLANES)
                if reverse:
                    o_ref[0, dst, :] = o_ref[0, dst, :] + hh
                    return hh[0:1, :]
                o_ref[0, dst, :] = hh
                return hh[SUBLANES - 1:SUBLANES, :]

            return lax.fori_loop(0, n_grp, group_step, h, unroll=LRU_GROUP_UNROLL)

        lax.fori_loop(0, n_chunk, chunk_step, jnp.zeros((1, wc), jnp.float32))

    scan_direction(0)
    scan_direction(1)


def _lru(p, conv_w, conv_b, wg, bg, lam, n_ctx):
    b, t, _ = p.shape
    wc = LRU_COLS
    assert t % LRU_CHUNK == 0 and n_ctx % LRU_CHUNK == 0 and n_ctx > 0 and t > n_ctx
    xcol = LRU_OFF // wc
    return pl.pallas_call(
        functools.partial(_lru_kernel, n_ctx=n_ctx),
        grid=(b, LRU_WIDTH // wc),
        in_specs=[pl.BlockSpec((1, t, wc), lambda bi, c: (bi, 0, xcol + c)),
                  pl.BlockSpec((CONV_WIDTH, wc), lambda bi, c: (0, c)),
                  pl.BlockSpec((1, wc), lambda bi, c: (0, c)),
                  pl.BlockSpec((2, 1, wc, 2 * wc), lambda bi, c: (0, c, 0, 0)),
                  pl.BlockSpec((2, 1, 1, 2 * wc), lambda bi, c: (0, c, 0, 0)),
                  pl.BlockSpec((2, 1, wc), lambda bi, c: (0, 0, c))],
        out_specs=pl.BlockSpec((1, t, wc), lambda bi, c: (bi, 0, c)),
        out_shape=jax.ShapeDtypeStruct((b, t, LRU_WIDTH), jnp.float32),
        scratch_shapes=[pltpu.VMEM((t, wc), jnp.float32),
                        pltpu.VMEM((LRU_CHUNK, wc), jnp.float32),
                        pltpu.VMEM((LRU_CHUNK, wc), jnp.float32)],
        compiler_params=_cparams("parallel", "parallel"),
        name="rglru",
    )(p, conv_w, conv_b, wg, bg, lam)


def _merge_kernel(oa_ref, ob_ref, oc_ref, gpa_ref, gpb_ref, gpc_ref, mga_ref, mgb_ref, mgc_ref,
                  wb_ref, z_ref):
    z = None
    for n, (o_ref, gp_ref, mg_ref) in enumerate(((oa_ref, gpa_ref, mga_ref),
                                                 (ob_ref, gpb_ref, mgb_ref),
                                                 (oc_ref, gpc_ref, mgc_ref))):
        gp = gp_ref[0].astype(jnp.float32)
        tkn = (o_ref[0].astype(jnp.float32) * (gp * jax.nn.sigmoid(gp))).astype(jnp.bfloat16)
        y = jax.nn.sigmoid(mg_ref[0].astype(jnp.float32)) * _dot(tkn, wb_ref[n])
        z = y if z is None else z + y
    z_ref[0] = z.astype(z_ref.dtype)


def _merge(oa, ob, oc, p, wb, layer):
    b, t, _ = oa.shape
    d = wb.shape[-1]
    tm = _row_tile(t, 544)
    gpc, mgc = GP_OFF // BRANCH_W, MG_OFF // d
    assert GP_OFF % BRANCH_W == 0 and MG_OFF % d == 0
    o_spec = pl.BlockSpec((1, tm, BRANCH_W), lambda bi, i: (bi, i, 0))
    gp_specs = [pl.BlockSpec((1, tm, BRANCH_W), functools.partial(lambda bi, i, n: (bi, i, gpc + n), n=n))
                for n in range(N_BRANCH)]
    mg_specs = [pl.BlockSpec((1, tm, d), functools.partial(lambda bi, i, n: (bi, i, mgc + n), n=n))
                for n in range(N_BRANCH)]
    return pl.pallas_call(
        _merge_kernel,
        grid=(b, t // tm),
        in_specs=[o_spec, o_spec, o_spec, *gp_specs, *mg_specs,
                  pl.BlockSpec((None,) + wb.shape[1:], lambda bi, i: (layer, 0, 0, 0),
                               pipeline_mode=pl.Buffered(1))],
        out_specs=pl.BlockSpec((1, tm, d), lambda bi, i: (bi, i, 0)),
        out_shape=jax.ShapeDtypeStruct((b, t, d), jnp.bfloat16),
        compiler_params=_cparams("parallel", "parallel"),
        name="merge",
    )(oa, ob, oc, p, p, p, p, p, p, wb)


def _outproj_kernel(z_ref, w_ref, x_ref, gl_ref, gc_ref, o_ref, *, n_ctx):
    i = pl.program_id(1)
    tm = x_ref.shape[1]
    row = i * tm + lax.broadcasted_iota(jnp.int32, (tm, 1), 0)
    gate = jnp.where(row < n_ctx, gc_ref[...], gl_ref[0])
    o_ref[0] = x_ref[0] + gate * _dot(z_ref[0], w_ref[...])


def _outproj(z, w, xa, gt_l, gt_c, layer, n_ctx):
    b, t, d = xa.shape
    tm = _row_tile(t, 544)
    return pl.pallas_call(
        functools.partial(_outproj_kernel, n_ctx=n_ctx),
        grid=(b, t // tm),
        in_specs=[pl.BlockSpec((1, tm, d), lambda bi, i: (bi, i, 0)),
                  pl.BlockSpec((None, d, d), lambda bi, i: (layer, 0, 0), pipeline_mode=pl.Buffered(1)),
                  pl.BlockSpec((1, tm, d), lambda bi, i: (bi, i, 0)),
                  pl.BlockSpec((1, 1, d), lambda bi, i: (bi, 0, 0)),
                  pl.BlockSpec((1, d), lambda bi, i: (0, 0))],
        out_specs=pl.BlockSpec((1, tm, d), lambda bi, i: (bi, i, 0)),
        out_shape=jax.ShapeDtypeStruct((b, t, d), jnp.float32),
        compiler_params=_cparams("parallel", "parallel"),
        name="outproj",
    )(z, w, xa, gt_l, gt_c)


def _outproj_final_kernel(z_ref, w_ref, x_ref, gl_ref, g_ref, o_ref):
    o_ref[0] = _rms(x_ref[0] + gl_ref[0] * _dot(z_ref[0], w_ref[...]), g_ref[...])


def _outproj_final(z, w, xa, gt_l, g, layer, n_ctx):
    b, t, d = xa.shape
    s = t - n_ctx
    tm = _row_tile(int(np.gcd(s, n_ctx)), 512)
    skip = n_ctx // tm
    return pl.pallas_call(
        _outproj_final_kernel,
        grid=(b, s // tm),
        in_specs=[pl.BlockSpec((1, tm, d), lambda bi, i: (bi, i + skip, 0)),
                  pl.BlockSpec((None, d, d), lambda bi, i: (layer, 0, 0), pipeline_mode=pl.Buffered(1)),
                  pl.BlockSpec((1, tm, d), lambda bi, i: (bi, i + skip, 0)),
                  pl.BlockSpec((1, 1, d), lambda bi, i: (bi, 0, 0)),
                  pl.BlockSpec((1, d), lambda bi, i: (0, 0))],
        out_specs=pl.BlockSpec((1, tm, d), lambda bi, i: (bi, i, 0)),
        out_shape=jax.ShapeDtypeStruct((b, s, d), jnp.float32),
        compiler_params=_cparams("parallel", "parallel"),
        name="outproj_final",
    )(z, w, xa, gt_l, g)


def _shift_cast_kernel(*refs, lane_off):
    o_ref = refs[-1]
    x = jnp.concatenate([r[...] for r in refs[:-1]], axis=1)
    o_ref[...] = x[:, lane_off:lane_off + o_ref.shape[1]].astype(o_ref.dtype)


def _shift_cast(w, start, tn, piece):
    depth, d, n = w.shape
    n_out = n - start
    lane_off = start % piece
    n_piece = (lane_off + tn + piece - 1) // piece
    assert n_out % tn == 0 and tn % piece == 0 and piece % LANES == 0
    tr = _row_tile(d, 512)
    in_specs = [pl.BlockSpec((None, tr, piece),
                             functools.partial(lambda l, r, j, k: (l, r, start // piece + j * (tn // piece) + k), k=k))
                for k in range(n_piece)]
    return pl.pallas_call(
        functools.partial(_shift_cast_kernel, lane_off=lane_off),
        grid=(depth, d // tr, n_out // tn),
        in_specs=in_specs,
        out_specs=pl.BlockSpec((None, tr, tn), lambda l, r, j: (l, r, j)),
        out_shape=jax.ShapeDtypeStruct((depth, d, n_out), jnp.bfloat16),
        compiler_params=_cparams("parallel", "parallel", "parallel"),
        name="w_in_cast",
    )(*([w] * n_piece))


def _prep_w_in(w_in):
    kr0 = KV_OFF + MLA_KV_LORA
    kr1 = kr0 + MLA_ROPE
    w_kr = w_in[..., kr0:kr1]
    pad = jnp.zeros(w_in.shape[:-1] + (KR_W - 2 * MLA_ROPE,), w_in.dtype)
    w_mla = jnp.concatenate([w_in[..., :kr0], pad, w_kr, w_kr[..., ROPE_SWAP]], axis=-1)
    return w_mla.astype(jnp.bfloat16), _shift_cast(w_in, kr1, MLA_IN_W, 2 * LANES)


def _prep_w_uq(w_uq):
    depth, r, _ = w_uq.shape
    w = w_uq.reshape(depth, r, MLA_HEADS, MLA_NOPE + MLA_ROPE)
    rope = w[..., MLA_NOPE:]
    w = jnp.concatenate([w[..., :MLA_NOPE], rope, rope[..., ROPE_SWAP]], axis=-1)
    return w.reshape(depth, r, MLA_HEADS * MLA_HEAD_W).astype(jnp.bfloat16)


def _prep_w_ukv(w_ukv):
    depth, r, _ = w_ukv.shape
    w = w_ukv.reshape(depth, r, MLA_HEADS, MLA_NOPE + MLA_V)
    k = jnp.concatenate([w[..., :MLA_NOPE], jnp.zeros_like(w[..., :MLA_HEAD_W - MLA_NOPE])], axis=-1)
    k = k.reshape(depth, r, MLA_HEADS * MLA_HEAD_W)
    v = w[..., MLA_NOPE:].reshape(depth, r, MLA_HEADS * MLA_V)
    return jnp.concatenate([k, v], axis=-1).astype(jnp.bfloat16)


def _rope_tables(n_ctx, s):
    tok = jnp.arange(s)
    row = (tok // GRID_W).astype(jnp.float32)
    col = (tok % GRID_W).astype(jnp.float32)
    half = MLA_ROPE // 2
    inv = 1.0 / (ROPE_BASE ** (jnp.arange(0, half, 2, dtype=jnp.float32) / half))
    ang_r = row[:, None] * inv
    ang_c = col[:, None] * inv
    cos = jnp.concatenate([jnp.cos(ang_r), jnp.cos(ang_r), jnp.cos(ang_c), jnp.cos(ang_c)], axis=-1)
    sin = jnp.concatenate([-jnp.sin(ang_r), jnp.sin(ang_r), -jnp.sin(ang_c), jnp.sin(ang_c)], axis=-1)
    cos = jnp.concatenate([jnp.ones((n_ctx, MLA_ROPE), jnp.float32), cos], axis=0)
    sin = jnp.concatenate([jnp.zeros((n_ctx, MLA_ROPE), jnp.float32), sin], axis=0)
    t = n_ctx + s
    tail = MLA_HEAD_W - MLA_NOPE - MLA_ROPE
    cos_t = jnp.concatenate([jnp.ones((t, MLA_NOPE), jnp.float32), cos, jnp.zeros((t, tail), jnp.float32)], axis=-1)
    sin_t = jnp.concatenate([jnp.zeros((t, MLA_NOPE), jnp.float32), sin, jnp.zeros((t, tail), jnp.float32)], axis=-1)
    return cos_t, sin_t


def _prep_lru_gates(w_gate, b_gate):
    depth = w_gate.shape[0]
    per = LRU_COLS // LRU_BLOCK
    n_col = LRU_BLOCKS // per
    eye = jnp.eye(per, dtype=w_gate.dtype)

    def block_diag(w):
        w = w.reshape(depth, 2, n_col, per, LRU_BLOCK, LRU_BLOCK)
        w = w[:, :, :, :, :, None, :] * eye[None, None, None, :, None, :, None]
        return w.reshape(depth, 2, n_col, LRU_COLS, LRU_COLS)

    wg = jnp.concatenate([block_diag(w_gate[..., :LRU_BLOCK]), block_diag(w_gate[..., LRU_BLOCK:])], axis=-1)
    bg = b_gate.reshape(depth, 2, LRU_BLOCKS, 2 * LRU_BLOCK)
    bg = jnp.concatenate([bg[..., :LRU_BLOCK].reshape(depth, 2, n_col, 1, LRU_COLS),
                          bg[..., LRU_BLOCK:].reshape(depth, 2, n_col, 1, LRU_COLS)], axis=-1)
    return wg.astype(jnp.bfloat16), bg


def kernel(x, c, ctx, c_ctx, ada_w, ada_b, norm_g, w_in, mla_q_norm_g, mla_kv_norm_g, mla_w_uq, mla_w_ukv, na_rel_bias, lru_conv_w, lru_conv_b, lru_w_gate, lru_b_gate, lru_lambda, w_branch, w_out, final_norm_g):
    b, s, d = x.shape
    n_ctx = ctx.shape[1]
    depth = ada_w.shape[0]
    rows = s // GRID_W

    w_mla_p, w_rest_p = _prep_w_in(w_in)
    wq_p = _prep_w_uq(mla_w_uq)
    wkv_p = _prep_w_ukv(mla_w_ukv)
    cos_t, sin_t = _rope_tables(n_ctx, s)
    bm = _na_bias_tables(na_rel_bias, rows)
    wg_p, bg_p = _prep_lru_gates(lru_w_gate, lru_b_gate)
    wb_p = w_branch.astype(jnp.bfloat16)
    wo_p = w_out.astype(jnp.bfloat16)

    n_cond = -(-(b + 1) // SUBLANES) * SUBLANES
    cond = jnp.concatenate([c, c_ctx[None, :], jnp.zeros((n_cond - b - 1, d), c.dtype)], axis=0)
    mod = _ada_mod(cond, ada_w, ada_b)

    xa = jnp.concatenate([ctx, x], axis=1)
    for l in range(depth):
        sh_l, sc_l, gt_l = (mod[l, :b, k * d:(k + 1) * d][:, None, :] for k in range(3))
        sh_c, sc_c, gt_c = (mod[l, b:b + 1, k * d:(k + 1) * d] for k in range(3))
        p = _inproj(xa, norm_g[l][None, :], sc_l, sh_l, sc_c, sh_c, w_mla_p, w_rest_p, l, n_ctx)
        q, k, v = _mla_prep(p, mla_q_norm_g[l][None, :], mla_kv_norm_g[l][None, :],
                            wq_p[l], wkv_p[l], cos_t, sin_t)
        o_a = _mla_attn(q, k, v, n_ctx)
        o_c = _lru(p, lru_conv_w[l], lru_conv_b[l][None, :], wg_p[l], bg_p[l],
                   lru_lambda[l][:, None, :], n_ctx)
        o_b = _na_attn(p, bm, l, n_ctx)
        z = _merge(o_a, o_b, o_c, p, wb_p, l)
        if l < depth - 1:
            xa = _outproj(z, wo_p, xa, gt_l, gt_c, l, n_ctx)
    return _outproj_final(z, wo_p, xa, gt_l, final_norm_g[None, :], depth - 1, n_ctx)
```

```python
import functools

import numpy as np
import jax
import jax.numpy as jnp
from jax import lax
from jax.experimental import pallas as pl
from jax.experimental.pallas import tpu as pltpu

GRID_W = 64
MLA_HEADS = 8
MLA_Q_LORA = 512
MLA_KV_LORA = 256
MLA_NOPE = 128
MLA_ROPE = 64
MLA_V = 128
NA_HEADS = 16
NA_HEAD_DIM = 64
NA_WIN_ROWS = 8
NA_WIN_COLS = 16
LRU_WIDTH = 1024
LRU_BLOCKS = 16
LRU_BLOCK = LRU_WIDTH // LRU_BLOCKS
LRU_C = 8.0
CONV_WIDTH = 4
N_BRANCH = 3
BRANCH_W = 1024
ROPE_BASE = 10000.0
EPS = 1e-6

LANES = 128
SUBLANES = 8
VMEM_LIMIT_BYTES = 56 * 1024 * 1024

MLA_HEAD_W = 2 * LANES
Q_OFF = 0
KV_OFF = Q_OFF + MLA_Q_LORA
KR_OFF = KV_OFF + MLA_KV_LORA
KR_W = MLA_HEAD_W
MLA_IN_W = KR_OFF + KR_W
NA_W = NA_HEADS * NA_HEAD_DIM
NAQ_OFF = MLA_IN_W
NAK_OFF = NAQ_OFF + NA_W
NAV_OFF = NAK_OFF + NA_W
LRU_OFF = NAV_OFF + NA_W
GP_OFF = LRU_OFF + LRU_WIDTH
MG_OFF = GP_OFF + N_BRANCH * BRANCH_W
NEG_INF = -1e30

_Q = MLA_ROPE // 4
ROPE_SWAP = np.concatenate([np.arange(_Q, 2 * _Q), np.arange(0, _Q),
                            np.arange(3 * _Q, 4 * _Q), np.arange(2 * _Q, 3 * _Q)])

MLA_LOG2_SCALE = float((MLA_NOPE + MLA_ROPE) ** -0.5 * np.log2(np.e))
MLA_HEADS_PER_STEP = 4
MLA_KEY_CHUNK = 512
NA_HEADS_PER_STEP = 8
NA_Q_ROWS = 4
NA_KEY_ROWS = NA_Q_ROWS + NA_WIN_ROWS - 1
LRU_CHUNK = 256
LRU_COLS = 512
LRU_GROUP_UNROLL = 8


def _cparams(*sem):
    return pltpu.CompilerParams(dimension_semantics=sem, vmem_limit_bytes=VMEM_LIMIT_BYTES)


def _dot(a, b):
    return jnp.dot(a, b, preferred_element_type=jnp.float32)


def _dot_nt(a, b):
    return lax.dot_general(a, b, (((1,), (1,)), ((), ())), preferred_element_type=jnp.float32)


def _rms(x, g):
    return x * lax.rsqrt(jnp.mean(x * x, axis=-1, keepdims=True) + EPS) * g


def _row_tile(t, target):
    best = None
    for cand in range(2 * SUBLANES, min(t, target) + 1, 2 * SUBLANES):
        if t % cand == 0:
            best = cand
    assert best is not None
    return best


def _ada_kernel(c_ref, w_ref, b_ref, o_ref):
    c = c_ref[...]
    s = (c * jax.nn.sigmoid(c)).astype(jnp.bfloat16)
    o_ref[0] = _dot(s, w_ref[0].astype(jnp.bfloat16)) + b_ref[0]


def _ada_mod(cond, ada_w, ada_b):
    depth, d, n = ada_w.shape
    rows = cond.shape[0]
    tn = 1024
    return pl.pallas_call(
        _ada_kernel,
        grid=(depth, n // tn),
        in_specs=[pl.BlockSpec((rows, d), lambda l, j: (0, 0)),
                  pl.BlockSpec((1, d, tn), lambda l, j: (l, 0, j)),
                  pl.BlockSpec((1, 1, tn), lambda l, j: (l, 0, j))],
        out_specs=pl.BlockSpec((1, rows, tn), lambda l, j: (l, 0, j)),
        out_shape=jax.ShapeDtypeStruct((depth, rows, n), jnp.float32),
        compiler_params=_cparams("parallel", "parallel"),
        name="ada_mod",
    )(cond, ada_w, ada_b.reshape(depth, 1, n))


def _inproj_kernel(x_ref, g_ref, scl_ref, shl_ref, scc_ref, shc_ref, wa_ref, wb_ref, o_ref, h_ref,
                   *, n_ctx):
    i = pl.program_id(1)
    j = pl.program_id(2)
    tm = x_ref.shape[1]

    @pl.when(j == 0)
    def _():
        y = _rms(x_ref[0], g_ref[...])
        row = i * tm + lax.broadcasted_iota(jnp.int32, (tm, 1), 0)
        is_ctx = row < n_ctx
        sc = jnp.where(is_ctx, scc_ref[...], scl_ref[0])
        sh = jnp.where(is_ctx, shc_ref[...], shl_ref[0])
        h_ref[...] = (y * (1.0 + sc) + sh).astype(h_ref.dtype)
        o_ref[0] = _dot_nt(h_ref[...], wa_ref[...]).astype(o_ref.dtype)

    @pl.when(j > 0)
    def _():
        o_ref[0] = _dot_nt(h_ref[...], wb_ref[...]).astype(o_ref.dtype)


def _inproj(xa, g, sc_l, sh_l, sc_c, sh_c, w_mla, w_rest, layer, n_ctx):
    b, t, d = xa.shape
    tn = w_mla.shape[1]
    n_rest = w_rest.shape[1]
    assert n_rest % tn == 0
    tm = _row_tile(t, 1088)
    vec = pl.BlockSpec((1, d), lambda bi, i, j: (0, 0))
    per_b = pl.BlockSpec((1, 1, d), lambda bi, i, j: (bi, 0, 0))
    return pl.pallas_call(
        functools.partial(_inproj_kernel, n_ctx=n_ctx),
        grid=(b, t // tm, 1 + n_rest // tn),
        in_specs=[pl.BlockSpec((1, tm, d), lambda bi, i, j: (bi, i, 0)),
                  vec, per_b, per_b, vec, vec,
                  pl.BlockSpec((None, tn, d), lambda bi, i, j: (layer, 0, 0)),
                  pl.BlockSpec((None, tn, d), lambda bi, i, j: (layer, jnp.maximum(j - 1, 0), 0))],
        out_specs=pl.BlockSpec((1, tm, tn), lambda bi, i, j: (bi, i, j)),
        out_shape=jax.ShapeDtypeStruct((b, t, tn + n_rest), jnp.bfloat16),
        scratch_shapes=[pltpu.VMEM((tm, d), jnp.bfloat16)],
        compiler_params=_cparams("parallel", "parallel", "arbitrary"),
        name="inproj",
    )(xa, g, sc_l, sh_l, sc_c, sh_c, w_mla, w_rest)


def _mla_prep_kernel(p_ref, gq_ref, gkv_ref, wq_ref, wkv_ref, cos_ref, sin_ref,
                     q_ref, k_ref, v_ref):
    p = p_ref[0].astype(jnp.float32)
    cos = cos_ref[...]
    sin = sin_ref[...]
    shift = MLA_HEAD_W - MLA_ROPE

    def rope(a):
        return a * cos + pltpu.roll(a, shift, 1) * sin

    hq = _rms(p[:, Q_OFF:Q_OFF + MLA_Q_LORA], gq_ref[...]).astype(jnp.bfloat16)
    hkv = _rms(p[:, KV_OFF:KV_OFF + MLA_KV_LORA], gkv_ref[...]).astype(jnp.bfloat16)
    k_rope = rope(p[:, KR_OFF:KR_OFF + KR_W])
    q = _dot(hq, wq_ref[...])
    kv = _dot(hkv, wkv_ref[...])
    k_w = MLA_HEADS * MLA_HEAD_W
    ones = jnp.ones((p.shape[0], MLA_HEAD_W - MLA_V), v_ref.dtype)
    for h in range(MLA_HEADS):
        sl = slice(h * MLA_HEAD_W, (h + 1) * MLA_HEAD_W)
        q_ref[0, :, sl] = (rope(q[:, sl]) * MLA_LOG2_SCALE).astype(q_ref.dtype)
        k_ref[0, :, sl] = (kv[:, sl] + k_rope).astype(k_ref.dtype)
        v_ref[0, :, h * MLA_HEAD_W:h * MLA_HEAD_W + MLA_V] = (
            kv[:, k_w + h * MLA_V:k_w + (h + 1) * MLA_V].astype(v_ref.dtype))
        v_ref[0, :, h * MLA_HEAD_W + MLA_V:(h + 1) * MLA_HEAD_W] = ones


def _mla_prep(p, gq, gkv, wq, wkv, cos_t, sin_t):
    b, t, _ = p.shape
    tm = _row_tile(t, 272)
    qk_w = MLA_HEADS * MLA_HEAD_W
    v_w = MLA_HEADS * MLA_HEAD_W
    full = lambda a: pl.BlockSpec(a.shape, lambda bi, i: (0,) * a.ndim)
    return pl.pallas_call(
        _mla_prep_kernel,
        grid=(b, t // tm),
        in_specs=[pl.BlockSpec((1, tm, MLA_IN_W), lambda bi, i: (bi, i, 0)),
                  full(gq), full(gkv), full(wq), full(wkv),
                  pl.BlockSpec((tm, MLA_HEAD_W), lambda bi, i: (i, 0)),
                  pl.BlockSpec((tm, MLA_HEAD_W), lambda bi, i: (i, 0))],
        out_specs=[pl.BlockSpec((1, tm, qk_w), lambda bi, i: (bi, i, 0)),
                   pl.BlockSpec((1, tm, qk_w), lambda bi, i: (bi, i, 0)),
                   pl.BlockSpec((1, tm, v_w), lambda bi, i: (bi, i, 0))],
        out_shape=[jax.ShapeDtypeStruct((b, t, qk_w), jnp.bfloat16),
                   jax.ShapeDtypeStruct((b, t, qk_w), jnp.bfloat16),
                   jax.ShapeDtypeStruct((b, t, v_w), jnp.bfloat16)],
        compiler_params=_cparams("parallel", "parallel"),
        name="mla_prep",
    )(p, gq, gkv, wq, wkv, cos_t, sin_t)


def _mla_attend(q_ref, k_ref, v_ref, o_ref, nk):
    hw = MLA_HEAD_W
    outs = []
    for hh in range(MLA_HEADS_PER_STEP):
        lanes = slice(hh * hw, (hh + 1) * hw)
        q = q_ref[0, :, lanes]
        m = acc = None
        for j0 in range(0, nk, MLA_KEY_CHUNK):
            j1 = min(j0 + MLA_KEY_CHUNK, nk)
            s = _dot_nt(q, k_ref[0, j0:j1, lanes])
            m_j = jnp.max(s, axis=-1, keepdims=True)
            m_new = m_j if m is None else jnp.maximum(m, m_j)
            e = jnp.exp2(s - m_new).astype(jnp.bfloat16)
            pv = _dot(e, v_ref[0, j0:j1, lanes])
            acc = pv if acc is None else acc * jnp.exp2(m - m_new) + pv
            m = m_new
        outs.append(acc[:, :MLA_V] / acc[:, MLA_V:])
    o_ref[0] = jnp.concatenate(outs, axis=1).astype(o_ref.dtype)


def _mla_attn_kernel(q_ref, k_ref, v_ref, o_ref, *, n_ctx):
    i = pl.program_id(2)
    tq = q_ref.shape[1]
    t = k_ref.shape[1]

    @pl.when(i < n_ctx // tq)
    def _():
        _mla_attend(q_ref, k_ref, v_ref, o_ref, n_ctx)

    @pl.when(i >= n_ctx // tq)
    def _():
        _mla_attend(q_ref, k_ref, v_ref, o_ref, t)


def _mla_attn(q, k, v, n_ctx):
    b, t, _ = q.shape
    tq = 2 * LANES
    hps = MLA_HEADS_PER_STEP
    assert n_ctx % tq == 0 and t % tq == 0 and MLA_HEADS % hps == 0 and MLA_HEAD_W == 2 * MLA_V
    return pl.pallas_call(
        functools.partial(_mla_attn_kernel, n_ctx=n_ctx),
        grid=(b, MLA_HEADS // hps, t // tq),
        in_specs=[pl.BlockSpec((1, tq, hps * MLA_HEAD_W), lambda bi, h, i: (bi, i, h)),
                  pl.BlockSpec((1, t, hps * MLA_HEAD_W), lambda bi, h, i: (bi, 0, h)),
                  pl.BlockSpec((1, t, hps * MLA_HEAD_W), lambda bi, h, i: (bi, 0, h))],
        out_specs=pl.BlockSpec((1, tq, hps * MLA_V), lambda bi, h, i: (bi, i, h)),
        out_shape=jax.ShapeDtypeStruct((b, t, MLA_HEADS * MLA_V), jnp.bfloat16),
        compiler_params=_cparams("parallel", "parallel", "arbitrary"),
        name="mla_attn",
    )(q, k, v)


def _na_plan(rows):
    kr = min(NA_WIN_ROWS, rows)
    assert rows % NA_Q_ROWS == 0 and rows >= NA_KEY_ROWS
    starts, patterns, var_of_block = [], [], []
    for r in range(rows // NA_Q_ROWS):
        start = int(np.clip(NA_Q_ROWS * r - kr // 2, 0, rows - NA_KEY_ROWS))
        valid = np.zeros((NA_Q_ROWS, NA_KEY_ROWS), bool)
        dr0 = np.zeros((NA_Q_ROWS,), np.int64)
        for a in range(NA_Q_ROWS):
            i = NA_Q_ROWS * r + a
            r0 = int(np.clip(i - kr // 2, 0, rows - kr))
            assert start <= r0 and r0 + kr <= start + NA_KEY_ROWS
            krow = start + np.arange(NA_KEY_ROWS)
            valid[a] = (r0 <= krow) & (krow < r0 + kr)
            dr0[a] = start - i + (NA_WIN_ROWS - 1)
        key = (valid.tobytes(), dr0.tobytes())
        for v, (pk, _, _) in enumerate(patterns):
            if pk == key:
                var_of_block.append(v)
                break
        else:
            var_of_block.append(len(patterns))
            patterns.append((key, valid, dr0))
        starts.append(start)
    valid_r = np.stack([p[1] for p in patterns])
    dr0 = np.stack([p[2] for p in patterns])
    return np.array(starts), np.array(var_of_block), valid_r, dr0


def _na_bias_tables(rel_bias, rows):
    _, _, valid_r, dr0 = _na_plan(rows)
    depth, heads, n_dr, n_dc = rel_bias.shape
    cols = np.arange(GRID_W)
    col_start = np.clip(cols - NA_WIN_COLS // 2, 0, GRID_W - NA_WIN_COLS)
    valid_c = (cols[None, :] >= col_start[:, None]) & (cols[None, :] < col_start[:, None] + NA_WIN_COLS)
    dc = cols[None, :] - cols[:, None] + (NA_WIN_COLS - 1)
    onehot = (valid_c[:, :, None] & (dc[:, :, None] == np.arange(n_dc))).astype(np.float32)
    e1 = jnp.einsum('lhrd,jcd->lhjrc', rel_bias, jnp.asarray(onehot), precision=lax.Precision.HIGHEST)
    e1 = jnp.where(jnp.asarray(valid_c)[None, None, :, None, :], e1, NEG_INF)
    flat = e1.reshape(depth, heads, GRID_W, n_dr * GRID_W)
    n_win = NA_KEY_ROWS * GRID_W
    pad = jnp.full((depth, heads, GRID_W, n_win), NEG_INF, flat.dtype)
    flat = jnp.concatenate([pad, flat, pad], axis=-1)
    assert dr0.min() > -NA_KEY_ROWS and dr0.max() < n_dr
    blocks = []
    for v in range(valid_r.shape[0]):
        for a in range(NA_Q_ROWS):
            off = n_win + int(dr0[v, a]) * GRID_W
            keep = np.repeat(valid_r[v, a], GRID_W)
            blocks.append(jnp.where(jnp.asarray(keep), flat[..., off:off + n_win], NEG_INF))
    bm = jnp.stack(blocks, axis=2)
    return bm.reshape(depth, heads, valid_r.shape[0], NA_Q_ROWS * GRID_W, n_win)


def _na_attn_kernel(var_ref, ks_ref, q_ref, k_ref, v_ref, bm_ref, o_ref, *, n_ctx, scale):
    del var_ref
    i = pl.program_id(2)
    tq = q_ref.shape[1]
    n_win = bm_ref.shape[-1]
    per_group = LANES // NA_HEAD_DIM
    low = lax.broadcasted_iota(jnp.int32, (1, LANES), 1) < NA_HEAD_DIM
    zero = jnp.zeros((), q_ref.dtype)

    def softmax_pv(parts):
        m = None
        for s, _ in parts:
            m_s = jnp.max(s, axis=-1, keepdims=True)
            m = m_s if m is None else jnp.maximum(m, m_s)
        l = pv = None
        for s, vals in parts:
            e = jnp.exp(s - m)
            l_s = jnp.sum(e, axis=-1, keepdims=True)
            pv_s = _dot(e.astype(jnp.bfloat16), vals)
            l = l_s if l is None else l + l_s
            pv = pv_s if pv is None else pv + pv_s
        return pv / l

    def run(window):
        if window:
            ks = pl.multiple_of(ks_ref[i], GRID_W)
        for g in range(NA_HEADS_PER_STEP // per_group):
            lanes = slice(g * LANES, (g + 1) * LANES)
            q = q_ref[0, :, lanes] * scale
            k_ctx = k_ref[0, :n_ctx, lanes]
            v_ctx = v_ref[0, :n_ctx, lanes]
            outs = []
            for hh in range(per_group):
                qh = jnp.where(low if hh == 0 else jnp.logical_not(low), q, zero)
                parts = [(_dot_nt(qh, k_ctx), v_ctx)]
                if window:
                    s_w = _dot_nt(qh, k_ref[0, pl.ds(ks, n_win), lanes]) + bm_ref[g * per_group + hh, 0]
                    parts.append((s_w, v_ref[0, pl.ds(ks, n_win), lanes]))
                outs.append(softmax_pv(parts))
            o_ref[0, :, lanes] = jnp.where(low, outs[0], outs[1]).astype(o_ref.dtype)

    @pl.when(i < n_ctx // tq)
    def _():
        run(False)

    @pl.when(i >= n_ctx // tq)
    def _():
        run(True)


def _na_attn(p, bm, layer, n_ctx):
    b, t, _ = p.shape
    tq = NA_Q_ROWS * GRID_W
    hps = NA_HEADS_PER_STEP
    gw = hps * NA_HEAD_DIM
    assert n_ctx % tq == 0 and (t - n_ctx) % tq == 0 and LANES == 2 * NA_HEAD_DIM
    assert float(np.log2(NA_HEAD_DIM ** -0.5)).is_integer()
    rows = (t - n_ctx) // GRID_W
    starts, var_of_block, _, _ = _na_plan(rows)
    n_cb = n_ctx // tq
    var_tab = jnp.asarray(np.concatenate([np.zeros(n_cb, np.int32), var_of_block]).astype(np.int32))
    ks_tab = jnp.asarray(np.concatenate([np.zeros(n_cb, np.int32), n_ctx + starts * GRID_W]).astype(np.int32))
    n_win = NA_KEY_ROWS * GRID_W
    qc, kc, vc = NAQ_OFF // gw, NAK_OFF // gw, NAV_OFF // gw
    grid_spec = pltpu.PrefetchScalarGridSpec(
        num_scalar_prefetch=2,
        grid=(b, NA_HEADS // hps, t // tq),
        in_specs=[pl.BlockSpec((1, tq, gw), lambda bi, h, i, var, ks: (bi, i, qc + h)),
                  pl.BlockSpec((1, t, gw), lambda bi, h, i, var, ks: (bi, 0, kc + h)),
                  pl.BlockSpec((1, t, gw), lambda bi, h, i, var, ks: (bi, 0, vc + h)),
                  pl.BlockSpec((None, hps, 1, tq, n_win),
                               lambda bi, h, i, var, ks: (layer, h, var[i], 0, 0))],
        out_specs=pl.BlockSpec((1, tq, gw), lambda bi, h, i, var, ks: (bi, i, h)),
    )
    return pl.pallas_call(
        functools.partial(_na_attn_kernel, n_ctx=n_ctx, scale=float(NA_HEAD_DIM ** -0.5)),
        grid_spec=grid_spec,
        out_shape=jax.ShapeDtypeStruct((b, t, NA_W), jnp.bfloat16),
        compiler_params=_cparams("parallel", "parallel", "arbitrary"),
        name="na_attn",
    )(var_tab, ks_tab, p, p, p, bm)


def _lru_kernel(x_ref, cw_ref, cb_ref, wg_ref, bg_ref, lam_ref, o_ref, xc_ref, a_ref, u_ref, *, n_ctx):
    t, wc = xc_ref.shape
    r = LRU_CHUNK
    n_chunk = t // r
    n_cc = n_ctx // r
    left = CONV_WIDTH // 2
    row = lax.broadcasted_iota(jnp.int32, (r, wc), 0)

    def conv_chunk(c, carry):
        t0 = pl.multiple_of(c * r, r)
        first = jnp.logical_or(c == 0, c == n_cc)
        last = jnp.logical_or(c == n_cc - 1, c == n_chunk - 1)
        halo = 2 * SUBLANES
        main = x_ref[0, pl.ds(t0, r), :].astype(jnp.float32)
        prev = x_ref[0, pl.ds(pl.multiple_of(jnp.maximum(t0 - halo, 0), halo), halo), :]
        nxt = x_ref[0, pl.ds(pl.multiple_of(jnp.minimum(t0 + r, t - halo), halo), halo), :]
        prev = jnp.where(first, 0.0, prev.astype(jnp.float32))
        nxt = jnp.where(last, 0.0, nxt.astype(jnp.float32))
        y = cb_ref[...] + jnp.zeros((r, wc), jnp.float32)
        for tap in range(CONV_WIDTH):
            off = tap - left
            if off == 0:
                xs = main
            elif off < 0:
                xs = pltpu.roll(main, -off, 0)
                for k in range(-off):
                    xs = jnp.where(row == k, prev[halo + off + k:halo + off + k + 1, :], xs)
            else:
                xs = pltpu.roll(main, r - off, 0)
                for k in range(off):
                    xs = jnp.where(row == r - off + k, nxt[k:k + 1, :], xs)
            y = y + xs * cw_ref[tap:tap + 1, :]
        xc_ref[pl.ds(t0, r), :] = y
        return carry

    lax.fori_loop(0, n_chunk, conv_chunk, 0)

    grow = lax.broadcasted_iota(jnp.int32, (SUBLANES, wc), 0)
    n_grp = r // SUBLANES

    def scan_direction(direction):
        reverse = direction == 1
        lam = lam_ref[direction]
        neg = -lam
        softplus = jnp.maximum(neg, 0.0) + jnp.log1p(jnp.exp(-jnp.abs(neg)))
        decay = -LRU_C * softplus

        def chunk_step(step, h):
            if reverse:
                c = jnp.where(step < n_cc, n_cc - 1 - step, n_chunk - 1 - (step - n_cc))
            else:
                c = step
            t0 = pl.multiple_of(c * r, r)
            xc = xc_ref[pl.ds(t0, r), :]
            g = _dot(xc.astype(jnp.bfloat16), wg_ref[direction, 0]) + bg_ref[direction, 0]
            rg = jax.nn.sigmoid(g[:, :wc])
            ig = jax.nn.sigmoid(g[:, wc:])
            log_a = decay * rg
            a = jnp.exp(log_a)
            a_ref[...] = a
            v = jnp.maximum(1.0 - a * a, 0.0)
            u_ref[...] = jnp.where(v > 0.0, v * lax.rsqrt(v), 0.0) * (ig * xc)

            def group_step(gi, h):
                g0 = (n_grp - 1 - gi) if reverse else gi
                off = pl.multiple_of(g0 * SUBLANES, SUBLANES)
                a = a_ref[pl.ds(off, SUBLANES), :]
                u = u_ref[pl.ds(off, SUBLANES), :]
                for s in (1, 2, 4):
                    if reverse:
                        keep = grow < SUBLANES - s
                        a_s = pltpu.roll(a, SUBLANES - s, 0)
                        u_s = pltpu.roll(u, SUBLANES - s, 0)
                    else:
                        keep = grow >= s
                        a_s = pltpu.roll(a, s, 0)
                        u_s = pltpu.roll(u, s, 0)
                    u = jnp.where(keep, a * u_s + u, u)
                    a = jnp.where(keep, a * a_s, a)
                hh = a * h + u
                dst = pl.ds(pl.multiple_of(t0 + off, SUBLANES), SUBLANES)
                if reverse:
                    o_ref[0, dst, :] = o_ref[0, dst, :] + hh
                    return hh[0:1, :]
                o_ref[0, dst, :] = hh
                return hh[SUBLANES - 1:SUBLANES, :]

            return lax.fori_loop(0, n_grp, group_step, h, unroll=LRU_GROUP_UNROLL)

        lax.fori_loop(0, n_chunk, chunk_step, jnp.zeros((1, wc), jnp.float32))

    scan_direction(0)
    scan_direction(1)


def _lru(p, conv_w, conv_b, wg, bg, lam, n_ctx):
    b, t, _ = p.shape
    wc = LRU_COLS
    assert t % LRU_CHUNK == 0 and n_ctx % LRU_CHUNK == 0 and n_ctx > 0 and t > n_ctx
    xcol = LRU_OFF // wc
    return pl.pallas_call(
        functools.partial(_lru_kernel, n_ctx=n_ctx),
        grid=(b, LRU_WIDTH // wc),
        in_specs=[pl.BlockSpec((1, t, wc), lambda bi, c: (bi, 0, xcol + c)),
                  pl.BlockSpec((CONV_WIDTH, wc), lambda bi, c: (0, c)),
                  pl.BlockSpec((1, wc), lambda bi, c: (0, c)),
                  pl.BlockSpec((2, 1, wc, 2 * wc), lambda bi, c: (0, c, 0, 0)),
                  pl.BlockSpec((2, 1, 1, 2 * wc), lambda bi, c: (0, c, 0, 0)),
                  pl.BlockSpec((2, 1, wc), lambda bi, c: (0, 0, c))],
        out_specs=pl.BlockSpec((1, t, wc), lambda bi, c: (bi, 0, c)),
        out_shape=jax.ShapeDtypeStruct((b, t, LRU_WIDTH), jnp.float32),
        scratch_shapes=[pltpu.VMEM((t, wc), jnp.float32),
                        pltpu.VMEM((LRU_CHUNK, wc), jnp.float32),
                        pltpu.VMEM((LRU_CHUNK, wc), jnp.float32)],
        compiler_params=_cparams("parallel", "parallel"),
        name="rglru",
    )(p, conv_w, conv_b, wg, bg, lam)


def _merge_kernel(oa_ref, ob_ref, oc_ref, gpa_ref, gpb_ref, gpc_ref, mga_ref, mgb_ref, mgc_ref,
                  wb_ref, z_ref):
    z = None
    for n, (o_ref, gp_ref, mg_ref) in enumerate(((oa_ref, gpa_ref, mga_ref),
                                                 (ob_ref, gpb_ref, mgb_ref),
                                                 (oc_ref, gpc_ref, mgc_ref))):
        gp = gp_ref[0].astype(jnp.float32)
        tkn = (o_ref[0].astype(jnp.float32) * (gp * jax.nn.sigmoid(gp))).astype(jnp.bfloat16)
        y = jax.nn.sigmoid(mg_ref[0].astype(jnp.float32)) * _dot(tkn, wb_ref[n])
        z = y if z is None else z + y
    z_ref[0] = z.astype(z_ref.dtype)


def _merge(oa, ob, oc, p, wb, layer):
    b, t, _ = oa.shape
    d = wb.shape[-1]
    tm = _row_tile(t, 544)
    gpc, mgc = GP_OFF // BRANCH_W, MG_OFF // d
    assert GP_OFF % BRANCH_W == 0 and MG_OFF % d == 0
    o_spec = pl.BlockSpec((1, tm, BRANCH_W), lambda bi, i: (bi, i, 0))
    gp_specs = [pl.BlockSpec((1, tm, BRANCH_W), functools.partial(lambda bi, i, n: (bi, i, gpc + n), n=n))
                for n in range(N_BRANCH)]
    mg_specs = [pl.BlockSpec((1, tm, d), functools.partial(lambda bi, i, n: (bi, i, mgc + n), n=n))
                for n in range(N_BRANCH)]
    return pl.pallas_call(
        _merge_kernel,
        grid=(b, t // tm),
        in_specs=[o_spec, o_spec, o_spec, *gp_specs, *mg_specs,
                  pl.BlockSpec((None,) + wb.shape[1:], lambda bi, i: (layer, 0, 0, 0),
                               pipeline_mode=pl.Buffered(1))],
        out_specs=pl.BlockSpec((1, tm, d), lambda bi, i: (bi, i, 0)),
        out_shape=jax.ShapeDtypeStruct((b, t, d), jnp.bfloat16),
        compiler_params=_cparams("parallel", "parallel"),
        name="merge",
    )(oa, ob, oc, p, p, p, p, p, p, wb)


def _outproj_kernel(z_ref, w_ref, x_ref, gl_ref, gc_ref, o_ref, *, n_ctx):
    i = pl.program_id(1)
    tm = x_ref.shape[1]
    row = i * tm + lax.broadcasted_iota(jnp.int32, (tm, 1), 0)
    gate = jnp.where(row < n_ctx, gc_ref[...], gl_ref[0])
    o_ref[0] = x_ref[0] + gate * _dot(z_ref[0], w_ref[...])


def _outproj(z, w, xa, gt_l, gt_c, layer, n_ctx):
    b, t, d = xa.shape
    tm = _row_tile(t, 544)
    return pl.pallas_call(
        functools.partial(_outproj_kernel, n_ctx=n_ctx),
        grid=(b, t // tm),
        in_specs=[pl.BlockSpec((1, tm, d), lambda bi, i: (bi, i, 0)),
                  pl.BlockSpec((None, d, d), lambda bi, i: (layer, 0, 0), pipeline_mode=pl.Buffered(1)),
                  pl.BlockSpec((1, tm, d), lambda bi, i: (bi, i, 0)),
                  pl.BlockSpec((1, 1, d), lambda bi, i: (bi, 0, 0)),
                  pl.BlockSpec((1, d), lambda bi, i: (0, 0))],
        out_specs=pl.BlockSpec((1, tm, d), lambda bi, i: (bi, i, 0)),
        out_shape=jax.ShapeDtypeStruct((b, t, d), jnp.float32),
        compiler_params=_cparams("parallel", "parallel"),
        name="outproj",
    )(z, w, xa, gt_l, gt_c)


def _outproj_final_kernel(z_ref, w_ref, x_ref, gl_ref, g_ref, o_ref):
    o_ref[0] = _rms(x_ref[0] + gl_ref[0] * _dot(z_ref[0], w_ref[...]), g_ref[...])


def _outproj_final(z, w, xa, gt_l, g, layer, n_ctx):
    b, t, d = xa.shape
    s = t - n_ctx
    tm = _row_tile(int(np.gcd(s, n_ctx)), 512)
    skip = n_ctx // tm
    return pl.pallas_call(
        _outproj_final_kernel,
        grid=(b, s // tm),
        in_specs=[pl.BlockSpec((1, tm, d), lambda bi, i: (bi, i + skip, 0)),
                  pl.BlockSpec((None, d, d), lambda bi, i: (layer, 0, 0), pipeline_mode=pl.Buffered(1)),
                  pl.BlockSpec((1, tm, d), lambda bi, i: (bi, i + skip, 0)),
                  pl.BlockSpec((1, 1, d), lambda bi, i: (bi, 0, 0)),
                  pl.BlockSpec((1, d), lambda bi, i: (0, 0))],
        out_specs=pl.BlockSpec((1, tm, d), lambda bi, i: (bi, i, 0)),
        out_shape=jax.ShapeDtypeStruct((b, s, d), jnp.float32),
        compiler_params=_cparams("parallel", "parallel"),
        name="outproj_final",
    )(z, w, xa, gt_l, g)


def _cast_rows_kernel(x_ref, o_ref):
    o_ref[...] = x_ref[...].astype(o_ref.dtype)


def _cast_rows(w_t, start):
    depth, n, d = w_t.shape
    tr = start
    assert n % tr == 0 and tr % (2 * SUBLANES) == 0
    return pl.pallas_call(
        _cast_rows_kernel,
        grid=(depth, n // tr - 1),
        in_specs=[pl.BlockSpec((None, tr, d), lambda l, j: (l, j + 1, 0))],
        out_specs=pl.BlockSpec((None, tr, d), lambda l, j: (l, j, 0)),
        out_shape=jax.ShapeDtypeStruct((depth, n - start, d), jnp.bfloat16),
        compiler_params=_cparams("parallel", "parallel"),
        name="w_in_cast",
    )(w_t)


def _prep_w_in(w_in):
    w_t = jnp.swapaxes(w_in, 1, 2)
    kr0 = KV_OFF + MLA_KV_LORA
    kr1 = kr0 + MLA_ROPE
    w_kr = w_t[:, kr0:kr1]
    pad = jnp.zeros((w_t.shape[0], KR_W - 2 * MLA_ROPE, w_t.shape[2]), w_t.dtype)
    w_mla = jnp.concatenate([w_t[:, :kr0], pad, w_kr, w_kr[:, ROPE_SWAP]], axis=1)
    return w_mla.astype(jnp.bfloat16), _cast_rows(w_t, kr1)


def _prep_w_uq(w_uq):
    depth, r, _ = w_uq.shape
    w = w_uq.reshape(depth, r, MLA_HEADS, MLA_NOPE + MLA_ROPE)
    rope = w[..., MLA_NOPE:]
    w = jnp.concatenate([w[..., :MLA_NOPE], rope, rope[..., ROPE_SWAP]], axis=-1)
    return w.reshape(depth, r, MLA_HEADS * MLA_HEAD_W).astype(jnp.bfloat16)


def _prep_w_ukv(w_ukv):
    depth, r, _ = w_ukv.shape
    w = w_ukv.reshape(depth, r, MLA_HEADS, MLA_NOPE + MLA_V)
    k = jnp.concatenate([w[..., :MLA_NOPE], jnp.zeros_like(w[..., :MLA_HEAD_W - MLA_NOPE])], axis=-1)
    k = k.reshape(depth, r, MLA_HEADS * MLA_HEAD_W)
    v = w[..., MLA_NOPE:].reshape(depth, r, MLA_HEADS * MLA_V)
    return jnp.concatenate([k, v], axis=-1).astype(jnp.bfloat16)


def _rope_tables(n_ctx, s):
    tok = jnp.arange(s)
    row = (tok // GRID_W).astype(jnp.float32)
    col = (tok % GRID_W).astype(jnp.float32)
    half = MLA_ROPE // 2
    inv = 1.0 / (ROPE_BASE ** (jnp.arange(0, half, 2, dtype=jnp.float32) / half))
    ang_r = row[:, None] * inv
    ang_c = col[:, None] * inv
    cos = jnp.concatenate([jnp.cos(ang_r), jnp.cos(ang_r), jnp.cos(ang_c), jnp.cos(ang_c)], axis=-1)
    sin = jnp.concatenate([-jnp.sin(ang_r), jnp.sin(ang_r), -jnp.sin(ang_c), jnp.sin(ang_c)], axis=-1)
    cos = jnp.concatenate([jnp.ones((n_ctx, MLA_ROPE), jnp.float32), cos], axis=0)
    sin = jnp.concatenate([jnp.zeros((n_ctx, MLA_ROPE), jnp.float32), sin], axis=0)
    t = n_ctx + s
    tail = MLA_HEAD_W - MLA_NOPE - MLA_ROPE
    cos_t = jnp.concatenate([jnp.ones((t, MLA_NOPE), jnp.float32), cos, jnp.zeros((t, tail), jnp.float32)], axis=-1)
    sin_t = jnp.concatenate([jnp.zeros((t, MLA_NOPE), jnp.float32), sin, jnp.zeros((t, tail), jnp.float32)], axis=-1)
    return cos_t, sin_t


def _prep_lru_gates(w_gate, b_gate):
    depth = w_gate.shape[0]
    per = LRU_COLS // LRU_BLOCK
    n_col = LRU_BLOCKS // per
    eye = jnp.eye(per, dtype=w_gate.dtype)

    def block_diag(w):
        w = w.reshape(depth, 2, n_col, per, LRU_BLOCK, LRU_BLOCK)
        w = w[:, :, :, :, :, None, :] * eye[None, None, None, :, None, :, None]
        return w.reshape(depth, 2, n_col, LRU_COLS, LRU_COLS)

    wg = jnp.concatenate([block_diag(w_gate[..., :LRU_BLOCK]), block_diag(w_gate[..., LRU_BLOCK:])], axis=-1)
    bg = b_gate.reshape(depth, 2, LRU_BLOCKS, 2 * LRU_BLOCK)
    bg = jnp.concatenate([bg[..., :LRU_BLOCK].reshape(depth, 2, n_col, 1, LRU_COLS),
                          bg[..., LRU_BLOCK:].reshape(depth, 2, n_col, 1, LRU_COLS)], axis=-1)
    return wg.astype(jnp.bfloat16), bg


def kernel(x, c, ctx, c_ctx, ada_w, ada_b, norm_g, w_in, mla_q_norm_g, mla_kv_norm_g, mla_w_uq, mla_w_ukv, na_rel_bias, lru_conv_w, lru_conv_b, lru_w_gate, lru_b_gate, lru_lambda, w_branch, w_out, final_norm_g):
    b, s, d = x.shape
    n_ctx = ctx.shape[1]
    depth = ada_w.shape[0]
    rows = s // GRID_W

    w_mla_p, w_rest_p = _prep_w_in(w_in)
    wq_p = _prep_w_uq(mla_w_uq)
    wkv_p = _prep_w_ukv(mla_w_ukv)
    cos_t, sin_t = _rope_tables(n_ctx, s)
    bm = _na_bias_tables(na_rel_bias, rows)
    wg_p, bg_p = _prep_lru_gates(lru_w_gate, lru_b_gate)
    wb_p = w_branch.astype(jnp.bfloat16)
    wo_p = w_out.astype(jnp.bfloat16)

    n_cond = -(-(b + 1) // SUBLANES) * SUBLANES
    cond = jnp.concatenate([c, c_ctx[None, :], jnp.zeros((n_cond - b - 1, d), c.dtype)], axis=0)
    mod = _ada_mod(cond, ada_w, ada_b)

    xa = jnp.concatenate([ctx, x], axis=1)
    for l in range(depth):
        sh_l, sc_l, gt_l = (mod[l, :b, k * d:(k + 1) * d][:, None, :] for k in range(3))
        sh_c, sc_c, gt_c = (mod[l, b:b + 1, k * d:(k + 1) * d] for k in range(3))
        p = _inproj(xa, norm_g[l][None, :], sc_l, sh_l, sc_c, sh_c, w_mla_p, w_rest_p, l, n_ctx)
        q, k, v = _mla_prep(p, mla_q_norm_g[l][None, :], mla_kv_norm_g[l][None, :],
                            wq_p[l], wkv_p[l], cos_t, sin_t)
        o_a = _mla_attn(q, k, v, n_ctx)
        o_c = _lru(p, lru_conv_w[l], lru_conv_b[l][None, :], wg_p[l], bg_p[l],
                   lru_lambda[l][:, None, :], n_ctx)
        o_b = _na_attn(p, bm, l, n_ctx)
        z = _merge(o_a, o_b, o_c, p, wb_p, l)
        if l < depth - 1:
            xa = _outproj(z, wo_p, xa, gt_l, gt_c, l, n_ctx)
    return _outproj_final(z, wo_p, xa, gt_l, final_norm_g[None, :], depth - 1, n_ctx)
```

```python
import functools

import numpy as np
import jax
import jax.numpy as jnp
from jax import lax
from jax.experimental import pallas as pl
from jax.experimental.pallas import tpu as pltpu

GRID_W = 64
MLA_HEADS = 8
MLA_Q_LORA = 512
MLA_KV_LORA = 256
MLA_NOPE = 128
MLA_ROPE = 64
MLA_V = 128
NA_HEADS = 16
NA_HEAD_DIM = 64
NA_WIN_ROWS = 8
NA_WIN_COLS = 16
LRU_WIDTH = 1024
LRU_BLOCKS = 16
LRU_BLOCK = LRU_WIDTH // LRU_BLOCKS
LRU_C = 8.0
CONV_WIDTH = 4
N_BRANCH = 3
BRANCH_W = 1024
ROPE_BASE = 10000.0
EPS = 1e-6

LANES = 128
SUBLANES = 8
VMEM_LIMIT_BYTES = 56 * 1024 * 1024

MLA_HEAD_W = 2 * LANES
Q_OFF = 0
KV_OFF = Q_OFF + MLA_Q_LORA
KR_OFF = KV_OFF + MLA_KV_LORA
KR_W = MLA_HEAD_W
MLA_IN_W = KR_OFF + KR_W
NA_W = NA_HEADS * NA_HEAD_DIM
NAQ_OFF = MLA_IN_W
NAK_OFF = NAQ_OFF + NA_W
NAV_OFF = NAK_OFF + NA_W
LRU_OFF = NAV_OFF + NA_W
GP_OFF = LRU_OFF + LRU_WIDTH
MG_OFF = GP_OFF + N_BRANCH * BRANCH_W
NEG_INF = -1e30

_Q = MLA_ROPE // 4
ROPE_SWAP = np.concatenate([np.arange(_Q, 2 * _Q), np.arange(0, _Q),
                            np.arange(3 * _Q, 4 * _Q), np.arange(2 * _Q, 3 * _Q)])

LOG2_E = float(np.log2(np.e))
MLA_LOG2_SCALE = float((MLA_NOPE + MLA_ROPE) ** -0.5 * LOG2_E)
NA_LOG2_SCALE = float(NA_HEAD_DIM ** -0.5 * LOG2_E)
MLA_HEADS_PER_STEP = 4
MLA_KEY_CHUNK = 512
NA_HEADS_PER_STEP = 8
NA_Q_ROWS = 4
NA_KEY_ROWS = NA_Q_ROWS + NA_WIN_ROWS - 1
LRU_CHUNK = 256
LRU_COLS = 512
LRU_GROUP_UNROLL = 8


def _cparams(*sem):
    return pltpu.CompilerParams(dimension_semantics=sem, vmem_limit_bytes=VMEM_LIMIT_BYTES)


def _dot(a, b):
    return jnp.dot(a, b, preferred_element_type=jnp.float32)


def _dot_nt(a, b):
    return lax.dot_general(a, b, (((1,), (1,)), ((), ())), preferred_element_type=jnp.float32)


def _rms(x, g):
    return x * lax.rsqrt(jnp.mean(x * x, axis=-1, keepdims=True) + EPS) * g


def _row_tile(t, target):
    best = None
    for cand in range(2 * SUBLANES, min(t, target) + 1, 2 * SUBLANES):
        if t % cand == 0:
            best = cand
    assert best is not None
    return best


def _ada_kernel(c_ref, w_ref, b_ref, o_ref):
    c = c_ref[...]
    s = (c * jax.nn.sigmoid(c)).astype(jnp.bfloat16)
    o_ref[0] = _dot(s, w_ref[0].astype(jnp.bfloat16)) + b_ref[0]


def _ada_mod(cond, ada_w, ada_b):
    depth, d, n = ada_w.shape
    rows = cond.shape[0]
    tn = 1024
    return pl.pallas_call(
        _ada_kernel,
        grid=(depth, n // tn),
        in_specs=[pl.BlockSpec((rows, d), lambda l, j: (0, 0)),
                  pl.BlockSpec((1, d, tn), lambda l, j: (l, 0, j)),
                  pl.BlockSpec((1, 1, tn), lambda l, j: (l, 0, j))],
        out_specs=pl.BlockSpec((1, rows, tn), lambda l, j: (l, 0, j)),
        out_shape=jax.ShapeDtypeStruct((depth, rows, n), jnp.float32),
        compiler_params=_cparams("parallel", "parallel"),
        name="ada_mod",
    )(cond, ada_w, ada_b.reshape(depth, 1, n))


def _inproj_kernel(x_ref, g_ref, scl_ref, shl_ref, scc_ref, shc_ref, wa_ref, wb_ref, o_ref, h_ref,
                   *, n_ctx):
    i = pl.program_id(1)
    j = pl.program_id(2)
    tm = x_ref.shape[1]

    @pl.when(j == 0)
    def _():
        y = _rms(x_ref[0], g_ref[...])
        row = i * tm + lax.broadcasted_iota(jnp.int32, (tm, 1), 0)
        is_ctx = row < n_ctx
        sc = jnp.where(is_ctx, scc_ref[...], scl_ref[0])
        sh = jnp.where(is_ctx, shc_ref[...], shl_ref[0])
        h_ref[...] = (y * (1.0 + sc) + sh).astype(h_ref.dtype)
        o_ref[0] = _dot_nt(h_ref[...], wa_ref[...]).astype(o_ref.dtype)

    @pl.when(j > 0)
    def _():
        o_ref[0] = _dot_nt(h_ref[...], wb_ref[...]).astype(o_ref.dtype)


def _inproj(xa, g, sc_l, sh_l, sc_c, sh_c, w_mla, w_rest, layer, n_ctx):
    b, t, d = xa.shape
    tn = w_mla.shape[1]
    n_rest = w_rest.shape[1]
    assert n_rest % tn == 0
    tm = _row_tile(t, 1088)
    vec = pl.BlockSpec((1, d), lambda bi, i, j: (0, 0))
    per_b = pl.BlockSpec((1, 1, d), lambda bi, i, j: (bi, 0, 0))
    return pl.pallas_call(
        functools.partial(_inproj_kernel, n_ctx=n_ctx),
        grid=(b, t // tm, 1 + n_rest // tn),
        in_specs=[pl.BlockSpec((1, tm, d), lambda bi, i, j: (bi, i, 0)),
                  vec, per_b, per_b, vec, vec,
                  pl.BlockSpec((None, tn, d), lambda bi, i, j: (layer, 0, 0)),
                  pl.BlockSpec((None, tn, d), lambda bi, i, j: (layer, jnp.maximum(j - 1, 0), 0))],
        out_specs=pl.BlockSpec((1, tm, tn), lambda bi, i, j: (bi, i, j)),
        out_shape=jax.ShapeDtypeStruct((b, t, tn + n_rest), jnp.bfloat16),
        scratch_shapes=[pltpu.VMEM((tm, d), jnp.bfloat16)],
        compiler_params=_cparams("parallel", "parallel", "arbitrary"),
        name="inproj",
    )(xa, g, sc_l, sh_l, sc_c, sh_c, w_mla, w_rest)


def _mla_prep_kernel(p_ref, gq_ref, gkv_ref, wq_ref, wkv_ref, cos_ref, sin_ref,
                     q_ref, k_ref, v_ref):
    p = p_ref[0].astype(jnp.float32)
    cos = cos_ref[...]
    sin = sin_ref[...]
    shift = MLA_HEAD_W - MLA_ROPE

    def rope(a):
        return a * cos + pltpu.roll(a, shift, 1) * sin

    hq = _rms(p[:, Q_OFF:Q_OFF + MLA_Q_LORA], gq_ref[...]).astype(jnp.bfloat16)
    hkv = _rms(p[:, KV_OFF:KV_OFF + MLA_KV_LORA], gkv_ref[...]).astype(jnp.bfloat16)
    k_rope = rope(p[:, KR_OFF:KR_OFF + KR_W])
    q = _dot(hq, wq_ref[...])
    kv = _dot(hkv, wkv_ref[...])
    k_w = MLA_HEADS * MLA_HEAD_W
    ones = jnp.ones((p.shape[0], MLA_HEAD_W - MLA_V), v_ref.dtype)
    for h in range(MLA_HEADS):
        sl = slice(h * MLA_HEAD_W, (h + 1) * MLA_HEAD_W)
        q_ref[0, :, sl] = (rope(q[:, sl]) * MLA_LOG2_SCALE).astype(q_ref.dtype)
        k_ref[0, :, sl] = (kv[:, sl] + k_rope).astype(k_ref.dtype)
        v_ref[0, :, h * MLA_HEAD_W:h * MLA_HEAD_W + MLA_V] = (
            kv[:, k_w + h * MLA_V:k_w + (h + 1) * MLA_V].astype(v_ref.dtype))
        v_ref[0, :, h * MLA_HEAD_W + MLA_V:(h + 1) * MLA_HEAD_W] = ones


def _mla_prep(p, gq, gkv, wq, wkv, cos_t, sin_t):
    b, t, _ = p.shape
    tm = _row_tile(t, 272)
    qk_w = MLA_HEADS * MLA_HEAD_W
    v_w = MLA_HEADS * MLA_HEAD_W
    full = lambda a: pl.BlockSpec(a.shape, lambda bi, i: (0,) * a.ndim)
    return pl.pallas_call(
        _mla_prep_kernel,
        grid=(b, t // tm),
        in_specs=[pl.BlockSpec((1, tm, MLA_IN_W), lambda bi, i: (bi, i, 0)),
                  full(gq), full(gkv), full(wq), full(wkv),
                  pl.BlockSpec((tm, MLA_HEAD_W), lambda bi, i: (i, 0)),
                  pl.BlockSpec((tm, MLA_HEAD_W), lambda bi, i: (i, 0))],
        out_specs=[pl.BlockSpec((1, tm, qk_w), lambda bi, i: (bi, i, 0)),
                   pl.BlockSpec((1, tm, qk_w), lambda bi, i: (bi, i, 0)),
                   pl.BlockSpec((1, tm, v_w), lambda bi, i: (bi, i, 0))],
        out_shape=[jax.ShapeDtypeStruct((b, t, qk_w), jnp.bfloat16),
                   jax.ShapeDtypeStruct((b, t, qk_w), jnp.bfloat16),
                   jax.ShapeDtypeStruct((b, t, v_w), jnp.bfloat16)],
        compiler_params=_cparams("parallel", "parallel"),
        name="mla_prep",
    )(p, gq, gkv, wq, wkv, cos_t, sin_t)


def _mla_attend(q_ref, k_ref, v_ref, o_ref, nk):
    hw = MLA_HEAD_W
    outs = []
    for hh in range(MLA_HEADS_PER_STEP):
        lanes = slice(hh * hw, (hh + 1) * hw)
        q = q_ref[0, :, lanes]
        m = acc = None
        for j0 in range(0, nk, MLA_KEY_CHUNK):
            j1 = min(j0 + MLA_KEY_CHUNK, nk)
            s = _dot_nt(q, k_ref[0, j0:j1, lanes])
            m_j = jnp.max(s, axis=-1, keepdims=True)
            m_new = m_j if m is None else jnp.maximum(m, m_j)
            e = jnp.exp2(s - m_new).astype(jnp.bfloat16)
            pv = _dot(e, v_ref[0, j0:j1, lanes])
            acc = pv if acc is None else acc * jnp.exp2(m - m_new) + pv
            m = m_new
        outs.append(acc[:, :MLA_V] / acc[:, MLA_V:])
    o_ref[0] = jnp.concatenate(outs, axis=1).astype(o_ref.dtype)


def _mla_attn_kernel(q_ref, k_ref, v_ref, o_ref, *, n_ctx):
    i = pl.program_id(2)
    tq = q_ref.shape[1]
    t = k_ref.shape[1]

    @pl.when(i < n_ctx // tq)
    def _():
        _mla_attend(q_ref, k_ref, v_ref, o_ref, n_ctx)

    @pl.when(i >= n_ctx // tq)
    def _():
        _mla_attend(q_ref, k_ref, v_ref, o_ref, t)


def _mla_attn(q, k, v, n_ctx):
    b, t, _ = q.shape
    tq = 2 * LANES
    hps = MLA_HEADS_PER_STEP
    assert n_ctx % tq == 0 and t % tq == 0 and MLA_HEADS % hps == 0 and MLA_HEAD_W == 2 * MLA_V
    return pl.pallas_call(
        functools.partial(_mla_attn_kernel, n_ctx=n_ctx),
        grid=(b, MLA_HEADS // hps, t // tq),
        in_specs=[pl.BlockSpec((1, tq, hps * MLA_HEAD_W), lambda bi, h, i: (bi, i, h)),
                  pl.BlockSpec((1, t, hps * MLA_HEAD_W), lambda bi, h, i: (bi, 0, h)),
                  pl.BlockSpec((1, t, hps * MLA_HEAD_W), lambda bi, h, i: (bi, 0, h))],
        out_specs=pl.BlockSpec((1, tq, hps * MLA_V), lambda bi, h, i: (bi, i, h)),
        out_shape=jax.ShapeDtypeStruct((b, t, MLA_HEADS * MLA_V), jnp.bfloat16),
        compiler_params=_cparams("parallel", "parallel", "arbitrary"),
        name="mla_attn",
    )(q, k, v)


def _na_plan(rows):
    kr = min(NA_WIN_ROWS, rows)
    assert rows % NA_Q_ROWS == 0 and rows >= NA_KEY_ROWS
    starts, patterns, var_of_block = [], [], []
    for r in range(rows // NA_Q_ROWS):
        start = int(np.clip(NA_Q_ROWS * r - kr // 2, 0, rows - NA_KEY_ROWS))
        valid = np.zeros((NA_Q_ROWS, NA_KEY_ROWS), bool)
        dr0 = np.zeros((NA_Q_ROWS,), np.int64)
        for a in range(NA_Q_ROWS):
            i = NA_Q_ROWS * r + a
            r0 = int(np.clip(i - kr // 2, 0, rows - kr))
            assert start <= r0 and r0 + kr <= start + NA_KEY_ROWS
            krow = start + np.arange(NA_KEY_ROWS)
            valid[a] = (r0 <= krow) & (krow < r0 + kr)
            dr0[a] = start - i + (NA_WIN_ROWS - 1)
        key = (valid.tobytes(), dr0.tobytes())
        for v, (pk, _, _) in enumerate(patterns):
            if pk == key:
                var_of_block.append(v)
                break
        else:
            var_of_block.append(len(patterns))
            patterns.append((key, valid, dr0))
        starts.append(start)
    valid_r = np.stack([p[1] for p in patterns])
    dr0 = np.stack([p[2] for p in patterns])
    return np.array(starts), np.array(var_of_block), valid_r, dr0


def _na_bias_tables(rel_bias, rows):
    _, _, valid_r, dr0 = _na_plan(rows)
    depth, heads, n_dr, n_dc = rel_bias.shape
    cols = np.arange(GRID_W)
    col_start = np.clip(cols - NA_WIN_COLS // 2, 0, GRID_W - NA_WIN_COLS)
    valid_c = (cols[None, :] >= col_start[:, None]) & (cols[None, :] < col_start[:, None] + NA_WIN_COLS)
    dc = cols[None, :] - cols[:, None] + (NA_WIN_COLS - 1)
    onehot = (valid_c[:, :, None] & (dc[:, :, None] == np.arange(n_dc))).astype(np.float32)
    e1 = jnp.einsum('lhrd,jcd->lhjrc', rel_bias, jnp.asarray(onehot), precision=lax.Precision.HIGHEST)
    e1 = jnp.where(jnp.asarray(valid_c)[None, None, :, None, :], e1 * LOG2_E, NEG_INF)
    flat = e1.reshape(depth, heads, GRID_W, n_dr * GRID_W)
    n_win = NA_KEY_ROWS * GRID_W
    pad = jnp.full((depth, heads, GRID_W, n_win), NEG_INF, flat.dtype)
    flat = jnp.concatenate([pad, flat, pad], axis=-1)
    assert dr0.min() > -NA_KEY_ROWS and dr0.max() < n_dr
    blocks = []
    for v in range(valid_r.shape[0]):
        for a in range(NA_Q_ROWS):
            off = n_win + int(dr0[v, a]) * GRID_W
            keep = np.repeat(valid_r[v, a], GRID_W)
            blocks.append(jnp.where(jnp.asarray(keep), flat[..., off:off + n_win], NEG_INF))
    bm = jnp.stack(blocks, axis=2)
    return bm.reshape(depth, heads, valid_r.shape[0], NA_Q_ROWS * GRID_W, n_win)


def _na_attn_kernel(var_ref, ks_ref, q_ref, k_ref, v_ref, bm_ref, o_ref, vx_ref, *, n_ctx):
    del var_ref
    i = pl.program_id(2)
    tq = q_ref.shape[1]
    n_win = bm_ref.shape[-1]
    per_group = LANES // NA_HEAD_DIM
    n_group = NA_HEADS_PER_STEP // per_group
    low = lax.broadcasted_iota(jnp.int32, (1, LANES), 1) < NA_HEAD_DIM
    zero = jnp.zeros((), q_ref.dtype)

    @pl.when(i == 0)
    def _():
        for g in range(n_group):
            vx_ref[g, :, :LANES] = v_ref[0, :, g * LANES:(g + 1) * LANES]
            vx_ref[g, :, LANES:] = jnp.ones((vx_ref.shape[1], LANES), vx_ref.dtype)

    def softmax_pv(parts):
        m = None
        for s, _ in parts:
            m_s = jnp.max(s, axis=-1, keepdims=True)
            m = m_s if m is None else jnp.maximum(m, m_s)
        pv = None
        for s, vals in parts:
            pv_s = _dot(jnp.exp2(s - m).astype(jnp.bfloat16), vals)
            pv = pv_s if pv is None else pv + pv_s
        return pv[:, :LANES] / pv[:, LANES:]

    def run(window):
        if window:
            ks = pl.multiple_of(ks_ref[i], GRID_W)
        for g in range(n_group):
            lanes = slice(g * LANES, (g + 1) * LANES)
            q = q_ref[0, :, lanes]
            k_ctx = k_ref[0, :n_ctx, lanes]
            outs = []
            for hh in range(per_group):
                qh = jnp.where(low if hh == 0 else jnp.logical_not(low), q, zero)
                parts = [(_dot_nt(qh, k_ctx), vx_ref[g, :n_ctx, :])]
                if window:
                    s_w = _dot_nt(qh, k_ref[0, pl.ds(ks, n_win), lanes]) + bm_ref[g * per_group + hh, 0]
                    parts.append((s_w, vx_ref[g, pl.ds(ks, n_win), :]))
                outs.append(softmax_pv(parts))
            o_ref[0, :, lanes] = jnp.where(low, outs[0], outs[1]).astype(o_ref.dtype)

    @pl.when(i < n_ctx // tq)
    def _():
        run(False)

    @pl.when(i >= n_ctx // tq)
    def _():
        run(True)


def _na_attn(p, bm, layer, n_ctx):
    b, t, _ = p.shape
    tq = NA_Q_ROWS * GRID_W
    hps = NA_HEADS_PER_STEP
    gw = hps * NA_HEAD_DIM
    assert n_ctx % tq == 0 and (t - n_ctx) % tq == 0 and LANES == 2 * NA_HEAD_DIM
    rows = (t - n_ctx) // GRID_W
    starts, var_of_block, _, _ = _na_plan(rows)
    n_cb = n_ctx // tq
    var_tab = jnp.asarray(np.concatenate([np.zeros(n_cb, np.int32), var_of_block]).astype(np.int32))
    ks_tab = jnp.asarray(np.concatenate([np.zeros(n_cb, np.int32), n_ctx + starts * GRID_W]).astype(np.int32))
    n_win = NA_KEY_ROWS * GRID_W
    qc, kc, vc = NAQ_OFF // gw, NAK_OFF // gw, NAV_OFF // gw
    grid_spec = pltpu.PrefetchScalarGridSpec(
        num_scalar_prefetch=2,
        grid=(b, NA_HEADS // hps, t // tq),
        in_specs=[pl.BlockSpec((1, tq, gw), lambda bi, h, i, var, ks: (bi, i, qc + h)),
                  pl.BlockSpec((1, t, gw), lambda bi, h, i, var, ks: (bi, 0, kc + h)),
                  pl.BlockSpec((1, t, gw), lambda bi, h, i, var, ks: (bi, 0, vc + h)),
                  pl.BlockSpec((None, hps, 1, tq, n_win),
                               lambda bi, h, i, var, ks: (layer, h, var[i], 0, 0))],
        out_specs=pl.BlockSpec((1, tq, gw), lambda bi, h, i, var, ks: (bi, i, h)),
        scratch_shapes=[pltpu.VMEM((gw // LANES, t, 2 * LANES), jnp.bfloat16)],
    )
    return pl.pallas_call(
        functools.partial(_na_attn_kernel, n_ctx=n_ctx),
        grid_spec=grid_spec,
        out_shape=jax.ShapeDtypeStruct((b, t, NA_W), jnp.bfloat16),
        compiler_params=_cparams("parallel", "parallel", "arbitrary"),
        name="na_attn",
    )(var_tab, ks_tab, p, p, p, bm)


def _lru_kernel(x_ref, cw_ref, cb_ref, wg_ref, bg_ref, lam_ref, o_ref, xc_ref, a_ref, u_ref, *, n_ctx):
    t, wc = xc_ref.shape
    r = LRU_CHUNK
    n_chunk = t // r
    n_cc = n_ctx // r
    left = CONV_WIDTH // 2
    row = lax.broadcasted_iota(jnp.int32, (r, wc), 0)

    def conv_chunk(c, carry):
        t0 = pl.multiple_of(c * r, r)
        first = jnp.logical_or(c == 0, c == n_cc)
        last = jnp.logical_or(c == n_cc - 1, c == n_chunk - 1)
        halo = 2 * SUBLANES
        main = x_ref[0, pl.ds(t0, r), :].astype(jnp.float32)
        prev = x_ref[0, pl.ds(pl.multiple_of(jnp.maximum(t0 - halo, 0), halo), halo), :]
        nxt = x_ref[0, pl.ds(pl.multiple_of(jnp.minimum(t0 + r, t - halo), halo), halo), :]
        prev = jnp.where(first, 0.0, prev.astype(jnp.float32))
        nxt = jnp.where(last, 0.0, nxt.astype(jnp.float32))
        y = cb_ref[...] + jnp.zeros((r, wc), jnp.float32)
        for tap in range(CONV_WIDTH):
            off = tap - left
            if off == 0:
                xs = main
            elif off < 0:
                xs = pltpu.roll(main, -off, 0)
                for k in range(-off):
                    xs = jnp.where(row == k, prev[halo + off + k:halo + off + k + 1, :], xs)
            else:
                xs = pltpu.roll(main, r - off, 0)
                for k in range(off):
                    xs = jnp.where(row == r - off + k, nxt[k:k + 1, :], xs)
            y = y + xs * cw_ref[tap:tap + 1, :]
        xc_ref[pl.ds(t0, r), :] = y
        return carry

    lax.fori_loop(0, n_chunk, conv_chunk, 0)

    grow = lax.broadcasted_iota(jnp.int32, (SUBLANES, wc), 0)
    n_grp = r // SUBLANES

    def scan_direction(direction):
        reverse = direction == 1
        lam = lam_ref[direction]
        neg = -lam
        softplus = jnp.maximum(neg, 0.0) + jnp.log1p(jnp.exp(-jnp.abs(neg)))
        decay = -LRU_C * softplus

        def chunk_step(step, h):
            if reverse:
                c = jnp.where(step < n_cc, n_cc - 1 - step, n_chunk - 1 - (step - n_cc))
            else:
                c = step
            t0 = pl.multiple_of(c * r, r)
            xc = xc_ref[pl.ds(t0, r), :]
            g = _dot(xc.astype(jnp.bfloat16), wg_ref[direction, 0]) + bg_ref[direction, 0]
            rg = jax.nn.sigmoid(g[:, :wc])
            ig = jax.nn.sigmoid(g[:, wc:])
            log_a = decay * rg
            a = jnp.exp(log_a)
            a_ref[...] = a
            v = jnp.maximum(1.0 - a * a, 0.0)
            u_ref[...] = jnp.where(v > 0.0, v * lax.rsqrt(v), 0.0) * (ig * xc)

            def group_step(gi, h):
                g0 = (n_grp - 1 - gi) if reverse else gi
                off = pl.multiple_of(g0 * SUBLANES, SUBLANES)
                a = a_ref[pl.ds(off, SUBLANES), :]
                u = u_ref[pl.ds(off, SUBLANES), :]
                for s in (1, 2, 4):
                    if reverse:
                        keep = grow < SUBLANES - s
                        a_s = pltpu.roll(a, SUBLANES - s, 0)
                        u_s = pltpu.roll(u, SUBLANES - s, 0)
                    else:
                        keep = grow >= s
                        a_s = pltpu.roll(a, s, 0)
                        u_s = pltpu.roll(u, s, 0)
                    u = jnp.where(keep, a * u_s + u, u)
                    a = jnp.where(keep, a * a_s, a)
                hh = a * h + u
                dst = pl.ds(pl.multiple_of(t0 + off, SUBLANES), SUBLANES)
                if reverse:
                    o_ref[0, dst, :] = o_ref[0, dst, :] + hh
                    return hh[0:1, :]
                o_ref[0, dst, :] = hh
                return hh[SUBLANES - 1:SUBLANES, :]

            return lax.fori_loop(0, n_grp, group_step, h, unroll=LRU_GROUP_UNROLL)

        lax.fori_loop(0, n_chunk, chunk_step, jnp.zeros((1, wc), jnp.float32))

    scan_direction(0)
    scan_direction(1)


def _lru(p, conv_w, conv_b, wg, bg, lam, n_ctx):
    b, t, _ = p.shape
    wc = LRU_COLS
    assert t % LRU_CHUNK == 0 and n_ctx % LRU_CHUNK == 0 and n_ctx > 0 and t > n_ctx
    xcol = LRU_OFF // wc
    return pl.pallas_call(
        functools.partial(_lru_kernel, n_ctx=n_ctx),
        grid=(b, LRU_WIDTH // wc),
        in_specs=[pl.BlockSpec((1, t, wc), lambda bi, c: (bi, 0, xcol + c)),
                  pl.BlockSpec((CONV_WIDTH, wc), lambda bi, c: (0, c)),
                  pl.BlockSpec((1, wc), lambda bi, c: (0, c)),
                  pl.BlockSpec((2, 1, wc, 2 * wc), lambda bi, c: (0, c, 0, 0)),
                  pl.BlockSpec((2, 1, 1, 2 * wc), lambda bi, c: (0, c, 0, 0)),
                  pl.BlockSpec((2, 1, wc), lambda bi, c: (0, 0, c))],
        out_specs=pl.BlockSpec((1, t, wc), lambda bi, c: (bi, 0, c)),
        out_shape=jax.ShapeDtypeStruct((b, t, LRU_WIDTH), jnp.float32),
        scratch_shapes=[pltpu.VMEM((t, wc), jnp.float32),
                        pltpu.VMEM((LRU_CHUNK, wc), jnp.float32),
                        pltpu.VMEM((LRU_CHUNK, wc), jnp.float32)],
        compiler_params=_cparams("parallel", "parallel"),
        name="rglru",
    )(p, conv_w, conv_b, wg, bg, lam)


def _merge_kernel(oa_ref, ob_ref, oc_ref, gpa_ref, gpb_ref, gpc_ref, mga_ref, mgb_ref, mgc_ref,
                  wb_ref, z_ref):
    z = None
    for n, (o_ref, gp_ref, mg_ref) in enumerate(((oa_ref, gpa_ref, mga_ref),
                                                 (ob_ref, gpb_ref, mgb_ref),
                                                 (oc_ref, gpc_ref, mgc_ref))):
        gp = gp_ref[0].astype(jnp.float32)
        tkn = (o_ref[0].astype(jnp.float32) * (gp * jax.nn.sigmoid(gp))).astype(jnp.bfloat16)
        y = jax.nn.sigmoid(mg_ref[0].astype(jnp.float32)) * _dot(tkn, wb_ref[n])
        z = y if z is None else z + y
    z_ref[0] = z.astype(z_ref.dtype)


def _merge(oa, ob, oc, p, wb, layer):
    b, t, _ = oa.shape
    d = wb.shape[-1]
    tm = _row_tile(t, 544)
    gpc, mgc = GP_OFF // BRANCH_W, MG_OFF // d
    assert GP_OFF % BRANCH_W == 0 and MG_OFF % d == 0
    o_spec = pl.BlockSpec((1, tm, BRANCH_W), lambda bi, i: (bi, i, 0))
    gp_specs = [pl.BlockSpec((1, tm, BRANCH_W), functools.partial(lambda bi, i, n: (bi, i, gpc + n), n=n))
                for n in range(N_BRANCH)]
    mg_specs = [pl.BlockSpec((1, tm, d), functools.partial(lambda bi, i, n: (bi, i, mgc + n), n=n))
                for n in range(N_BRANCH)]
    return pl.pallas_call(
        _merge_kernel,
        grid=(b, t // tm),
        in_specs=[o_spec, o_spec, o_spec, *gp_specs, *mg_specs,
                  pl.BlockSpec((None,) + wb.shape[1:], lambda bi, i: (layer, 0, 0, 0),
                               pipeline_mode=pl.Buffered(1))],
        out_specs=pl.BlockSpec((1, tm, d), lambda bi, i: (bi, i, 0)),
        out_shape=jax.ShapeDtypeStruct((b, t, d), jnp.bfloat16),
        compiler_params=_cparams("parallel", "parallel"),
        name="merge",
    )(oa, ob, oc, p, p, p, p, p, p, wb)


def _outproj_kernel(z_ref, w_ref, x_ref, gl_ref, gc_ref, o_ref, *, n_ctx):
    i = pl.program_id(1)
    tm = x_ref.shape[1]
    row = i * tm + lax.broadcasted_iota(jnp.int32, (tm, 1), 0)
    gate = jnp.where(row < n_ctx, gc_ref[...], gl_ref[0])
    o_ref[0] = x_ref[0] + gate * _dot(z_ref[0], w_ref[...])


def _outproj(z, w, xa, gt_l, gt_c, layer, n_ctx):
    b, t, d = xa.shape
    tm = _row_tile(t, 544)
    return pl.pallas_call(
        functools.partial(_outproj_kernel, n_ctx=n_ctx),
        grid=(b, t // tm),
        in_specs=[pl.BlockSpec((1, tm, d), lambda bi, i: (bi, i, 0)),
                  pl.BlockSpec((None, d, d), lambda bi, i: (layer, 0, 0), pipeline_mode=pl.Buffered(1)),
                  pl.BlockSpec((1, tm, d), lambda bi, i: (bi, i, 0)),
                  pl.BlockSpec((1, 1, d), lambda bi, i: (bi, 0, 0)),
                  pl.BlockSpec((1, d), lambda bi, i: (0, 0))],
        out_specs=pl.BlockSpec((1, tm, d), lambda bi, i: (bi, i, 0)),
        out_shape=jax.ShapeDtypeStruct((b, t, d), jnp.float32),
        compiler_params=_cparams("parallel", "parallel"),
        name="outproj",
    )(z, w, xa, gt_l, gt_c)


def _outproj_final_kernel(z_ref, w_ref, x_ref, gl_ref, g_ref, o_ref):
    o_ref[0] = _rms(x_ref[0] + gl_ref[0] * _dot(z_ref[0], w_ref[...]), g_ref[...])


def _outproj_final(z, w, xa, gt_l, g, layer, n_ctx):
    b, t, d = xa.shape
    s = t - n_ctx
    tm = _row_tile(int(np.gcd(s, n_ctx)), 512)
    skip = n_ctx // tm
    return pl.pallas_call(
        _outproj_final_kernel,
        grid=(b, s // tm),
        in_specs=[pl.BlockSpec((1, tm, d), lambda bi, i: (bi, i + skip, 0)),
                  pl.BlockSpec((None, d, d), lambda bi, i: (layer, 0, 0), pipeline_mode=pl.Buffered(1)),
                  pl.BlockSpec((1, tm, d), lambda bi, i: (bi, i + skip, 0)),
                  pl.BlockSpec((1, 1, d), lambda bi, i: (bi, 0, 0)),
                  pl.BlockSpec((1, d), lambda bi, i: (0, 0))],
        out_specs=pl.BlockSpec((1, tm, d), lambda bi, i: (bi, i, 0)),
        out_shape=jax.ShapeDtypeStruct((b, s, d), jnp.float32),
        compiler_params=_cparams("parallel", "parallel"),
        name="outproj_final",
    )(z, w, xa, gt_l, g)


def _cast_rows_kernel(x_ref, o_ref, *, n_scaled, scale):
    tr = x_ref.shape[0]
    row = pl.program_id(1) * tr + lax.broadcasted_iota(jnp.int32, (tr, 1), 0)
    x = x_ref[...]
    o_ref[...] = jnp.where(row < n_scaled, x * scale, x).astype(o_ref.dtype)


def _cast_rows(w_t, start, n_scaled, scale):
    depth, n, d = w_t.shape
    tr = start
    assert n % tr == 0 and tr % (2 * SUBLANES) == 0
    return pl.pallas_call(
        functools.partial(_cast_rows_kernel, n_scaled=n_scaled, scale=scale),
        grid=(depth, n // tr - 1),
        in_specs=[pl.BlockSpec((None, tr, d), lambda l, j: (l, j + 1, 0))],
        out_specs=pl.BlockSpec((None, tr, d), lambda l, j: (l, j, 0)),
        out_shape=jax.ShapeDtypeStruct((depth, n - start, d), jnp.bfloat16),
        compiler_params=_cparams("parallel", "parallel"),
        name="w_in_cast",
    )(w_t)


def _prep_w_in(w_in):
    w_t = jnp.swapaxes(w_in, 1, 2)
    kr0 = KV_OFF + MLA_KV_LORA
    kr1 = kr0 + MLA_ROPE
    w_kr = w_t[:, kr0:kr1]
    pad = jnp.zeros((w_t.shape[0], KR_W - 2 * MLA_ROPE, w_t.shape[2]), w_t.dtype)
    w_mla = jnp.concatenate([w_t[:, :kr0], pad, w_kr, w_kr[:, ROPE_SWAP]], axis=1)
    assert NAQ_OFF == MLA_IN_W
    return w_mla.astype(jnp.bfloat16), _cast_rows(w_t, kr1, NA_W, NA_LOG2_SCALE)


def _prep_w_uq(w_uq):
    depth, r, _ = w_uq.shape
    w = w_uq.reshape(depth, r, MLA_HEADS, MLA_NOPE + MLA_ROPE)
    rope = w[..., MLA_NOPE:]
    w = jnp.concatenate([w[..., :MLA_NOPE], rope, rope[..., ROPE_SWAP]], axis=-1)
    return w.reshape(depth, r, MLA_HEADS * MLA_HEAD_W).astype(jnp.bfloat16)


def _prep_w_ukv(w_ukv):
    depth, r, _ = w_ukv.shape
    w = w_ukv.reshape(depth, r, MLA_HEADS, MLA_NOPE + MLA_V)
    k = jnp.concatenate([w[..., :MLA_NOPE], jnp.zeros_like(w[..., :MLA_HEAD_W - MLA_NOPE])], axis=-1)
    k = k.reshape(depth, r, MLA_HEADS * MLA_HEAD_W)
    v = w[..., MLA_NOPE:].reshape(depth, r, MLA_HEADS * MLA_V)
    return jnp.concatenate([k, v], axis=-1).astype(jnp.bfloat16)


def _rope_tables(n_ctx, s):
    tok = jnp.arange(s)
    row = (tok // GRID_W).astype(jnp.float32)
    col = (tok % GRID_W).astype(jnp.float32)
    half = MLA_ROPE // 2
    inv = 1.0 / (ROPE_BASE ** (jnp.arange(0, half, 2, dtype=jnp.float32) / half))
    ang_r = row[:, None] * inv
    ang_c = col[:, None] * inv
    cos = jnp.concatenate([jnp.cos(ang_r), jnp.cos(ang_r), jnp.cos(ang_c), jnp.cos(ang_c)], axis=-1)
    sin = jnp.concatenate([-jnp.sin(ang_r), jnp.sin(ang_r), -jnp.sin(ang_c), jnp.sin(ang_c)], axis=-1)
    cos = jnp.concatenate([jnp.ones((n_ctx, MLA_ROPE), jnp.float32), cos], axis=0)
    sin = jnp.concatenate([jnp.zeros((n_ctx, MLA_ROPE), jnp.float32), sin], axis=0)
    t = n_ctx + s
    tail = MLA_HEAD_W - MLA_NOPE - MLA_ROPE
    cos_t = jnp.concatenate([jnp.ones((t, MLA_NOPE), jnp.float32), cos, jnp.zeros((t, tail), jnp.float32)], axis=-1)
    sin_t = jnp.concatenate([jnp.zeros((t, MLA_NOPE), jnp.float32), sin, jnp.zeros((t, tail), jnp.float32)], axis=-1)
    return cos_t, sin_t


def _prep_lru_gates(w_gate, b_gate):
    depth = w_gate.shape[0]
    per = LRU_COLS // LRU_BLOCK
    n_col = LRU_BLOCKS // per
    eye = jnp.eye(per, dtype=w_gate.dtype)

    def block_diag(w):
        w = w.reshape(depth, 2, n_col, per, LRU_BLOCK, LRU_BLOCK)
        w = w[:, :, :, :, :, None, :] * eye[None, None, None, :, None, :, None]
        return w.reshape(depth, 2, n_col, LRU_COLS, LRU_COLS)

    wg = jnp.concatenate([block_diag(w_gate[..., :LRU_BLOCK]), block_diag(w_gate[..., LRU_BLOCK:])], axis=-1)
    bg = b_gate.reshape(depth, 2, LRU_BLOCKS, 2 * LRU_BLOCK)
    bg = jnp.concatenate([bg[..., :LRU_BLOCK].reshape(depth, 2, n_col, 1, LRU_COLS),
                          bg[..., LRU_BLOCK:].reshape(depth, 2, n_col, 1, LRU_COLS)], axis=-1)
    return wg.astype(jnp.bfloat16), bg


def kernel(x, c, ctx, c_ctx, ada_w, ada_b, norm_g, w_in, mla_q_norm_g, mla_kv_norm_g, mla_w_uq, mla_w_ukv, na_rel_bias, lru_conv_w, lru_conv_b, lru_w_gate, lru_b_gate, lru_lambda, w_branch, w_out, final_norm_g):
    b, s, d = x.shape
    n_ctx = ctx.shape[1]
    depth = ada_w.shape[0]
    rows = s // GRID_W

    w_mla_p, w_rest_p = _prep_w_in(w_in)
    wq_p = _prep_w_uq(mla_w_uq)
    wkv_p = _prep_w_ukv(mla_w_ukv)
    cos_t, sin_t = _rope_tables(n_ctx, s)
    bm = _na_bias_tables(na_rel_bias, rows)
    wg_p, bg_p = _prep_lru_gates(lru_w_gate, lru_b_gate)
    wb_p = w_branch.astype(jnp.bfloat16)
    wo_p = w_out.astype(jnp.bfloat16)

    n_cond = -(-(b + 1) // SUBLANES) * SUBLANES
    cond = jnp.concatenate([c, c_ctx[None, :], jnp.zeros((n_cond - b - 1, d), c.dtype)], axis=0)
    mod = _ada_mod(cond, ada_w, ada_b)

    xa = jnp.concatenate([ctx, x], axis=1)
    for l in range(depth):
        sh_l, sc_l, gt_l = (mod[l, :b, k * d:(k + 1) * d][:, None, :] for k in range(3))
        sh_c, sc_c, gt_c = (mod[l, b:b + 1, k * d:(k + 1) * d] for k in range(3))
        p = _inproj(xa, norm_g[l][None, :], sc_l, sh_l, sc_c, sh_c, w_mla_p, w_rest_p, l, n_ctx)
        q, k, v = _mla_prep(p, mla_q_norm_g[l][None, :], mla_kv_norm_g[l][None, :],
                            wq_p[l], wkv_p[l], cos_t, sin_t)
        o_a = _mla_attn(q, k, v, n_ctx)
        o_c = _lru(p, lru_conv_w[l], lru_conv_b[l][None, :], wg_p[l], bg_p[l],
                   lru_lambda[l][:, None, :], n_ctx)
        o_b = _na_attn(p, bm, l, n_ctx)
        z = _merge(o_a, o_b, o_c, p, wb_p, l)
        if l < depth - 1:
            xa = _outproj(z, wo_p, xa, gt_l, gt_c, l, n_ctx)
    return _outproj_final(z, wo_p, xa, gt_l, final_norm_g[None, :], depth - 1, n_ctx)
```

```python
import functools

import numpy as np
import jax
import jax.numpy as jnp
from jax import lax
from jax.experimental import pallas as pl
from jax.experimental.pallas import tpu as pltpu

GRID_W = 64
MLA_HEADS = 8
MLA_Q_LORA = 512
MLA_KV_LORA = 256
MLA_NOPE = 128
MLA_ROPE = 64
MLA_V = 128
NA_HEADS = 16
NA_HEAD_DIM = 64
NA_WIN_ROWS = 8
NA_WIN_COLS = 16
LRU_WIDTH = 1024
LRU_BLOCKS = 16
LRU_BLOCK = LRU_WIDTH // LRU_BLOCKS
LRU_C = 8.0
CONV_WIDTH = 4
N_BRANCH = 3
BRANCH_W = 1024
ROPE_BASE = 10000.0
EPS = 1e-6

LANES = 128
SUBLANES = 8
VMEM_LIMIT_BYTES = 56 * 1024 * 1024

MLA_HEAD_W = 2 * LANES
Q_OFF = 0
KV_OFF = Q_OFF + MLA_Q_LORA
KR_OFF = KV_OFF + MLA_KV_LORA
KR_W = MLA_HEAD_W
MLA_IN_W = KR_OFF + KR_W
NA_W = NA_HEADS * NA_HEAD_DIM
NAQ_OFF = MLA_IN_W
NAK_OFF = NAQ_OFF + NA_W
NAV_OFF = NAK_OFF + NA_W
LRU_OFF = NAV_OFF + NA_W
GP_OFF = LRU_OFF + LRU_WIDTH
MG_OFF = GP_OFF + N_BRANCH * BRANCH_W
NEG_INF = -1e30

_Q = MLA_ROPE // 4
ROPE_SWAP = np.concatenate([np.arange(_Q, 2 * _Q), np.arange(0, _Q),
                            np.arange(3 * _Q, 4 * _Q), np.arange(2 * _Q, 3 * _Q)])

LOG2_E = float(np.log2(np.e))
MLA_LOG2_SCALE = float((MLA_NOPE + MLA_ROPE) ** -0.5 * LOG2_E)
NA_LOG2_SCALE = float(NA_HEAD_DIM ** -0.5 * LOG2_E)
MLA_HEADS_PER_STEP = 4
MLA_KEY_CHUNK = 512
NA_HEADS_PER_STEP = 8
NA_Q_ROWS = 4
NA_KEY_ROWS = NA_Q_ROWS + NA_WIN_ROWS - 1
LRU_CHUNK = 256
LRU_COLS = 512
LRU_GROUP_UNROLL = 8


def _cparams(*sem):
    return pltpu.CompilerParams(dimension_semantics=sem, vmem_limit_bytes=VMEM_LIMIT_BYTES)


def _dot(a, b):
    return jnp.dot(a, b, preferred_element_type=jnp.float32)


def _dot_nt(a, b):
    return lax.dot_general(a, b, (((1,), (1,)), ((), ())), preferred_element_type=jnp.float32)


def _rms(x, g):
    return x * lax.rsqrt(jnp.mean(x * x, axis=-1, keepdims=True) + EPS) * g


def _row_tile(t, target):
    best = None
    for cand in range(2 * SUBLANES, min(t, target) + 1, 2 * SUBLANES):
        if t % cand == 0:
            best = cand
    assert best is not None
    return best


def _ada_kernel(c_ref, w_ref, b_ref, o_ref):
    c = c_ref[...]
    s = (c * jax.nn.sigmoid(c)).astype(jnp.bfloat16)
    o_ref[0] = _dot(s, w_ref[0].astype(jnp.bfloat16)) + b_ref[0]


def _ada_mod(cond, ada_w, ada_b):
    depth, d, n = ada_w.shape
    rows = cond.shape[0]
    tn = 1024
    return pl.pallas_call(
        _ada_kernel,
        grid=(depth, n // tn),
        in_specs=[pl.BlockSpec((rows, d), lambda l, j: (0, 0)),
                  pl.BlockSpec((1, d, tn), lambda l, j: (l, 0, j)),
                  pl.BlockSpec((1, 1, tn), lambda l, j: (l, 0, j))],
        out_specs=pl.BlockSpec((1, rows, tn), lambda l, j: (l, 0, j)),
        out_shape=jax.ShapeDtypeStruct((depth, rows, n), jnp.float32),
        compiler_params=_cparams("parallel", "parallel"),
        name="ada_mod",
    )(cond, ada_w, ada_b.reshape(depth, 1, n))


def _inproj_kernel(x_ref, g_ref, scl_ref, shl_ref, scc_ref, shc_ref, wa_ref, wb_ref, o_ref, h_ref,
                   *, n_ctx):
    i = pl.program_id(1)
    j = pl.program_id(2)
    tm = x_ref.shape[1]

    @pl.when(j == 0)
    def _():
        y = _rms(x_ref[0], g_ref[...])
        row = i * tm + lax.broadcasted_iota(jnp.int32, (tm, 1), 0)
        is_ctx = row < n_ctx
        sc = jnp.where(is_ctx, scc_ref[...], scl_ref[0])
        sh = jnp.where(is_ctx, shc_ref[...], shl_ref[0])
        h_ref[...] = (y * (1.0 + sc) + sh).astype(h_ref.dtype)
        o_ref[0] = _dot_nt(h_ref[...], wa_ref[...]).astype(o_ref.dtype)

    @pl.when(j > 0)
    def _():
        o_ref[0] = _dot_nt(h_ref[...], wb_ref[...]).astype(o_ref.dtype)


def _inproj(xa, g, sc_l, sh_l, sc_c, sh_c, w_mla, w_rest, layer, n_ctx):
    b, t, d = xa.shape
    tn = w_mla.shape[1]
    n_rest = w_rest.shape[1]
    assert n_rest % tn == 0
    tm = _row_tile(t, 1088)
    vec = pl.BlockSpec((1, d), lambda bi, i, j: (0, 0))
    per_b = pl.BlockSpec((1, 1, d), lambda bi, i, j: (bi, 0, 0))
    return pl.pallas_call(
        functools.partial(_inproj_kernel, n_ctx=n_ctx),
        grid=(b, t // tm, 1 + n_rest // tn),
        in_specs=[pl.BlockSpec((1, tm, d), lambda bi, i, j: (bi, i, 0)),
                  vec, per_b, per_b, vec, vec,
                  pl.BlockSpec((None, tn, d), lambda bi, i, j: (layer, 0, 0)),
                  pl.BlockSpec((None, tn, d), lambda bi, i, j: (layer, jnp.maximum(j - 1, 0), 0))],
        out_specs=pl.BlockSpec((1, tm, tn), lambda bi, i, j: (bi, i, j)),
        out_shape=jax.ShapeDtypeStruct((b, t, tn + n_rest), jnp.bfloat16),
        scratch_shapes=[pltpu.VMEM((tm, d), jnp.bfloat16)],
        compiler_params=_cparams("parallel", "parallel", "arbitrary"),
        name="inproj",
    )(xa, g, sc_l, sh_l, sc_c, sh_c, w_mla, w_rest)


def _mla_prep_kernel(p_ref, gq_ref, gkv_ref, wq_ref, wkv_ref, cos_ref, sin_ref,
                     q_ref, k_ref, v_ref):
    p = p_ref[0].astype(jnp.float32)
    cos = cos_ref[...]
    sin = sin_ref[...]
    shift = MLA_HEAD_W - MLA_ROPE

    def rope(a):
        return a * cos + pltpu.roll(a, shift, 1) * sin

    hq = _rms(p[:, Q_OFF:Q_OFF + MLA_Q_LORA], gq_ref[...]).astype(jnp.bfloat16)
    hkv = _rms(p[:, KV_OFF:KV_OFF + MLA_KV_LORA], gkv_ref[...]).astype(jnp.bfloat16)
    k_rope = rope(p[:, KR_OFF:KR_OFF + KR_W])
    q = _dot(hq, wq_ref[...])
    kv = _dot(hkv, wkv_ref[...])
    k_w = MLA_HEADS * MLA_HEAD_W
    ones = jnp.ones((p.shape[0], MLA_HEAD_W - MLA_V), v_ref.dtype)
    for h in range(MLA_HEADS):
        sl = slice(h * MLA_HEAD_W, (h + 1) * MLA_HEAD_W)
        q_ref[0, :, sl] = (rope(q[:, sl]) * MLA_LOG2_SCALE).astype(q_ref.dtype)
        k_ref[0, :, sl] = (kv[:, sl] + k_rope).astype(k_ref.dtype)
        v_ref[0, :, h * MLA_HEAD_W:h * MLA_HEAD_W + MLA_V] = (
            kv[:, k_w + h * MLA_V:k_w + (h + 1) * MLA_V].astype(v_ref.dtype))
        v_ref[0, :, h * MLA_HEAD_W + MLA_V:(h + 1) * MLA_HEAD_W] = ones


def _mla_prep(p, gq, gkv, wq, wkv, cos_t, sin_t):
    b, t, _ = p.shape
    tm = _row_tile(t, 272)
    qk_w = MLA_HEADS * MLA_HEAD_W
    v_w = MLA_HEADS * MLA_HEAD_W
    full = lambda a: pl.BlockSpec(a.shape, lambda bi, i: (0,) * a.ndim)
    return pl.pallas_call(
        _mla_prep_kernel,
        grid=(b, t // tm),
        in_specs=[pl.BlockSpec((1, tm, MLA_IN_W), lambda bi, i: (bi, i, 0)),
                  full(gq), full(gkv), full(wq), full(wkv),
                  pl.BlockSpec((tm, MLA_HEAD_W), lambda bi, i: (i, 0)),
                  pl.BlockSpec((tm, MLA_HEAD_W), lambda bi, i: (i, 0))],
        out_specs=[pl.BlockSpec((1, tm, qk_w), lambda bi, i: (bi, i, 0)),
                   pl.BlockSpec((1, tm, qk_w), lambda bi, i: (bi, i, 0)),
                   pl.BlockSpec((1, tm, v_w), lambda bi, i: (bi, i, 0))],
        out_shape=[jax.ShapeDtypeStruct((b, t, qk_w), jnp.bfloat16),
                   jax.ShapeDtypeStruct((b, t, qk_w), jnp.bfloat16),
                   jax.ShapeDtypeStruct((b, t, v_w), jnp.bfloat16)],
        compiler_params=_cparams("parallel", "parallel"),
        name="mla_prep",
    )(p, gq, gkv, wq, wkv, cos_t, sin_t)


def _mla_attend(q_ref, k_ref, v_ref, o_ref, nk):
    hw = MLA_HEAD_W
    outs = []
    for hh in range(MLA_HEADS_PER_STEP):
        lanes = slice(hh * hw, (hh + 1) * hw)
        q = q_ref[0, :, lanes]
        m = acc = None
        for j0 in range(0, nk, MLA_KEY_CHUNK):
            j1 = min(j0 + MLA_KEY_CHUNK, nk)
            s = _dot_nt(q, k_ref[0, j0:j1, lanes])
            m_j = jnp.max(s, axis=-1, keepdims=True)
            m_new = m_j if m is None else jnp.maximum(m, m_j)
            e = jnp.exp2(s - m_new).astype(jnp.bfloat16)
            pv = _dot(e, v_ref[0, j0:j1, lanes])
            acc = pv if acc is None else acc * jnp.exp2(m - m_new) + pv
            m = m_new
        outs.append(acc[:, :MLA_V] / acc[:, MLA_V:])
    o_ref[0] = jnp.concatenate(outs, axis=1).astype(o_ref.dtype)


def _mla_attn_kernel(q_ref, k_ref, v_ref, o_ref, *, n_ctx):
    i = pl.program_id(2)
    tq = q_ref.shape[1]
    t = k_ref.shape[1]

    @pl.when(i < n_ctx // tq)
    def _():
        _mla_attend(q_ref, k_ref, v_ref, o_ref, n_ctx)

    @pl.when(i >= n_ctx // tq)
    def _():
        _mla_attend(q_ref, k_ref, v_ref, o_ref, t)


def _mla_attn(q, k, v, n_ctx):
    b, t, _ = q.shape
    tq = 2 * LANES
    hps = MLA_HEADS_PER_STEP
    assert n_ctx % tq == 0 and t % tq == 0 and MLA_HEADS % hps == 0 and MLA_HEAD_W == 2 * MLA_V
    return pl.pallas_call(
        functools.partial(_mla_attn_kernel, n_ctx=n_ctx),
        grid=(b, MLA_HEADS // hps, t // tq),
        in_specs=[pl.BlockSpec((1, tq, hps * MLA_HEAD_W), lambda bi, h, i: (bi, i, h)),
                  pl.BlockSpec((1, t, hps * MLA_HEAD_W), lambda bi, h, i: (bi, 0, h)),
                  pl.BlockSpec((1, t, hps * MLA_HEAD_W), lambda bi, h, i: (bi, 0, h))],
        out_specs=pl.BlockSpec((1, tq, hps * MLA_V), lambda bi, h, i: (bi, i, h)),
        out_shape=jax.ShapeDtypeStruct((b, t, MLA_HEADS * MLA_V), jnp.bfloat16),
        compiler_params=_cparams("parallel", "parallel", "arbitrary"),
        name="mla_attn",
    )(q, k, v)


def _na_plan(rows):
    kr = min(NA_WIN_ROWS, rows)
    assert rows % NA_Q_ROWS == 0 and rows >= NA_KEY_ROWS
    starts, patterns, var_of_block = [], [], []
    for r in range(rows // NA_Q_ROWS):
        start = int(np.clip(NA_Q_ROWS * r - kr // 2, 0, rows - NA_KEY_ROWS))
        valid = np.zeros((NA_Q_ROWS, NA_KEY_ROWS), bool)
        dr0 = np.zeros((NA_Q_ROWS,), np.int64)
        for a in range(NA_Q_ROWS):
            i = NA_Q_ROWS * r + a
            r0 = int(np.clip(i - kr // 2, 0, rows - kr))
            assert start <= r0 and r0 + kr <= start + NA_KEY_ROWS
            krow = start + np.arange(NA_KEY_ROWS)
            valid[a] = (r0 <= krow) & (krow < r0 + kr)
            dr0[a] = start - i + (NA_WIN_ROWS - 1)
        key = (valid.tobytes(), dr0.tobytes())
        for v, (pk, _, _) in enumerate(patterns):
            if pk == key:
                var_of_block.append(v)
                break
        else:
            var_of_block.append(len(patterns))
            patterns.append((key, valid, dr0))
        starts.append(start)
    valid_r = np.stack([p[1] for p in patterns])
    dr0 = np.stack([p[2] for p in patterns])
    return np.array(starts), np.array(var_of_block), valid_r, dr0


def _na_bias_tables(rel_bias, rows):
    _, _, valid_r, dr0 = _na_plan(rows)
    depth, heads, n_dr, n_dc = rel_bias.shape
    cols = np.arange(GRID_W)
    col_start = np.clip(cols - NA_WIN_COLS // 2, 0, GRID_W - NA_WIN_COLS)
    valid_c = (cols[None, :] >= col_start[:, None]) & (cols[None, :] < col_start[:, None] + NA_WIN_COLS)
    dc = cols[None, :] - cols[:, None] + (NA_WIN_COLS - 1)
    onehot = (valid_c[:, :, None] & (dc[:, :, None] == np.arange(n_dc))).astype(np.float32)
    e1 = jnp.einsum('lhrd,jcd->lhjrc', rel_bias, jnp.asarray(onehot), precision=lax.Precision.HIGHEST)
    e1 = jnp.where(jnp.asarray(valid_c)[None, None, :, None, :], e1 * LOG2_E, NEG_INF)
    flat = e1.reshape(depth, heads, GRID_W, n_dr * GRID_W)
    n_win = NA_KEY_ROWS * GRID_W
    pad = jnp.full((depth, heads, GRID_W, n_win), NEG_INF, flat.dtype)
    flat = jnp.concatenate([pad, flat, pad], axis=-1)
    assert dr0.min() > -NA_KEY_ROWS and dr0.max() < n_dr
    blocks = []
    for v in range(valid_r.shape[0]):
        for a in range(NA_Q_ROWS):
            off = n_win + int(dr0[v, a]) * GRID_W
            keep = np.repeat(valid_r[v, a], GRID_W)
            blocks.append(jnp.where(jnp.asarray(keep), flat[..., off:off + n_win], NEG_INF))
    bm = jnp.stack(blocks, axis=2)
    return bm.reshape(depth, heads, valid_r.shape[0], NA_Q_ROWS * GRID_W, n_win)


def _na_attn_kernel(var_ref, ks_ref, q_ref, k_ref, v_ref, bm_ref, o_ref, vx_ref, *, n_ctx):
    del var_ref
    i = pl.program_id(2)
    tq = q_ref.shape[1]
    n_win = bm_ref.shape[-1]
    per_group = LANES // NA_HEAD_DIM
    n_group = NA_HEADS_PER_STEP // per_group
    low = lax.broadcasted_iota(jnp.int32, (1, LANES), 1) < NA_HEAD_DIM
    zero = jnp.zeros((), q_ref.dtype)

    @pl.when(i == 0)
    def _():
        for g in range(n_group):
            vx_ref[g, :, :LANES] = v_ref[0, :, g * LANES:(g + 1) * LANES]
            vx_ref[g, :, LANES:] = jnp.ones((vx_ref.shape[1], LANES), vx_ref.dtype)

    def softmax_pv(parts):
        m = None
        for s, _ in parts:
            m_s = jnp.max(s, axis=-1, keepdims=True)
            m = m_s if m is None else jnp.maximum(m, m_s)
        pv = None
        for s, vals in parts:
            pv_s = _dot(jnp.exp2(s - m).astype(jnp.bfloat16), vals)
            pv = pv_s if pv is None else pv + pv_s
        return pv[:, :LANES] / pv[:, LANES:]

    def run(window):
        if window:
            ks = pl.multiple_of(ks_ref[i], GRID_W)
        for g in range(n_group):
            lanes = slice(g * LANES, (g + 1) * LANES)
            q = q_ref[0, :, lanes]
            k_ctx = k_ref[0, :n_ctx, lanes]
            outs = []
            for hh in range(per_group):
                qh = jnp.where(low if hh == 0 else jnp.logical_not(low), q, zero)
                parts = [(_dot_nt(qh, k_ctx), vx_ref[g, :n_ctx, :])]
                if window:
                    s_w = _dot_nt(qh, k_ref[0, pl.ds(ks, n_win), lanes]) + bm_ref[g * per_group + hh, 0]
                    parts.append((s_w, vx_ref[g, pl.ds(ks, n_win), :]))
                outs.append(softmax_pv(parts))
            o_ref[0, :, lanes] = jnp.where(low, outs[0], outs[1]).astype(o_ref.dtype)

    @pl.when(i < n_ctx // tq)
    def _():
        run(False)

    @pl.when(i >= n_ctx // tq)
    def _():
        run(True)


def _na_attn(p, bm, layer, n_ctx):
    b, t, _ = p.shape
    tq = NA_Q_ROWS * GRID_W
    hps = NA_HEADS_PER_STEP
    gw = hps * NA_HEAD_DIM
    assert n_ctx % tq == 0 and (t - n_ctx) % tq == 0 and LANES == 2 * NA_HEAD_DIM
    rows = (t - n_ctx) // GRID_W
    starts, var_of_block, _, _ = _na_plan(rows)
    n_cb = n_ctx // tq
    var_tab = jnp.asarray(np.concatenate([np.zeros(n_cb, np.int32), var_of_block]).astype(np.int32))
    ks_tab = jnp.asarray(np.concatenate([np.zeros(n_cb, np.int32), n_ctx + starts * GRID_W]).astype(np.int32))
    n_win = NA_KEY_ROWS * GRID_W
    qc, kc, vc = NAQ_OFF // gw, NAK_OFF // gw, NAV_OFF // gw
    grid_spec = pltpu.PrefetchScalarGridSpec(
        num_scalar_prefetch=2,
        grid=(b, NA_HEADS // hps, t // tq),
        in_specs=[pl.BlockSpec((1, tq, gw), lambda bi, h, i, var, ks: (bi, i, qc + h)),
                  pl.BlockSpec((1, t, gw), lambda bi, h, i, var, ks: (bi, 0, kc + h)),
                  pl.BlockSpec((1, t, gw), lambda bi, h, i, var, ks: (bi, 0, vc + h)),
                  pl.BlockSpec((None, hps, 1, tq, n_win),
                               lambda bi, h, i, var, ks: (layer, h, var[i], 0, 0))],
        out_specs=pl.BlockSpec((1, tq, gw), lambda bi, h, i, var, ks: (bi, i, h)),
        scratch_shapes=[pltpu.VMEM((gw // LANES, t, 2 * LANES), jnp.bfloat16)],
    )
    return pl.pallas_call(
        functools.partial(_na_attn_kernel, n_ctx=n_ctx),
        grid_spec=grid_spec,
        out_shape=jax.ShapeDtypeStruct((b, t, NA_W), jnp.bfloat16),
        compiler_params=_cparams("parallel", "parallel", "arbitrary"),
        name="na_attn",
    )(var_tab, ks_tab, p, p, p, bm)


def _lru_kernel(x_ref, cw_ref, cb_ref, wg_ref, bg_ref, lam_ref, o_ref, xc_ref, a_ref, u_ref, *, n_ctx):
    t, wc = xc_ref.shape
    r = LRU_CHUNK
    n_chunk = t // r
    n_cc = n_ctx // r
    left = CONV_WIDTH // 2
    row = lax.broadcasted_iota(jnp.int32, (r, wc), 0)

    def conv_chunk(c, carry):
        t0 = pl.multiple_of(c * r, r)
        first = jnp.logical_or(c == 0, c == n_cc)
        last = jnp.logical_or(c == n_cc - 1, c == n_chunk - 1)
        halo = 2 * SUBLANES
        main = x_ref[0, pl.ds(t0, r), :].astype(jnp.float32)
        prev = x_ref[0, pl.ds(pl.multiple_of(jnp.maximum(t0 - halo, 0), halo), halo), :]
        nxt = x_ref[0, pl.ds(pl.multiple_of(jnp.minimum(t0 + r, t - halo), halo), halo), :]
        prev = jnp.where(first, 0.0, prev.astype(jnp.float32))
        nxt = jnp.where(last, 0.0, nxt.astype(jnp.float32))
        y = cb_ref[...] + jnp.zeros((r, wc), jnp.float32)
        for tap in range(CONV_WIDTH):
            off = tap - left
            if off == 0:
                xs = main
            elif off < 0:
                xs = pltpu.roll(main, -off, 0)
                for k in range(-off):
                    xs = jnp.where(row == k, prev[halo + off + k:halo + off + k + 1, :], xs)
            else:
                xs = pltpu.roll(main, r - off, 0)
                for k in range(off):
                    xs = jnp.where(row == r - off + k, nxt[k:k + 1, :], xs)
            y = y + xs * cw_ref[tap:tap + 1, :]
        xc_ref[pl.ds(t0, r), :] = y
        return carry

    lax.fori_loop(0, n_chunk, conv_chunk, 0)

    grow = lax.broadcasted_iota(jnp.int32, (SUBLANES, wc), 0)
    n_grp = r // SUBLANES

    def scan_direction(direction):
        reverse = direction == 1
        lam = lam_ref[direction]
        neg = -lam
        softplus = jnp.maximum(neg, 0.0) + jnp.log1p(jnp.exp(-jnp.abs(neg)))
        half_decay = (-0.5 * LRU_C * LOG2_E) * softplus

        def chunk_step(step, h):
            if reverse:
                c = jnp.where(step < n_cc, n_cc - 1 - step, n_chunk - 1 - (step - n_cc))
            else:
                c = step
            t0 = pl.multiple_of(c * r, r)
            xc = xc_ref[pl.ds(t0, r), :]
            t = jnp.tanh(_dot(xc.astype(jnp.bfloat16), wg_ref[direction, 0]) + bg_ref[direction, 0])
            a = jnp.exp2(half_decay + half_decay * t[:, :wc])
            a_ref[...] = a
            v = jnp.maximum(1.0 - a * a, 0.0)
            xh = 0.5 * xc
            u_ref[...] = jnp.where(v > 0.0, v * lax.rsqrt(v), 0.0) * (xh + xh * t[:, wc:])

            def group_step(gi, h):
                g0 = (n_grp - 1 - gi) if reverse else gi
                off = pl.multiple_of(g0 * SUBLANES, SUBLANES)
                a = a_ref[pl.ds(off, SUBLANES), :]
                u = u_ref[pl.ds(off, SUBLANES), :]
                for s in (1, 2, 4):
                    if reverse:
                        keep = grow < SUBLANES - s
                        a_s = pltpu.roll(a, SUBLANES - s, 0)
                        u_s = pltpu.roll(u, SUBLANES - s, 0)
                    else:
                        keep = grow >= s
                        a_s = pltpu.roll(a, s, 0)
                        u_s = pltpu.roll(u, s, 0)
                    u = jnp.where(keep, a * u_s + u, u)
                    a = jnp.where(keep, a * a_s, a)
                hh = a * h + u
                dst = pl.ds(pl.multiple_of(t0 + off, SUBLANES), SUBLANES)
                if reverse:
                    o_ref[0, dst, :] = o_ref[0, dst, :] + hh
                    return hh[0:1, :]
                o_ref[0, dst, :] = hh
                return hh[SUBLANES - 1:SUBLANES, :]

            return lax.fori_loop(0, n_grp, group_step, h, unroll=LRU_GROUP_UNROLL)

        lax.fori_loop(0, n_chunk, chunk_step, jnp.zeros((1, wc), jnp.float32))

    scan_direction(0)
    scan_direction(1)


def _lru(p, conv_w, conv_b, wg, bg, lam, n_ctx):
    b, t, _ = p.shape
    wc = LRU_COLS
    assert t % LRU_CHUNK == 0 and n_ctx % LRU_CHUNK == 0 and n_ctx > 0 and t > n_ctx
    xcol = LRU_OFF // wc
    return pl.pallas_call(
        functools.partial(_lru_kernel, n_ctx=n_ctx),
        grid=(b, LRU_WIDTH // wc),
        in_specs=[pl.BlockSpec((1, t, wc), lambda bi, c: (bi, 0, xcol + c)),
                  pl.BlockSpec((CONV_WIDTH, wc), lambda bi, c: (0, c)),
                  pl.BlockSpec((1, wc), lambda bi, c: (0, c)),
                  pl.BlockSpec((2, 1, wc, 2 * wc), lambda bi, c: (0, c, 0, 0)),
                  pl.BlockSpec((2, 1, 1, 2 * wc), lambda bi, c: (0, c, 0, 0)),
                  pl.BlockSpec((2, 1, wc), lambda bi, c: (0, 0, c))],
        out_specs=pl.BlockSpec((1, t, wc), lambda bi, c: (bi, 0, c)),
        out_shape=jax.ShapeDtypeStruct((b, t, LRU_WIDTH), jnp.float32),
        scratch_shapes=[pltpu.VMEM((t, wc), jnp.float32),
                        pltpu.VMEM((LRU_CHUNK, wc), jnp.float32),
                        pltpu.VMEM((LRU_CHUNK, wc), jnp.float32)],
        compiler_params=_cparams("parallel", "parallel"),
        name="rglru",
    )(p, conv_w, conv_b, wg, bg, lam)


def _merge_kernel(oa_ref, ob_ref, oc_ref, gpa_ref, gpb_ref, gpc_ref, mga_ref, mgb_ref, mgc_ref,
                  wb_ref, z_ref):
    z = None
    for n, (o_ref, gp_ref, mg_ref) in enumerate(((oa_ref, gpa_ref, mga_ref),
                                                 (ob_ref, gpb_ref, mgb_ref),
                                                 (oc_ref, gpc_ref, mgc_ref))):
        gp = gp_ref[0].astype(jnp.float32)
        tkn = (o_ref[0].astype(jnp.float32) * (gp * jax.nn.sigmoid(gp))).astype(jnp.bfloat16)
        y = jax.nn.sigmoid(mg_ref[0].astype(jnp.float32)) * _dot(tkn, wb_ref[n])
        z = y if z is None else z + y
    z_ref[0] = z.astype(z_ref.dtype)


def _merge(oa, ob, oc, p, wb, layer):
    b, t, _ = oa.shape
    d = wb.shape[-1]
    tm = _row_tile(t, 544)
    gpc, mgc = GP_OFF // BRANCH_W, MG_OFF // d
    assert GP_OFF % BRANCH_W == 0 and MG_OFF % d == 0
    o_spec = pl.BlockSpec((1, tm, BRANCH_W), lambda bi, i: (bi, i, 0))
    gp_specs = [pl.BlockSpec((1, tm, BRANCH_W), functools.partial(lambda bi, i, n: (bi, i, gpc + n), n=n))
                for n in range(N_BRANCH)]
    mg_specs = [pl.BlockSpec((1, tm, d), functools.partial(lambda bi, i, n: (bi, i, mgc + n), n=n))
                for n in range(N_BRANCH)]
    return pl.pallas_call(
        _merge_kernel,
        grid=(b, t // tm),
        in_specs=[o_spec, o_spec, o_spec, *gp_specs, *mg_specs,
                  pl.BlockSpec((None,) + wb.shape[1:], lambda bi, i: (layer, 0, 0, 0),
                               pipeline_mode=pl.Buffered(1))],
        out_specs=pl.BlockSpec((1, tm, d), lambda bi, i: (bi, i, 0)),
        out_shape=jax.ShapeDtypeStruct((b, t, d), jnp.bfloat16),
        compiler_params=_cparams("parallel", "parallel"),
        name="merge",
    )(oa, ob, oc, p, p, p, p, p, p, wb)


def _outproj_kernel(z_ref, w_ref, x_ref, gl_ref, gc_ref, o_ref, *, n_ctx):
    i = pl.program_id(1)
    tm = x_ref.shape[1]
    row = i * tm + lax.broadcasted_iota(jnp.int32, (tm, 1), 0)
    gate = jnp.where(row < n_ctx, gc_ref[...], gl_ref[0])
    o_ref[0] = x_ref[0] + gate * _dot(z_ref[0], w_ref[...])


def _outproj(z, w, xa, gt_l, gt_c, layer, n_ctx):
    b, t, d = xa.shape
    tm = _row_tile(t, 544)
    return pl.pallas_call(
        functools.partial(_outproj_kernel, n_ctx=n_ctx),
        grid=(b, t // tm),
        in_specs=[pl.BlockSpec((1, tm, d), lambda bi, i: (bi, i, 0)),
                  pl.BlockSpec((None, d, d), lambda bi, i: (layer, 0, 0), pipeline_mode=pl.Buffered(1)),
                  pl.BlockSpec((1, tm, d), lambda bi, i: (bi, i, 0)),
                  pl.BlockSpec((1, 1, d), lambda bi, i: (bi, 0, 0)),
                  pl.BlockSpec((1, d), lambda bi, i: (0, 0))],
        out_specs=pl.BlockSpec((1, tm, d), lambda bi, i: (bi, i, 0)),
        out_shape=jax.ShapeDtypeStruct((b, t, d), jnp.float32),
        compiler_params=_cparams("parallel", "parallel"),
        name="outproj",
    )(z, w, xa, gt_l, gt_c)


def _outproj_final_kernel(z_ref, w_ref, x_ref, gl_ref, g_ref, o_ref):
    o_ref[0] = _rms(x_ref[0] + gl_ref[0] * _dot(z_ref[0], w_ref[...]), g_ref[...])


def _outproj_final(z, w, xa, gt_l, g, layer, n_ctx):
    b, t, d = xa.shape
    s = t - n_ctx
    tm = _row_tile(int(np.gcd(s, n_ctx)), 512)
    skip = n_ctx // tm
    return pl.pallas_call(
        _outproj_final_kernel,
        grid=(b, s // tm),
        in_specs=[pl.BlockSpec((1, tm, d), lambda bi, i: (bi, i + skip, 0)),
                  pl.BlockSpec((None, d, d), lambda bi, i: (layer, 0, 0), pipeline_mode=pl.Buffered(1)),
                  pl.BlockSpec((1, tm, d), lambda bi, i: (bi, i + skip, 0)),
                  pl.BlockSpec((1, 1, d), lambda bi, i: (bi, 0, 0)),
                  pl.BlockSpec((1, d), lambda bi, i: (0, 0))],
        out_specs=pl.BlockSpec((1, tm, d), lambda bi, i: (bi, i, 0)),
        out_shape=jax.ShapeDtypeStruct((b, s, d), jnp.float32),
        compiler_params=_cparams("parallel", "parallel"),
        name="outproj_final",
    )(z, w, xa, gt_l, g)


def _cast_rows_kernel(x_ref, o_ref, *, n_scaled, scale):
    tr = x_ref.shape[0]
    row = pl.program_id(1) * tr + lax.broadcasted_iota(jnp.int32, (tr, 1), 0)
    x = x_ref[...]
    o_ref[...] = jnp.where(row < n_scaled, x * scale, x).astype(o_ref.dtype)


def _cast_rows(w_t, start, n_scaled, scale):
    depth, n, d = w_t.shape
    tr = start
    assert n % tr == 0 and tr % (2 * SUBLANES) == 0
    return pl.pallas_call(
        functools.partial(_cast_rows_kernel, n_scaled=n_scaled, scale=scale),
        grid=(depth, n // tr - 1),
        in_specs=[pl.BlockSpec((None, tr, d), lambda l, j: (l, j + 1, 0))],
        out_specs=pl.BlockSpec((None, tr, d), lambda l, j: (l, j, 0)),
        out_shape=jax.ShapeDtypeStruct((depth, n - start, d), jnp.bfloat16),
        compiler_params=_cparams("parallel", "parallel"),
        name="w_in_cast",
    )(w_t)


def _prep_w_in(w_in):
    w_t = jnp.swapaxes(w_in, 1, 2)
    kr0 = KV_OFF + MLA_KV_LORA
    kr1 = kr0 + MLA_ROPE
    w_kr = w_t[:, kr0:kr1]
    pad = jnp.zeros((w_t.shape[0], KR_W - 2 * MLA_ROPE, w_t.shape[2]), w_t.dtype)
    w_mla = jnp.concatenate([w_t[:, :kr0], pad, w_kr, w_kr[:, ROPE_SWAP]], axis=1)
    assert NAQ_OFF == MLA_IN_W
    return w_mla.astype(jnp.bfloat16), _cast_rows(w_t, kr1, NA_W, NA_LOG2_SCALE)


def _prep_w_uq(w_uq):
    depth, r, _ = w_uq.shape
    w = w_uq.reshape(depth, r, MLA_HEADS, MLA_NOPE + MLA_ROPE)
    rope = w[..., MLA_NOPE:]
    w = jnp.concatenate([w[..., :MLA_NOPE], rope, rope[..., ROPE_SWAP]], axis=-1)
    return w.reshape(depth, r, MLA_HEADS * MLA_HEAD_W).astype(jnp.bfloat16)


def _prep_w_ukv(w_ukv):
    depth, r, _ = w_ukv.shape
    w = w_ukv.reshape(depth, r, MLA_HEADS, MLA_NOPE + MLA_V)
    k = jnp.concatenate([w[..., :MLA_NOPE], jnp.zeros_like(w[..., :MLA_HEAD_W - MLA_NOPE])], axis=-1)
    k = k.reshape(depth, r, MLA_HEADS * MLA_HEAD_W)
    v = w[..., MLA_NOPE:].reshape(depth, r, MLA_HEADS * MLA_V)
    return jnp.concatenate([k, v], axis=-1).astype(jnp.bfloat16)


def _rope_tables(n_ctx, s):
    tok = jnp.arange(s)
    row = (tok // GRID_W).astype(jnp.float32)
    col = (tok % GRID_W).astype(jnp.float32)
    half = MLA_ROPE // 2
    inv = 1.0 / (ROPE_BASE ** (jnp.arange(0, half, 2, dtype=jnp.float32) / half))
    ang_r = row[:, None] * inv
    ang_c = col[:, None] * inv
    cos = jnp.concatenate([jnp.cos(ang_r), jnp.cos(ang_r), jnp.cos(ang_c), jnp.cos(ang_c)], axis=-1)
    sin = jnp.concatenate([-jnp.sin(ang_r), jnp.sin(ang_r), -jnp.sin(ang_c), jnp.sin(ang_c)], axis=-1)
    cos = jnp.concatenate([jnp.ones((n_ctx, MLA_ROPE), jnp.float32), cos], axis=0)
    sin = jnp.concatenate([jnp.zeros((n_ctx, MLA_ROPE), jnp.float32), sin], axis=0)
    t = n_ctx + s
    tail = MLA_HEAD_W - MLA_NOPE - MLA_ROPE
    cos_t = jnp.concatenate([jnp.ones((t, MLA_NOPE), jnp.float32), cos, jnp.zeros((t, tail), jnp.float32)], axis=-1)
    sin_t = jnp.concatenate([jnp.zeros((t, MLA_NOPE), jnp.float32), sin, jnp.zeros((t, tail), jnp.float32)], axis=-1)
    return cos_t, sin_t


def _prep_lru_gates(w_gate, b_gate):
    depth = w_gate.shape[0]
    per = LRU_COLS // LRU_BLOCK
    n_col = LRU_BLOCKS // per
    eye = jnp.eye(per, dtype=w_gate.dtype)

    def block_diag(w):
        w = w.reshape(depth, 2, n_col, per, LRU_BLOCK, LRU_BLOCK)
        w = w[:, :, :, :, :, None, :] * eye[None, None, None, :, None, :, None]
        return w.reshape(depth, 2, n_col, LRU_COLS, LRU_COLS)

    wg = jnp.concatenate([block_diag(w_gate[..., :LRU_BLOCK]), block_diag(w_gate[..., LRU_BLOCK:])], axis=-1)
    bg = b_gate.reshape(depth, 2, LRU_BLOCKS, 2 * LRU_BLOCK)
    bg = jnp.concatenate([bg[..., :LRU_BLOCK].reshape(depth, 2, n_col, 1, LRU_COLS),
                          bg[..., LRU_BLOCK:].reshape(depth, 2, n_col, 1, LRU_COLS)], axis=-1)
    return (wg * 0.5).astype(jnp.bfloat16), bg * 0.5


def kernel(x, c, ctx, c_ctx, ada_w, ada_b, norm_g, w_in, mla_q_norm_g, mla_kv_norm_g, mla_w_uq, mla_w_ukv, na_rel_bias, lru_conv_w, lru_conv_b, lru_w_gate, lru_b_gate, lru_lambda, w_branch, w_out, final_norm_g):
    b, s, d = x.shape
    n_ctx = ctx.shape[1]
    depth = ada_w.shape[0]
    rows = s // GRID_W

    w_mla_p, w_rest_p = _prep_w_in(w_in)
    wq_p = _prep_w_uq(mla_w_uq)
    wkv_p = _prep_w_ukv(mla_w_ukv)
    cos_t, sin_t = _rope_tables(n_ctx, s)
    bm = _na_bias_tables(na_rel_bias, rows)
    wg_p, bg_p = _prep_lru_gates(lru_w_gate, lru_b_gate)
    wb_p = w_branch.astype(jnp.bfloat16)
    wo_p = w_out.astype(jnp.bfloat16)

    n_cond = -(-(b + 1) // SUBLANES) * SUBLANES
    cond = jnp.concatenate([c, c_ctx[None, :], jnp.zeros((n_cond - b - 1, d), c.dtype)], axis=0)
    mod = _ada_mod(cond, ada_w, ada_b)

    xa = jnp.concatenate([ctx, x], axis=1)
    for l in range(depth):
        sh_l, sc_l, gt_l = (mod[l, :b, k * d:(k + 1) * d][:, None, :] for k in range(3))
        sh_c, sc_c, gt_c = (mod[l, b:b + 1, k * d:(k + 1) * d] for k in range(3))
        p = _inproj(xa, norm_g[l][None, :], sc_l, sh_l, sc_c, sh_c, w_mla_p, w_rest_p, l, n_ctx)
        q, k, v = _mla_prep(p, mla_q_norm_g[l][None, :], mla_kv_norm_g[l][None, :],
                            wq_p[l], wkv_p[l], cos_t, sin_t)
        o_a = _mla_attn(q, k, v, n_ctx)
        o_c = _lru(p, lru_conv_w[l], lru_conv_b[l][None, :], wg_p[l], bg_p[l],
                   lru_lambda[l][:, None, :], n_ctx)
        o_b = _na_attn(p, bm, l, n_ctx)
        z = _merge(o_a, o_b, o_c, p, wb_p, l)
        if l < depth - 1:
            xa = _outproj(z, wo_p, xa, gt_l, gt_c, l, n_ctx)
    return _outproj_final(z, wo_p, xa, gt_l, final_norm_g[None, :], depth - 1, n_ctx)
```

```python
import functools

import numpy as np
import jax
import jax.numpy as jnp
from jax import lax
from jax.experimental import pallas as pl
from jax.experimental.pallas import tpu as pltpu

GRID_W = 64
MLA_HEADS = 8
MLA_Q_LORA = 512
MLA_KV_LORA = 256
MLA_NOPE = 128
MLA_ROPE = 64
MLA_V = 128
NA_HEADS = 16
NA_HEAD_DIM = 64
NA_WIN_ROWS = 8
NA_WIN_COLS = 16
LRU_WIDTH = 1024
LRU_BLOCKS = 16
LRU_BLOCK = LRU_WIDTH // LRU_BLOCKS
LRU_C = 8.0
CONV_WIDTH = 4
N_BRANCH = 3
BRANCH_W = 1024
ROPE_BASE = 10000.0
EPS = 1e-6

LANES = 128
SUBLANES = 8
VMEM_LIMIT_BYTES = 56 * 1024 * 1024

MLA_HEAD_W = 2 * LANES
Q_OFF = 0
KV_OFF = Q_OFF + MLA_Q_LORA
KR_OFF = KV_OFF + MLA_KV_LORA
KR_W = MLA_HEAD_W
MLA_IN_W = KR_OFF + KR_W
NA_W = NA_HEADS * NA_HEAD_DIM
NAQ_OFF = MLA_IN_W
NAK_OFF = NAQ_OFF + NA_W
NAV_OFF = NAK_OFF + NA_W
LRU_OFF = NAV_OFF + NA_W
GP_OFF = LRU_OFF + LRU_WIDTH
MG_OFF = GP_OFF + N_BRANCH * BRANCH_W
NEG_INF = -1e30

_Q = MLA_ROPE // 4
ROPE_SWAP = np.concatenate([np.arange(_Q, 2 * _Q), np.arange(0, _Q),
                            np.arange(3 * _Q, 4 * _Q), np.arange(2 * _Q, 3 * _Q)])

LOG2_E = float(np.log2(np.e))
MLA_LOG2_SCALE = float((MLA_NOPE + MLA_ROPE) ** -0.5 * LOG2_E)
NA_LOG2_SCALE = float(NA_HEAD_DIM ** -0.5 * LOG2_E)
MLA_HEADS_PER_STEP = 4
MLA_KEY_CHUNK = 512
NA_HEADS_PER_STEP = 8
NA_Q_ROWS = 4
NA_KEY_ROWS = NA_Q_ROWS + NA_WIN_ROWS - 1
LRU_CHUNK = 256
LRU_COLS = 512
LRU_GROUP_UNROLL = 8


def _cparams(*sem):
    return pltpu.CompilerParams(dimension_semantics=sem, vmem_limit_bytes=VMEM_LIMIT_BYTES)


def _dot(a, b):
    return jnp.dot(a, b, preferred_element_type=jnp.float32)


def _dot_nt(a, b):
    return lax.dot_general(a, b, (((1,), (1,)), ((), ())), preferred_element_type=jnp.float32)


def _rms(x, g):
    return x * lax.rsqrt(jnp.mean(x * x, axis=-1, keepdims=True) + EPS) * g


def _row_tile(t, target):
    best = None
    for cand in range(2 * SUBLANES, min(t, target) + 1, 2 * SUBLANES):
        if t % cand == 0:
            best = cand
    assert best is not None
    return best


def _ada_kernel(c_ref, w_ref, b_ref, o_ref):
    c = c_ref[...]
    s = (c * jax.nn.sigmoid(c)).astype(jnp.bfloat16)
    o_ref[0] = _dot(s, w_ref[0].astype(jnp.bfloat16)) + b_ref[0]


def _ada_mod(cond, ada_w, ada_b):
    depth, d, n = ada_w.shape
    rows = cond.shape[0]
    tn = 1024
    return pl.pallas_call(
        _ada_kernel,
        grid=(depth, n // tn),
        in_specs=[pl.BlockSpec((rows, d), lambda l, j: (0, 0)),
                  pl.BlockSpec((1, d, tn), lambda l, j: (l, 0, j)),
                  pl.BlockSpec((1, 1, tn), lambda l, j: (l, 0, j))],
        out_specs=pl.BlockSpec((1, rows, tn), lambda l, j: (l, 0, j)),
        out_shape=jax.ShapeDtypeStruct((depth, rows, n), jnp.float32),
        compiler_params=_cparams("parallel", "parallel"),
        name="ada_mod",
    )(cond, ada_w, ada_b.reshape(depth, 1, n))


def _inproj_kernel(x_ref, g_ref, scl_ref, shl_ref, scc_ref, shc_ref, wa_ref, wb_ref, o_ref, h_ref,
                   *, n_ctx):
    i = pl.program_id(1)
    j = pl.program_id(2)
    tm = x_ref.shape[1]

    @pl.when(j == 0)
    def _():
        y = _rms(x_ref[0], g_ref[...])
        row = i * tm + lax.broadcasted_iota(jnp.int32, (tm, 1), 0)
        is_ctx = row < n_ctx
        sc = jnp.where(is_ctx, scc_ref[...], scl_ref[0])
        sh = jnp.where(is_ctx, shc_ref[...], shl_ref[0])
        h_ref[...] = (y * (1.0 + sc) + sh).astype(h_ref.dtype)
        o_ref[0] = _dot_nt(h_ref[...], wa_ref[...]).astype(o_ref.dtype)

    @pl.when(j > 0)
    def _():
        o_ref[0] = _dot_nt(h_ref[...], wb_ref[...]).astype(o_ref.dtype)


def _inproj(xa, g, sc_l, sh_l, sc_c, sh_c, w_mla, w_rest, layer, n_ctx):
    b, t, d = xa.shape
    tn = w_mla.shape[1]
    n_rest = w_rest.shape[1]
    assert n_rest % tn == 0
    tm = _row_tile(t, 1088)
    vec = pl.BlockSpec((1, d), lambda bi, i, j: (0, 0))
    per_b = pl.BlockSpec((1, 1, d), lambda bi, i, j: (bi, 0, 0))
    return pl.pallas_call(
        functools.partial(_inproj_kernel, n_ctx=n_ctx),
        grid=(b, t // tm, 1 + n_rest // tn),
        in_specs=[pl.BlockSpec((1, tm, d), lambda bi, i, j: (bi, i, 0)),
                  vec, per_b, per_b, vec, vec,
                  pl.BlockSpec((None, tn, d), lambda bi, i, j: (layer, 0, 0)),
                  pl.BlockSpec((None, tn, d), lambda bi, i, j: (layer, jnp.maximum(j - 1, 0), 0))],
        out_specs=pl.BlockSpec((1, tm, tn), lambda bi, i, j: (bi, i, j)),
        out_shape=jax.ShapeDtypeStruct((b, t, tn + n_rest), jnp.bfloat16),
        scratch_shapes=[pltpu.VMEM((tm, d), jnp.bfloat16)],
        compiler_params=_cparams("parallel", "parallel", "arbitrary"),
        name="inproj",
    )(xa, g, sc_l, sh_l, sc_c, sh_c, w_mla, w_rest)


def _mla_prep_kernel(p_ref, gq_ref, gkv_ref, wq_ref, wkv_ref, cos_ref, sin_ref,
                     q_ref, k_ref, v_ref):
    p = p_ref[0].astype(jnp.float32)
    cos = cos_ref[...]
    sin = sin_ref[...]
    shift = MLA_HEAD_W - MLA_ROPE

    def rope(a):
        return a * cos + pltpu.roll(a, shift, 1) * sin

    hq = _rms(p[:, Q_OFF:Q_OFF + MLA_Q_LORA], gq_ref[...]).astype(jnp.bfloat16)
    hkv = _rms(p[:, KV_OFF:KV_OFF + MLA_KV_LORA], gkv_ref[...]).astype(jnp.bfloat16)
    k_rope = rope(p[:, KR_OFF:KR_OFF + KR_W])
    q = _dot(hq, wq_ref[...])
    kv = _dot(hkv, wkv_ref[...])
    k_w = MLA_HEADS * MLA_HEAD_W
    for h in range(MLA_HEADS):
        sl = slice(h * MLA_HEAD_W, (h + 1) * MLA_HEAD_W)
        q_ref[0, :, sl] = (rope(q[:, sl]) * MLA_LOG2_SCALE).astype(q_ref.dtype)
        k_ref[0, :, sl] = (kv[:, sl] + k_rope).astype(k_ref.dtype)
    v_ref[0] = kv[:, k_w:].astype(v_ref.dtype)


def _mla_prep(p, gq, gkv, wq, wkv, cos_t, sin_t):
    b, t, _ = p.shape
    tm = _row_tile(t, 272)
    qk_w = MLA_HEADS * MLA_HEAD_W
    v_w = MLA_HEADS * MLA_V
    full = lambda a: pl.BlockSpec(a.shape, lambda bi, i: (0,) * a.ndim)
    return pl.pallas_call(
        _mla_prep_kernel,
        grid=(b, t // tm),
        in_specs=[pl.BlockSpec((1, tm, MLA_IN_W), lambda bi, i: (bi, i, 0)),
                  full(gq), full(gkv), full(wq), full(wkv),
                  pl.BlockSpec((tm, MLA_HEAD_W), lambda bi, i: (i, 0)),
                  pl.BlockSpec((tm, MLA_HEAD_W), lambda bi, i: (i, 0))],
        out_specs=[pl.BlockSpec((1, tm, qk_w), lambda bi, i: (bi, i, 0)),
                   pl.BlockSpec((1, tm, qk_w), lambda bi, i: (bi, i, 0)),
                   pl.BlockSpec((1, tm, v_w), lambda bi, i: (bi, i, 0))],
        out_shape=[jax.ShapeDtypeStruct((b, t, qk_w), jnp.bfloat16),
                   jax.ShapeDtypeStruct((b, t, qk_w), jnp.bfloat16),
                   jax.ShapeDtypeStruct((b, t, v_w), jnp.bfloat16)],
        compiler_params=_cparams("parallel", "parallel"),
        name="mla_prep",
    )(p, gq, gkv, wq, wkv, cos_t, sin_t)


def _mla_attend(q_ref, k_ref, vx_ref, o_ref, nk):
    hw = MLA_HEAD_W
    outs = []
    for hh in range(MLA_HEADS_PER_STEP):
        lanes = slice(hh * hw, (hh + 1) * hw)
        q = q_ref[0, :, lanes]
        m = acc = None
        for j0 in range(0, nk, MLA_KEY_CHUNK):
            j1 = min(j0 + MLA_KEY_CHUNK, nk)
            s = _dot_nt(q, k_ref[0, j0:j1, lanes])
            m_j = jnp.max(s, axis=-1, keepdims=True)
            m_new = m_j if m is None else jnp.maximum(m, m_j)
            e = jnp.exp2(s - m_new).astype(jnp.bfloat16)
            pv = _dot(e, vx_ref[hh, j0:j1, :])
            acc = pv if acc is None else acc * jnp.exp2(m - m_new) + pv
            m = m_new
        outs.append(acc[:, :MLA_V] / acc[:, MLA_V:])
    o_ref[0] = jnp.concatenate(outs, axis=1).astype(o_ref.dtype)


def _mla_attn_kernel(q_ref, k_ref, v_ref, o_ref, vx_ref, *, n_ctx):
    i = pl.program_id(2)
    tq = q_ref.shape[1]
    t = k_ref.shape[1]

    @pl.when(i == 0)
    def _():
        for hh in range(MLA_HEADS_PER_STEP):
            vx_ref[hh, :, :MLA_V] = v_ref[0, :, hh * MLA_V:(hh + 1) * MLA_V]
            vx_ref[hh, :, MLA_V:] = jnp.ones((t, vx_ref.shape[2] - MLA_V), vx_ref.dtype)

    @pl.when(i < n_ctx // tq)
    def _():
        _mla_attend(q_ref, k_ref, vx_ref, o_ref, n_ctx)

    @pl.when(i >= n_ctx // tq)
    def _():
        _mla_attend(q_ref, k_ref, vx_ref, o_ref, t)


def _mla_attn(q, k, v, n_ctx):
    b, t, _ = q.shape
    tq = 2 * LANES
    hps = MLA_HEADS_PER_STEP
    assert n_ctx % tq == 0 and t % tq == 0 and MLA_HEADS % hps == 0 and MLA_HEAD_W == 2 * MLA_V
    return pl.pallas_call(
        functools.partial(_mla_attn_kernel, n_ctx=n_ctx),
        grid=(b, MLA_HEADS // hps, t // tq),
        in_specs=[pl.BlockSpec((1, tq, hps * MLA_HEAD_W), lambda bi, h, i: (bi, i, h)),
                  pl.BlockSpec((1, t, hps * MLA_HEAD_W), lambda bi, h, i: (bi, 0, h)),
                  pl.BlockSpec((1, t, hps * MLA_V), lambda bi, h, i: (bi, 0, h))],
        out_specs=pl.BlockSpec((1, tq, hps * MLA_V), lambda bi, h, i: (bi, i, h)),
        out_shape=jax.ShapeDtypeStruct((b, t, MLA_HEADS * MLA_V), jnp.bfloat16),
        scratch_shapes=[pltpu.VMEM((hps, t, MLA_HEAD_W), jnp.bfloat16)],
        compiler_params=_cparams("parallel", "parallel", "arbitrary"),
        name="mla_attn",
    )(q, k, v)


def _na_plan(rows):
    kr = min(NA_WIN_ROWS, rows)
    assert rows % NA_Q_ROWS == 0 and rows >= NA_KEY_ROWS
    starts, patterns, var_of_block = [], [], []
    for r in range(rows // NA_Q_ROWS):
        start = int(np.clip(NA_Q_ROWS * r - kr // 2, 0, rows - NA_KEY_ROWS))
        valid = np.zeros((NA_Q_ROWS, NA_KEY_ROWS), bool)
        dr0 = np.zeros((NA_Q_ROWS,), np.int64)
        for a in range(NA_Q_ROWS):
            i = NA_Q_ROWS * r + a
            r0 = int(np.clip(i - kr // 2, 0, rows - kr))
            assert start <= r0 and r0 + kr <= start + NA_KEY_ROWS
            krow = start + np.arange(NA_KEY_ROWS)
            valid[a] = (r0 <= krow) & (krow < r0 + kr)
            dr0[a] = start - i + (NA_WIN_ROWS - 1)
        key = (valid.tobytes(), dr0.tobytes())
        for v, (pk, _, _) in enumerate(patterns):
            if pk == key:
                var_of_block.append(v)
                break
        else:
            var_of_block.append(len(patterns))
            patterns.append((key, valid, dr0))
        starts.append(start)
    valid_r = np.stack([p[1] for p in patterns])
    dr0 = np.stack([p[2] for p in patterns])
    return np.array(starts), np.array(var_of_block), valid_r, dr0


def _na_bias_tables(rel_bias, rows):
    _, _, valid_r, dr0 = _na_plan(rows)
    depth, heads, n_dr, n_dc = rel_bias.shape
    cols = np.arange(GRID_W)
    col_start = np.clip(cols - NA_WIN_COLS // 2, 0, GRID_W - NA_WIN_COLS)
    valid_c = (cols[None, :] >= col_start[:, None]) & (cols[None, :] < col_start[:, None] + NA_WIN_COLS)
    dc = cols[None, :] - cols[:, None] + (NA_WIN_COLS - 1)
    onehot = (valid_c[:, :, None] & (dc[:, :, None] == np.arange(n_dc))).astype(np.float32)
    e1 = jnp.einsum('lhrd,jcd->lhjrc', rel_bias, jnp.asarray(onehot), precision=lax.Precision.HIGHEST)
    e1 = jnp.where(jnp.asarray(valid_c)[None, None, :, None, :], e1 * LOG2_E, NEG_INF)
    flat = e1.reshape(depth, heads, GRID_W, n_dr * GRID_W)
    n_win = NA_KEY_ROWS * GRID_W
    pad = jnp.full((depth, heads, GRID_W, n_win), NEG_INF, flat.dtype)
    flat = jnp.concatenate([pad, flat, pad], axis=-1)
    assert dr0.min() > -NA_KEY_ROWS and dr0.max() < n_dr
    blocks = []
    for v in range(valid_r.shape[0]):
        for a in range(NA_Q_ROWS):
            off = n_win + int(dr0[v, a]) * GRID_W
            keep = np.repeat(valid_r[v, a], GRID_W)
            blocks.append(jnp.where(jnp.asarray(keep), flat[..., off:off + n_win], NEG_INF))
    bm = jnp.stack(blocks, axis=2)
    return bm.reshape(depth, heads, valid_r.shape[0], NA_Q_ROWS * GRID_W, n_win)


def _na_attn_kernel(var_ref, ks_ref, q_ref, k_ref, v_ref, bm_ref, o_ref, vx_ref, *, n_ctx):
    del var_ref
    i = pl.program_id(2)
    tq = q_ref.shape[1]
    n_win = bm_ref.shape[-1]
    per_group = LANES // NA_HEAD_DIM
    n_group = NA_HEADS_PER_STEP // per_group
    low = lax.broadcasted_iota(jnp.int32, (1, LANES), 1) < NA_HEAD_DIM
    zero = jnp.zeros((), q_ref.dtype)

    @pl.when(i == 0)
    def _():
        for g in range(n_group):
            vx_ref[g, :, :LANES] = v_ref[0, :, g * LANES:(g + 1) * LANES]
            vx_ref[g, :, LANES:] = jnp.ones((vx_ref.shape[1], LANES), vx_ref.dtype)

    def softmax_pv(parts):
        m = None
        for s, _ in parts:
            m_s = jnp.max(s, axis=-1, keepdims=True)
            m = m_s if m is None else jnp.maximum(m, m_s)
        pv = None
        for s, vals in parts:
            pv_s = _dot(jnp.exp2(s - m).astype(jnp.bfloat16), vals)
            pv = pv_s if pv is None else pv + pv_s
        return pv[:, :LANES] / pv[:, LANES:]

    def run(window):
        if window:
            ks = pl.multiple_of(ks_ref[i], GRID_W)
        for g in range(n_group):
            lanes = slice(g * LANES, (g + 1) * LANES)
            q = q_ref[0, :, lanes]
            k_ctx = k_ref[0, :n_ctx, lanes]
            outs = []
            for hh in range(per_group):
                qh = jnp.where(low if hh == 0 else jnp.logical_not(low), q, zero)
                parts = [(_dot_nt(qh, k_ctx), vx_ref[g, :n_ctx, :])]
                if window:
                    s_w = _dot_nt(qh, k_ref[0, pl.ds(ks, n_win), lanes]) + bm_ref[g * per_group + hh, 0]
                    parts.append((s_w, vx_ref[g, pl.ds(ks, n_win), :]))
                outs.append(softmax_pv(parts))
            o_ref[0, :, lanes] = jnp.where(low, outs[0], outs[1]).astype(o_ref.dtype)

    @pl.when(i < n_ctx // tq)
    def _():
        run(False)

    @pl.when(i >= n_ctx // tq)
    def _():
        run(True)


def _na_attn(p, bm, layer, n_ctx):
    b, t, _ = p.shape
    tq = NA_Q_ROWS * GRID_W
    hps = NA_HEADS_PER_STEP
    gw = hps * NA_HEAD_DIM
    assert n_ctx % tq == 0 and (t - n_ctx) % tq == 0 and LANES == 2 * NA_HEAD_DIM
    rows = (t - n_ctx) // GRID_W
    starts, var_of_block, _, _ = _na_plan(rows)
    n_cb = n_ctx // tq
    var_tab = jnp.asarray(np.concatenate([np.zeros(n_cb, np.int32), var_of_block]).astype(np.int32))
    ks_tab = jnp.asarray(np.concatenate([np.zeros(n_cb, np.int32), n_ctx + starts * GRID_W]).astype(np.int32))
    n_win = NA_KEY_ROWS * GRID_W
    qc, kc, vc = NAQ_OFF // gw, NAK_OFF // gw, NAV_OFF // gw
    grid_spec = pltpu.PrefetchScalarGridSpec(
        num_scalar_prefetch=2,
        grid=(b, NA_HEADS // hps, t // tq),
        in_specs=[pl.BlockSpec((1, tq, gw), lambda bi, h, i, var, ks: (bi, i, qc + h)),
                  pl.BlockSpec((1, t, gw), lambda bi, h, i, var, ks: (bi, 0, kc + h)),
                  pl.BlockSpec((1, t, gw), lambda bi, h, i, var, ks: (bi, 0, vc + h)),
                  pl.BlockSpec((None, hps, 1, tq, n_win),
                               lambda bi, h, i, var, ks: (layer, h, var[i], 0, 0))],
        out_specs=pl.BlockSpec((1, tq, gw), lambda bi, h, i, var, ks: (bi, i, h)),
        scratch_shapes=[pltpu.VMEM((gw // LANES, t, 2 * LANES), jnp.bfloat16)],
    )
    return pl.pallas_call(
        functools.partial(_na_attn_kernel, n_ctx=n_ctx),
        grid_spec=grid_spec,
        out_shape=jax.ShapeDtypeStruct((b, t, NA_W), jnp.bfloat16),
        compiler_params=_cparams("parallel", "parallel", "arbitrary"),
        name="na_attn",
    )(var_tab, ks_tab, p, p, p, bm)


def _lru_kernel(x_ref, cw_ref, cb_ref, wg_ref, bg_ref, lam_ref, o_ref, xc_ref, a_ref, u_ref, *, n_ctx):
    t, wc = xc_ref.shape
    r = LRU_CHUNK
    n_chunk = t // r
    n_cc = n_ctx // r
    left = CONV_WIDTH // 2
    row = lax.broadcasted_iota(jnp.int32, (r, wc), 0)

    def conv_chunk(c, carry):
        t0 = pl.multiple_of(c * r, r)
        first = jnp.logical_or(c == 0, c == n_cc)
        last = jnp.logical_or(c == n_cc - 1, c == n_chunk - 1)
        halo = 2 * SUBLANES
        main = x_ref[0, pl.ds(t0, r), :].astype(jnp.float32)
        prev = x_ref[0, pl.ds(pl.multiple_of(jnp.maximum(t0 - halo, 0), halo), halo), :]
        nxt = x_ref[0, pl.ds(pl.multiple_of(jnp.minimum(t0 + r, t - halo), halo), halo), :]
        prev = jnp.where(first, 0.0, prev.astype(jnp.float32))
        nxt = jnp.where(last, 0.0, nxt.astype(jnp.float32))
        y = cb_ref[...] + jnp.zeros((r, wc), jnp.float32)
        for tap in range(CONV_WIDTH):
            off = tap - left
            if off == 0:
                xs = main
            elif off < 0:
                xs = pltpu.roll(main, -off, 0)
                for k in range(-off):
                    xs = jnp.where(row == k, prev[halo + off + k:halo + off + k + 1, :], xs)
            else:
                xs = pltpu.roll(main, r - off, 0)
                for k in range(off):
                    xs = jnp.where(row == r - off + k, nxt[k:k + 1, :], xs)
            y = y + xs * cw_ref[tap:tap + 1, :]
        xc_ref[pl.ds(t0, r), :] = y
        return carry

    lax.fori_loop(0, n_chunk, conv_chunk, 0)

    grow = lax.broadcasted_iota(jnp.int32, (SUBLANES, wc), 0)
    n_grp = r // SUBLANES

    def scan_direction(direction):
        reverse = direction == 1
        lam = lam_ref[direction]
        neg = -lam
        softplus = jnp.maximum(neg, 0.0) + jnp.log1p(jnp.exp(-jnp.abs(neg)))
        half_decay = (-0.5 * LRU_C * LOG2_E) * softplus

        def chunk_step(step, h):
            if reverse:
                c = jnp.where(step < n_cc, n_cc - 1 - step, n_chunk - 1 - (step - n_cc))
            else:
                c = step
            t0 = pl.multiple_of(c * r, r)
            xc = xc_ref[pl.ds(t0, r), :]
            t = jnp.tanh(_dot(xc.astype(jnp.bfloat16), wg_ref[direction, 0]) + bg_ref[direction, 0])
            a = jnp.exp2(half_decay + half_decay * t[:, :wc])
            a_ref[...] = a
            v = jnp.maximum(1.0 - a * a, 0.0)
            xh = 0.5 * xc
            u_ref[...] = jnp.where(v > 0.0, v * lax.rsqrt(v), 0.0) * (xh + xh * t[:, wc:])

            def group_step(gi, h):
                g0 = (n_grp - 1 - gi) if reverse else gi
                off = pl.multiple_of(g0 * SUBLANES, SUBLANES)
                a = a_ref[pl.ds(off, SUBLANES), :]
                u = u_ref[pl.ds(off, SUBLANES), :]
                for s in (1, 2, 4):
                    if reverse:
                        keep = grow < SUBLANES - s
                        a_s = pltpu.roll(a, SUBLANES - s, 0)
                        u_s = pltpu.roll(u, SUBLANES - s, 0)
                    else:
                        keep = grow >= s
                        a_s = pltpu.roll(a, s, 0)
                        u_s = pltpu.roll(u, s, 0)
                    u = jnp.where(keep, a * u_s + u, u)
                    a = jnp.where(keep, a * a_s, a)
                hh = a * h + u
                dst = pl.ds(pl.multiple_of(t0 + off, SUBLANES), SUBLANES)
                if reverse:
                    o_ref[0, dst, :] = o_ref[0, dst, :] + hh
                    return hh[0:1, :]
                o_ref[0, dst, :] = hh
                return hh[SUBLANES - 1:SUBLANES, :]

            return lax.fori_loop(0, n_grp, group_step, h, unroll=LRU_GROUP_UNROLL)

        lax.fori_loop(0, n_chunk, chunk_step, jnp.zeros((1, wc), jnp.float32))

    scan_direction(0)
    scan_direction(1)


def _lru(p, conv_w, conv_b, wg, bg, lam, n_ctx):
    b, t, _ = p.shape
    wc = LRU_COLS
    assert t % LRU_CHUNK == 0 and n_ctx % LRU_CHUNK == 0 and n_ctx > 0 and t > n_ctx
    xcol = LRU_OFF // wc
    return pl.pallas_call(
        functools.partial(_lru_kernel, n_ctx=n_ctx),
        grid=(b, LRU_WIDTH // wc),
        in_specs=[pl.BlockSpec((1, t, wc), lambda bi, c: (bi, 0, xcol + c)),
                  pl.BlockSpec((CONV_WIDTH, wc), lambda bi, c: (0, c)),
                  pl.BlockSpec((1, wc), lambda bi, c: (0, c)),
                  pl.BlockSpec((2, 1, wc, 2 * wc), lambda bi, c: (0, c, 0, 0)),
                  pl.BlockSpec((2, 1, 1, 2 * wc), lambda bi, c: (0, c, 0, 0)),
                  pl.BlockSpec((2, 1, wc), lambda bi, c: (0, 0, c))],
        out_specs=pl.BlockSpec((1, t, wc), lambda bi, c: (bi, 0, c)),
        out_shape=jax.ShapeDtypeStruct((b, t, LRU_WIDTH), jnp.float32),
        scratch_shapes=[pltpu.VMEM((t, wc), jnp.float32),
                        pltpu.VMEM((LRU_CHUNK, wc), jnp.float32),
                        pltpu.VMEM((LRU_CHUNK, wc), jnp.float32)],
        compiler_params=_cparams("parallel", "parallel"),
        name="rglru",
    )(p, conv_w, conv_b, wg, bg, lam)


def _merge_kernel(oa_ref, ob_ref, oc_ref, gpa_ref, gpb_ref, gpc_ref, mga_ref, mgb_ref, mgc_ref,
                  wb_ref, z_ref):
    z = None
    for n, (o_ref, gp_ref, mg_ref) in enumerate(((oa_ref, gpa_ref, mga_ref),
                                                 (ob_ref, gpb_ref, mgb_ref),
                                                 (oc_ref, gpc_ref, mgc_ref))):
        gp = gp_ref[0].astype(jnp.float32)
        tkn = (o_ref[0].astype(jnp.float32) * (gp * jax.nn.sigmoid(gp))).astype(jnp.bfloat16)
        y = jax.nn.sigmoid(mg_ref[0].astype(jnp.float32)) * _dot(tkn, wb_ref[n])
        z = y if z is None else z + y
    z_ref[0] = z.astype(z_ref.dtype)


def _merge(oa, ob, oc, p, wb, layer):
    b, t, _ = oa.shape
    d = wb.shape[-1]
    tm = _row_tile(t, 544)
    gpc, mgc = GP_OFF // BRANCH_W, MG_OFF // d
    assert GP_OFF % BRANCH_W == 0 and MG_OFF % d == 0
    o_spec = pl.BlockSpec((1, tm, BRANCH_W), lambda bi, i: (bi, i, 0))
    gp_specs = [pl.BlockSpec((1, tm, BRANCH_W), functools.partial(lambda bi, i, n: (bi, i, gpc + n), n=n))
                for n in range(N_BRANCH)]
    mg_specs = [pl.BlockSpec((1, tm, d), functools.partial(lambda bi, i, n: (bi, i, mgc + n), n=n))
                for n in range(N_BRANCH)]
    return pl.pallas_call(
        _merge_kernel,
        grid=(b, t // tm),
        in_specs=[o_spec, o_spec, o_spec, *gp_specs, *mg_specs,
                  pl.BlockSpec((None,) + wb.shape[1:], lambda bi, i: (layer, 0, 0, 0),
                               pipeline_mode=pl.Buffered(1))],
        out_specs=pl.BlockSpec((1, tm, d), lambda bi, i: (bi, i, 0)),
        out_shape=jax.ShapeDtypeStruct((b, t, d), jnp.bfloat16),
        compiler_params=_cparams("parallel", "parallel"),
        name="merge",
    )(oa, ob, oc, p, p, p, p, p, p, wb)


def _outproj_kernel(z_ref, w_ref, x_ref, gl_ref, gc_ref, o_ref, *, n_ctx):
    i = pl.program_id(1)
    tm = x_ref.shape[1]
    row = i * tm + lax.broadcasted_iota(jnp.int32, (tm, 1), 0)
    gate = jnp.where(row < n_ctx, gc_ref[...], gl_ref[0])
    o_ref[0] = x_ref[0] + gate * _dot(z_ref[0], w_ref[...])


def _outproj(z, w, xa, gt_l, gt_c, layer, n_ctx):
    b, t, d = xa.shape
    tm = _row_tile(t, 544)
    return pl.pallas_call(
        functools.partial(_outproj_kernel, n_ctx=n_ctx),
        grid=(b, t // tm),
        in_specs=[pl.BlockSpec((1, tm, d), lambda bi, i: (bi, i, 0)),
                  pl.BlockSpec((None, d, d), lambda bi, i: (layer, 0, 0), pipeline_mode=pl.Buffered(1)),
                  pl.BlockSpec((1, tm, d), lambda bi, i: (bi, i, 0)),
                  pl.BlockSpec((1, 1, d), lambda bi, i: (bi, 0, 0)),
                  pl.BlockSpec((1, d), lambda bi, i: (0, 0))],
        out_specs=pl.BlockSpec((1, tm, d), lambda bi, i: (bi, i, 0)),
        out_shape=jax.ShapeDtypeStruct((b, t, d), jnp.float32),
        compiler_params=_cparams("parallel", "parallel"),
        name="outproj",
    )(z, w, xa, gt_l, gt_c)


def _outproj_final_kernel(z_ref, w_ref, x_ref, gl_ref, g_ref, o_ref):
    o_ref[0] = _rms(x_ref[0] + gl_ref[0] * _dot(z_ref[0], w_ref[...]), g_ref[...])


def _outproj_final(z, w, xa, gt_l, g, layer, n_ctx):
    b, t, d = xa.shape
    s = t - n_ctx
    tm = _row_tile(int(np.gcd(s, n_ctx)), 512)
    skip = n_ctx // tm
    return pl.pallas_call(
        _outproj_final_kernel,
        grid=(b, s // tm),
        in_specs=[pl.BlockSpec((1, tm, d), lambda bi, i: (bi, i + skip, 0)),
                  pl.BlockSpec((None, d, d), lambda bi, i: (layer, 0, 0), pipeline_mode=pl.Buffered(1)),
                  pl.BlockSpec((1, tm, d), lambda bi, i: (bi, i + skip, 0)),
                  pl.BlockSpec((1, 1, d), lambda bi, i: (bi, 0, 0)),
                  pl.BlockSpec((1, d), lambda bi, i: (0, 0))],
        out_specs=pl.BlockSpec((1, tm, d), lambda bi, i: (bi, i, 0)),
        out_shape=jax.ShapeDtypeStruct((b, s, d), jnp.float32),
        compiler_params=_cparams("parallel", "parallel"),
        name="outproj_final",
    )(z, w, xa, gt_l, g)


def _cast_rows_kernel(x_ref, o_ref, *, n_scaled, scale):
    tr = x_ref.shape[0]
    row = pl.program_id(1) * tr + lax.broadcasted_iota(jnp.int32, (tr, 1), 0)
    x = x_ref[...]
    o_ref[...] = jnp.where(row < n_scaled, x * scale, x).astype(o_ref.dtype)


def _cast_rows(w_t, start, n_scaled, scale):
    depth, n, d = w_t.shape
    tr = start
    assert n % tr == 0 and tr % (2 * SUBLANES) == 0
    return pl.pallas_call(
        functools.partial(_cast_rows_kernel, n_scaled=n_scaled, scale=scale),
        grid=(depth, n // tr - 1),
        in_specs=[pl.BlockSpec((None, tr, d), lambda l, j: (l, j + 1, 0))],
        out_specs=pl.BlockSpec((None, tr, d), lambda l, j: (l, j, 0)),
        out_shape=jax.ShapeDtypeStruct((depth, n - start, d), jnp.bfloat16),
        compiler_params=_cparams("parallel", "parallel"),
        name="w_in_cast",
    )(w_t)


def _prep_w_in(w_in):
    w_t = jnp.swapaxes(w_in, 1, 2)
    kr0 = KV_OFF + MLA_KV_LORA
    kr1 = kr0 + MLA_ROPE
    w_kr = w_t[:, kr0:kr1]
    pad = jnp.zeros((w_t.shape[0], KR_W - 2 * MLA_ROPE, w_t.shape[2]), w_t.dtype)
    w_mla = jnp.concatenate([w_t[:, :kr0], pad, w_kr, w_kr[:, ROPE_SWAP]], axis=1)
    assert NAQ_OFF == MLA_IN_W
    return w_mla.astype(jnp.bfloat16), _cast_rows(w_t, kr1, NA_W, NA_LOG2_SCALE)


def _prep_w_uq(w_uq):
    depth, r, _ = w_uq.shape
    w = w_uq.reshape(depth, r, MLA_HEADS, MLA_NOPE + MLA_ROPE)
    rope = w[..., MLA_NOPE:]
    w = jnp.concatenate([w[..., :MLA_NOPE], rope, rope[..., ROPE_SWAP]], axis=-1)
    return w.reshape(depth, r, MLA_HEADS * MLA_HEAD_W).astype(jnp.bfloat16)


def _prep_w_ukv(w_ukv):
    depth, r, _ = w_ukv.shape
    w = w_ukv.reshape(depth, r, MLA_HEADS, MLA_NOPE + MLA_V)
    k = jnp.concatenate([w[..., :MLA_NOPE], jnp.zeros_like(w[..., :MLA_HEAD_W - MLA_NOPE])], axis=-1)
    k = k.reshape(depth, r, MLA_HEADS * MLA_HEAD_W)
    v = w[..., MLA_NOPE:].reshape(depth, r, MLA_HEADS * MLA_V)
    return jnp.concatenate([k, v], axis=-1).astype(jnp.bfloat16)


def _rope_tables(n_ctx, s):
    tok = jnp.arange(s)
    row = (tok // GRID_W).astype(jnp.float32)
    col = (tok % GRID_W).astype(jnp.float32)
    half = MLA_ROPE // 2
    inv = 1.0 / (ROPE_BASE ** (jnp.arange(0, half, 2, dtype=jnp.float32) / half))
    ang_r = row[:, None] * inv
    ang_c = col[:, None] * inv
    cos = jnp.concatenate([jnp.cos(ang_r), jnp.cos(ang_r), jnp.cos(ang_c), jnp.cos(ang_c)], axis=-1)
    sin = jnp.concatenate([-jnp.sin(ang_r), jnp.sin(ang_r), -jnp.sin(ang_c), jnp.sin(ang_c)], axis=-1)
    cos = jnp.concatenate([jnp.ones((n_ctx, MLA_ROPE), jnp.float32), cos], axis=0)
    sin = jnp.concatenate([jnp.zeros((n_ctx, MLA_ROPE), jnp.float32), sin], axis=0)
    t = n_ctx + s
    tail = MLA_HEAD_W - MLA_NOPE - MLA_ROPE
    cos_t = jnp.concatenate([jnp.ones((t, MLA_NOPE), jnp.float32), cos, jnp.zeros((t, tail), jnp.float32)], axis=-1)
    sin_t = jnp.concatenate([jnp.zeros((t, MLA_NOPE), jnp.float32), sin, jnp.zeros((t, tail), jnp.float32)], axis=-1)
    return cos_t, sin_t


def _prep_lru_gates(w_gate, b_gate):
    depth = w_gate.shape[0]
    per = LRU_COLS // LRU_BLOCK
    n_col = LRU_BLOCKS // per
    eye = jnp.eye(per, dtype=w_gate.dtype)

    def block_diag(w):
        w = w.reshape(depth, 2, n_col, per, LRU_BLOCK, LRU_BLOCK)
        w = w[:, :, :, :, :, None, :] * eye[None, None, None, :, None, :, None]
        return w.reshape(depth, 2, n_col, LRU_COLS, LRU_COLS)

    wg = jnp.concatenate([block_diag(w_gate[..., :LRU_BLOCK]), block_diag(w_gate[..., LRU_BLOCK:])], axis=-1)
    bg = b_gate.reshape(depth, 2, LRU_BLOCKS, 2 * LRU_BLOCK)
    bg = jnp.concatenate([bg[..., :LRU_BLOCK].reshape(depth, 2, n_col, 1, LRU_COLS),
                          bg[..., LRU_BLOCK:].reshape(depth, 2, n_col, 1, LRU_COLS)], axis=-1)
    return (wg * 0.5).astype(jnp.bfloat16), bg * 0.5


def kernel(x, c, ctx, c_ctx, ada_w, ada_b, norm_g, w_in, mla_q_norm_g, mla_kv_norm_g, mla_w_uq, mla_w_ukv, na_rel_bias, lru_conv_w, lru_conv_b, lru_w_gate, lru_b_gate, lru_lambda, w_branch, w_out, final_norm_g):
    b, s, d = x.shape
    n_ctx = ctx.shape[1]
    depth = ada_w.shape[0]
    rows = s // GRID_W

    w_mla_p, w_rest_p = _prep_w_in(w_in)
    wq_p = _prep_w_uq(mla_w_uq)
    wkv_p = _prep_w_ukv(mla_w_ukv)
    cos_t, sin_t = _rope_tables(n_ctx, s)
    bm = _na_bias_tables(na_rel_bias, rows)
    wg_p, bg_p = _prep_lru_gates(lru_w_gate, lru_b_gate)
    wb_p = w_branch.astype(jnp.bfloat16)
    wo_p = w_out.astype(jnp.bfloat16)

    n_cond = -(-(b + 1) // SUBLANES) * SUBLANES
    cond = jnp.concatenate([c, c_ctx[None, :], jnp.zeros((n_cond - b - 1, d), c.dtype)], axis=0)
    mod = _ada_mod(cond, ada_w, ada_b)

    xa = jnp.concatenate([ctx, x], axis=1)
    for l in range(depth):
        sh_l, sc_l, gt_l = (mod[l, :b, k * d:(k + 1) * d][:, None, :] for k in range(3))
        sh_c, sc_c, gt_c = (mod[l, b:b + 1, k * d:(k + 1) * d] for k in range(3))
        p = _inproj(xa, norm_g[l][None, :], sc_l, sh_l, sc_c, sh_c, w_mla_p, w_rest_p, l, n_ctx)
        q, k, v = _mla_prep(p, mla_q_norm_g[l][None, :], mla_kv_norm_g[l][None, :],
                            wq_p[l], wkv_p[l], cos_t, sin_t)
        o_a = _mla_attn(q, k, v, n_ctx)
        o_c = _lru(p, lru_conv_w[l], lru_conv_b[l][None, :], wg_p[l], bg_p[l],
                   lru_lambda[l][:, None, :], n_ctx)
        o_b = _na_attn(p, bm, l, n_ctx)
        z = _merge(o_a, o_b, o_c, p, wb_p, l)
        if l < depth - 1:
            xa = _outproj(z, wo_p, xa, gt_l, gt_c, l, n_ctx)
    return _outproj_final(z, wo_p, xa, gt_l, final_norm_g[None, :], depth - 1, n_ctx)
```

```python
import functools

import numpy as np
import jax
import jax.numpy as jnp
from jax import lax
from jax.experimental import pallas as pl
from jax.experimental.pallas import tpu as pltpu

GRID_W = 64
MLA_HEADS = 8
MLA_Q_LORA = 512
MLA_KV_LORA = 256
MLA_NOPE = 128
MLA_ROPE = 64
MLA_V = 128
NA_HEADS = 16
NA_HEAD_DIM = 64
NA_WIN_ROWS = 8
NA_WIN_COLS = 16
LRU_WIDTH = 1024
LRU_BLOCKS = 16
LRU_BLOCK = LRU_WIDTH // LRU_BLOCKS
LRU_C = 8.0
CONV_WIDTH = 4
N_BRANCH = 3
BRANCH_W = 1024
ROPE_BASE = 10000.0
EPS = 1e-6

LANES = 128
SUBLANES = 8
VMEM_LIMIT_BYTES = 56 * 1024 * 1024

MLA_HEAD_W = 2 * LANES
Q_OFF = 0
KV_OFF = Q_OFF + MLA_Q_LORA
KR_OFF = KV_OFF + MLA_KV_LORA
KR_W = MLA_HEAD_W
MLA_IN_W = KR_OFF + KR_W
NA_W = NA_HEADS * NA_HEAD_DIM
NAQ_OFF = MLA_IN_W
NAK_OFF = NAQ_OFF + NA_W
NAV_OFF = NAK_OFF + NA_W
LRU_OFF = NAV_OFF + NA_W
GP_OFF = LRU_OFF + LRU_WIDTH
MG_OFF = GP_OFF + N_BRANCH * BRANCH_W
NEG_INF = -1e30

_Q = MLA_ROPE // 4
ROPE_SWAP = np.concatenate([np.arange(_Q, 2 * _Q), np.arange(0, _Q),
                            np.arange(3 * _Q, 4 * _Q), np.arange(2 * _Q, 3 * _Q)])

LOG2_E = float(np.log2(np.e))
MLA_LOG2_SCALE = float((MLA_NOPE + MLA_ROPE) ** -0.5 * LOG2_E)
NA_LOG2_SCALE = float(NA_HEAD_DIM ** -0.5 * LOG2_E)
MLA_HEADS_PER_STEP = 4
MLA_KEY_CHUNK = 512
NA_HEADS_PER_STEP = 8
NA_Q_ROWS = 4
NA_KEY_ROWS = NA_Q_ROWS + NA_WIN_ROWS - 1
LRU_CHUNK = 256
LRU_COLS = 512
LRU_GROUP_UNROLL = 8


def _cparams(*sem):
    return pltpu.CompilerParams(dimension_semantics=sem, vmem_limit_bytes=VMEM_LIMIT_BYTES)


def _dot(a, b):
    return jnp.dot(a, b, preferred_element_type=jnp.float32)


def _dot_nt(a, b):
    return lax.dot_general(a, b, (((1,), (1,)), ((), ())), preferred_element_type=jnp.float32)


def _rms(x, g):
    return x * lax.rsqrt(jnp.mean(x * x, axis=-1, keepdims=True) + EPS) * g


def _row_tile(t, target):
    best = None
    for cand in range(2 * SUBLANES, min(t, target) + 1, 2 * SUBLANES):
        if t % cand == 0:
            best = cand
    assert best is not None
    return best


def _ada_kernel(c_ref, w_ref, b_ref, o_ref):
    c = c_ref[...]
    s = (c * jax.nn.sigmoid(c)).astype(jnp.bfloat16)
    o_ref[0] = _dot(s, w_ref[0].astype(jnp.bfloat16)) + b_ref[0]


def _ada_mod(cond, ada_w, ada_b):
    depth, d, n = ada_w.shape
    rows = cond.shape[0]
    tn = 1024
    return pl.pallas_call(
        _ada_kernel,
        grid=(depth, n // tn),
        in_specs=[pl.BlockSpec((rows, d), lambda l, j: (0, 0)),
                  pl.BlockSpec((1, d, tn), lambda l, j: (l, 0, j)),
                  pl.BlockSpec((1, 1, tn), lambda l, j: (l, 0, j))],
        out_specs=pl.BlockSpec((1, rows, tn), lambda l, j: (l, 0, j)),
        out_shape=jax.ShapeDtypeStruct((depth, rows, n), jnp.float32),
        compiler_params=_cparams("parallel", "parallel"),
        name="ada_mod",
    )(cond, ada_w, ada_b.reshape(depth, 1, n))


def _inproj_kernel(x_ref, g_ref, scl_ref, shl_ref, scc_ref, shc_ref, wa_ref, wb_ref, o_ref, h_ref,
                   *, n_ctx):
    i = pl.program_id(1)
    j = pl.program_id(2)
    tm = x_ref.shape[1]

    @pl.when(j == 0)
    def _():
        y = _rms(x_ref[0], g_ref[...])
        row = i * tm + lax.broadcasted_iota(jnp.int32, (tm, 1), 0)
        is_ctx = row < n_ctx
        sc = jnp.where(is_ctx, scc_ref[...], scl_ref[0])
        sh = jnp.where(is_ctx, shc_ref[...], shl_ref[0])
        h_ref[...] = (y * (1.0 + sc) + sh).astype(h_ref.dtype)
        o_ref[0] = _dot_nt(h_ref[...], wa_ref[...]).astype(o_ref.dtype)

    @pl.when(j > 0)
    def _():
        o_ref[0] = _dot_nt(h_ref[...], wb_ref[...]).astype(o_ref.dtype)


def _inproj(xa, g, sc_l, sh_l, sc_c, sh_c, w_mla, w_rest, layer, n_ctx):
    b, t, d = xa.shape
    tn = w_mla.shape[1]
    n_rest = w_rest.shape[1]
    assert n_rest % tn == 0
    tm = _row_tile(t, 1088)
    vec = pl.BlockSpec((1, d), lambda bi, i, j: (0, 0))
    per_b = pl.BlockSpec((1, 1, d), lambda bi, i, j: (bi, 0, 0))
    return pl.pallas_call(
        functools.partial(_inproj_kernel, n_ctx=n_ctx),
        grid=(b, t // tm, 1 + n_rest // tn),
        in_specs=[pl.BlockSpec((1, tm, d), lambda bi, i, j: (bi, i, 0)),
                  vec, per_b, per_b, vec, vec,
                  pl.BlockSpec((None, tn, d), lambda bi, i, j: (layer, 0, 0)),
                  pl.BlockSpec((None, tn, d), lambda bi, i, j: (layer, jnp.maximum(j - 1, 0), 0))],
        out_specs=pl.BlockSpec((1, tm, tn), lambda bi, i, j: (bi, i, j)),
        out_shape=jax.ShapeDtypeStruct((b, t, tn + n_rest), jnp.bfloat16),
        scratch_shapes=[pltpu.VMEM((tm, d), jnp.bfloat16)],
        compiler_params=_cparams("parallel", "parallel", "arbitrary"),
        name="inproj",
    )(xa, g, sc_l, sh_l, sc_c, sh_c, w_mla, w_rest)


def _mla_prep_kernel(p_ref, gq_ref, gkv_ref, wq_ref, wkv_ref, cos_ref, sin_ref,
                     q_ref, k_ref, v_ref):
    p = p_ref[0].astype(jnp.float32)
    cos = cos_ref[...]
    sin = sin_ref[...]
    shift = MLA_HEAD_W - MLA_ROPE

    def rope(a):
        return a * cos + pltpu.roll(a, shift, 1) * sin

    hq = _rms(p[:, Q_OFF:Q_OFF + MLA_Q_LORA], gq_ref[...]).astype(jnp.bfloat16)
    hkv = _rms(p[:, KV_OFF:KV_OFF + MLA_KV_LORA], gkv_ref[...]).astype(jnp.bfloat16)
    k_rope = rope(p[:, KR_OFF:KR_OFF + KR_W])
    q = _dot(hq, wq_ref[...])
    kv = _dot(hkv, wkv_ref[...])
    k_w = MLA_HEADS * MLA_HEAD_W
    for h in range(MLA_HEADS):
        sl = slice(h * MLA_HEAD_W, (h + 1) * MLA_HEAD_W)
        q_ref[0, :, sl] = (rope(q[:, sl]) * MLA_LOG2_SCALE).astype(q_ref.dtype)
        k_ref[0, :, sl] = (kv[:, sl] + k_rope).astype(k_ref.dtype)
    v_ref[0] = kv[:, k_w:].astype(v_ref.dtype)


def _mla_prep(p, gq, gkv, wq, wkv, cos_t, sin_t):
    b, t, _ = p.shape
    tm = _row_tile(t, 272)
    qk_w = MLA_HEADS * MLA_HEAD_W
    v_w = MLA_HEADS * MLA_V
    full = lambda a: pl.BlockSpec(a.shape, lambda bi, i: (0,) * a.ndim)
    return pl.pallas_call(
        _mla_prep_kernel,
        grid=(b, t // tm),
        in_specs=[pl.BlockSpec((1, tm, MLA_IN_W), lambda bi, i: (bi, i, 0)),
                  full(gq), full(gkv), full(wq), full(wkv),
                  pl.BlockSpec((tm, MLA_HEAD_W), lambda bi, i: (i, 0)),
                  pl.BlockSpec((tm, MLA_HEAD_W), lambda bi, i: (i, 0))],
        out_specs=[pl.BlockSpec((1, tm, qk_w), lambda bi, i: (bi, i, 0)),
                   pl.BlockSpec((1, tm, qk_w), lambda bi, i: (bi, i, 0)),
                   pl.BlockSpec((1, tm, v_w), lambda bi, i: (bi, i, 0))],
        out_shape=[jax.ShapeDtypeStruct((b, t, qk_w), jnp.bfloat16),
                   jax.ShapeDtypeStruct((b, t, qk_w), jnp.bfloat16),
                   jax.ShapeDtypeStruct((b, t, v_w), jnp.bfloat16)],
        compiler_params=_cparams("parallel", "parallel"),
        name="mla_prep",
    )(p, gq, gkv, wq, wkv, cos_t, sin_t)


def _mla_attend(q_ref, k_ref, vx_ref, o_ref, nk):
    hw = MLA_HEAD_W
    outs = []
    for hh in range(MLA_HEADS_PER_STEP):
        lanes = slice(hh * hw, (hh + 1) * hw)
        q = q_ref[0, :, lanes]
        m = acc = None
        for j0 in range(0, nk, MLA_KEY_CHUNK):
            j1 = min(j0 + MLA_KEY_CHUNK, nk)
            s = _dot_nt(q, k_ref[0, j0:j1, lanes])
            m_j = jnp.max(s, axis=-1, keepdims=True)
            m_new = m_j if m is None else jnp.maximum(m, m_j)
            e = jnp.exp2(s - m_new).astype(jnp.bfloat16)
            pv = _dot(e, vx_ref[hh, j0:j1, :])
            acc = pv if acc is None else acc * jnp.exp2(m - m_new) + pv
            m = m_new
        outs.append(acc[:, :MLA_V] / acc[:, MLA_V:])
    o_ref[0] = jnp.concatenate(outs, axis=1).astype(o_ref.dtype)


def _mla_attn_kernel(q_ref, k_ref, v_ref, o_ref, vx_ref, *, n_ctx):
    i = pl.program_id(2)
    tq = q_ref.shape[1]
    t = k_ref.shape[1]

    @pl.when(i == 0)
    def _():
        for hh in range(MLA_HEADS_PER_STEP):
            vx_ref[hh, :, :MLA_V] = v_ref[0, :, hh * MLA_V:(hh + 1) * MLA_V]
            vx_ref[hh, :, MLA_V:] = jnp.ones((t, vx_ref.shape[2] - MLA_V), vx_ref.dtype)

    @pl.when(i < n_ctx // tq)
    def _():
        _mla_attend(q_ref, k_ref, vx_ref, o_ref, n_ctx)

    @pl.when(i >= n_ctx // tq)
    def _():
        _mla_attend(q_ref, k_ref, vx_ref, o_ref, t)


def _mla_attn(q, k, v, n_ctx):
    b, t, _ = q.shape
    tq = 2 * LANES
    hps = MLA_HEADS_PER_STEP
    assert n_ctx % tq == 0 and t % tq == 0 and MLA_HEADS % hps == 0 and MLA_HEAD_W == 2 * MLA_V
    return pl.pallas_call(
        functools.partial(_mla_attn_kernel, n_ctx=n_ctx),
        grid=(b, MLA_HEADS // hps, t // tq),
        in_specs=[pl.BlockSpec((1, tq, hps * MLA_HEAD_W), lambda bi, h, i: (bi, i, h)),
                  pl.BlockSpec((1, t, hps * MLA_HEAD_W), lambda bi, h, i: (bi, 0, h)),
                  pl.BlockSpec((1, t, hps * MLA_V), lambda bi, h, i: (bi, 0, h))],
        out_specs=pl.BlockSpec((1, tq, hps * MLA_V), lambda bi, h, i: (bi, i, h)),
        out_shape=jax.ShapeDtypeStruct((b, t, MLA_HEADS * MLA_V), jnp.bfloat16),
        scratch_shapes=[pltpu.VMEM((hps, t, MLA_HEAD_W), jnp.bfloat16)],
        compiler_params=_cparams("parallel", "parallel", "arbitrary"),
        name="mla_attn",
    )(q, k, v)


def _na_plan(rows):
    kr = min(NA_WIN_ROWS, rows)
    assert rows % NA_Q_ROWS == 0 and rows >= NA_KEY_ROWS
    starts, patterns, var_of_block = [], [], []
    for r in range(rows // NA_Q_ROWS):
        start = int(np.clip(NA_Q_ROWS * r - kr // 2, 0, rows - NA_KEY_ROWS))
        valid = np.zeros((NA_Q_ROWS, NA_KEY_ROWS), bool)
        dr0 = np.zeros((NA_Q_ROWS,), np.int64)
        for a in range(NA_Q_ROWS):
            i = NA_Q_ROWS * r + a
            r0 = int(np.clip(i - kr // 2, 0, rows - kr))
            assert start <= r0 and r0 + kr <= start + NA_KEY_ROWS
            krow = start + np.arange(NA_KEY_ROWS)
            valid[a] = (r0 <= krow) & (krow < r0 + kr)
            dr0[a] = start - i + (NA_WIN_ROWS - 1)
        key = (valid.tobytes(), dr0.tobytes())
        for v, (pk, _, _) in enumerate(patterns):
            if pk == key:
                var_of_block.append(v)
                break
        else:
            var_of_block.append(len(patterns))
            patterns.append((key, valid, dr0))
        starts.append(start)
    valid_r = np.stack([p[1] for p in patterns])
    dr0 = np.stack([p[2] for p in patterns])
    return np.array(starts), np.array(var_of_block), valid_r, dr0


def _na_bias_tables(rel_bias, rows):
    _, _, valid_r, dr0 = _na_plan(rows)
    depth, heads, n_dr, n_dc = rel_bias.shape
    cols = np.arange(GRID_W)
    col_start = np.clip(cols - NA_WIN_COLS // 2, 0, GRID_W - NA_WIN_COLS)
    valid_c = (cols[None, :] >= col_start[:, None]) & (cols[None, :] < col_start[:, None] + NA_WIN_COLS)
    dc = cols[None, :] - cols[:, None] + (NA_WIN_COLS - 1)
    onehot = (valid_c[:, :, None] & (dc[:, :, None] == np.arange(n_dc))).astype(np.float32)
    e1 = jnp.einsum('lhrd,jcd->lhjrc', rel_bias, jnp.asarray(onehot), precision=lax.Precision.HIGHEST)
    e1 = jnp.where(jnp.asarray(valid_c)[None, None, :, None, :], e1 * LOG2_E, NEG_INF)
    flat = e1.reshape(depth, heads, GRID_W, n_dr * GRID_W)
    n_win = NA_KEY_ROWS * GRID_W
    pad = jnp.full((depth, heads, GRID_W, n_win), NEG_INF, flat.dtype)
    flat = jnp.concatenate([pad, flat, pad], axis=-1)
    assert dr0.min() > -NA_KEY_ROWS and dr0.max() < n_dr
    blocks = []
    for v in range(valid_r.shape[0]):
        for a in range(NA_Q_ROWS):
            off = n_win + int(dr0[v, a]) * GRID_W
            keep = np.repeat(valid_r[v, a], GRID_W)
            blocks.append(jnp.where(jnp.asarray(keep), flat[..., off:off + n_win], NEG_INF))
    bm = jnp.stack(blocks, axis=2)
    return bm.reshape(depth, heads, valid_r.shape[0], NA_Q_ROWS * GRID_W, n_win)


def _na_attn_kernel(var_ref, ks_ref, q_ref, k_ref, v_ref, bm_ref, o_ref, vx_ref, *, n_ctx):
    del var_ref
    i = pl.program_id(2)
    tq = q_ref.shape[1]
    n_win = bm_ref.shape[-1]
    per_group = LANES // NA_HEAD_DIM
    n_group = NA_HEADS_PER_STEP // per_group
    low = lax.broadcasted_iota(jnp.int32, (1, LANES), 1) < NA_HEAD_DIM
    zero = jnp.zeros((), q_ref.dtype)

    @pl.when(i == 0)
    def _():
        for g in range(n_group):
            vx_ref[g, :, :LANES] = v_ref[0, :, g * LANES:(g + 1) * LANES]
            vx_ref[g, :, LANES:] = jnp.ones((vx_ref.shape[1], LANES), vx_ref.dtype)

    def softmax_pv(parts):
        m = None
        for s, _ in parts:
            m_s = jnp.max(s, axis=-1, keepdims=True)
            m = m_s if m is None else jnp.maximum(m, m_s)
        pv = None
        for s, vals in parts:
            pv_s = _dot(jnp.exp2(s - m).astype(jnp.bfloat16), vals)
            pv = pv_s if pv is None else pv + pv_s
        return pv[:, :LANES] / pv[:, LANES:]

    def run(window):
        if window:
            ks = pl.multiple_of(ks_ref[i], GRID_W)
        for g in range(n_group):
            lanes = slice(g * LANES, (g + 1) * LANES)
            q = q_ref[0, :, lanes]
            k_ctx = k_ref[0, :n_ctx, lanes]
            outs = []
            for hh in range(per_group):
                qh = jnp.where(low if hh == 0 else jnp.logical_not(low), q, zero)
                parts = [(_dot_nt(qh, k_ctx), vx_ref[g, :n_ctx, :])]
                if window:
                    s_w = _dot_nt(qh, k_ref[0, pl.ds(ks, n_win), lanes]) + bm_ref[g * per_group + hh, 0]
                    parts.append((s_w, vx_ref[g, pl.ds(ks, n_win), :]))
                outs.append(softmax_pv(parts))
            o_ref[0, :, lanes] = jnp.where(low, outs[0], outs[1]).astype(o_ref.dtype)

    @pl.when(i < n_ctx // tq)
    def _():
        run(False)

    @pl.when(i >= n_ctx // tq)
    def _():
        run(True)


def _na_attn(p, bm, layer, n_ctx):
    b, t, _ = p.shape
    tq = NA_Q_ROWS * GRID_W
    hps = NA_HEADS_PER_STEP
    gw = hps * NA_HEAD_DIM
    assert n_ctx % tq == 0 and (t - n_ctx) % tq == 0 and LANES == 2 * NA_HEAD_DIM
    rows = (t - n_ctx) // GRID_W
    starts, var_of_block, _, _ = _na_plan(rows)
    n_cb = n_ctx // tq
    var_tab = jnp.asarray(np.concatenate([np.zeros(n_cb, np.int32), var_of_block]).astype(np.int32))
    ks_tab = jnp.asarray(np.concatenate([np.zeros(n_cb, np.int32), n_ctx + starts * GRID_W]).astype(np.int32))
    n_win = NA_KEY_ROWS * GRID_W
    qc, kc, vc = NAQ_OFF // gw, NAK_OFF // gw, NAV_OFF // gw
    grid_spec = pltpu.PrefetchScalarGridSpec(
        num_scalar_prefetch=2,
        grid=(b, NA_HEADS // hps, t // tq),
        in_specs=[pl.BlockSpec((1, tq, gw), lambda bi, h, i, var, ks: (bi, i, qc + h)),
                  pl.BlockSpec((1, t, gw), lambda bi, h, i, var, ks: (bi, 0, kc + h)),
                  pl.BlockSpec((1, t, gw), lambda bi, h, i, var, ks: (bi, 0, vc + h)),
                  pl.BlockSpec((None, hps, 1, tq, n_win),
                               lambda bi, h, i, var, ks: (layer, h, var[i], 0, 0))],
        out_specs=pl.BlockSpec((1, tq, gw), lambda bi, h, i, var, ks: (bi, i, h)),
        scratch_shapes=[pltpu.VMEM((gw // LANES, t, 2 * LANES), jnp.bfloat16)],
    )
    return pl.pallas_call(
        functools.partial(_na_attn_kernel, n_ctx=n_ctx),
        grid_spec=grid_spec,
        out_shape=jax.ShapeDtypeStruct((b, t, NA_W), jnp.bfloat16),
        compiler_params=_cparams("parallel", "parallel", "arbitrary"),
        name="na_attn",
    )(var_tab, ks_tab, p, p, p, bm)


def _lru_kernel(x_ref, cw_ref, cb_ref, wg_ref, bg_ref, lam_ref, o_ref, xc_ref, a_ref, u_ref, *, n_ctx):
    t, wc = xc_ref.shape
    r = LRU_CHUNK
    n_chunk = t // r
    n_cc = n_ctx // r
    left = CONV_WIDTH // 2
    row = lax.broadcasted_iota(jnp.int32, (r, wc), 0)

    def conv_chunk(c, carry):
        t0 = pl.multiple_of(c * r, r)
        first = jnp.logical_or(c == 0, c == n_cc)
        last = jnp.logical_or(c == n_cc - 1, c == n_chunk - 1)
        halo = 2 * SUBLANES
        main = x_ref[0, pl.ds(t0, r), :].astype(jnp.float32)
        prev = x_ref[0, pl.ds(pl.multiple_of(jnp.maximum(t0 - halo, 0), halo), halo), :]
        nxt = x_ref[0, pl.ds(pl.multiple_of(jnp.minimum(t0 + r, t - halo), halo), halo), :]
        prev = jnp.where(first, 0.0, prev.astype(jnp.float32))
        nxt = jnp.where(last, 0.0, nxt.astype(jnp.float32))
        y = cb_ref[...] + jnp.zeros((r, wc), jnp.float32)
        for tap in range(CONV_WIDTH):
            off = tap - left
            if off == 0:
                xs = main
            elif off < 0:
                xs = pltpu.roll(main, -off, 0)
                for k in range(-off):
                    xs = jnp.where(row == k, prev[halo + off + k:halo + off + k + 1, :], xs)
            else:
                xs = pltpu.roll(main, r - off, 0)
                for k in range(off):
                    xs = jnp.where(row == r - off + k, nxt[k:k + 1, :], xs)
            y = y + xs * cw_ref[tap:tap + 1, :]
        xc_ref[pl.ds(t0, r), :] = y
        return carry

    lax.fori_loop(0, n_chunk, conv_chunk, 0)

    grow = lax.broadcasted_iota(jnp.int32, (SUBLANES, wc), 0)
    n_grp = r // SUBLANES

    def scan_direction(direction):
        reverse = direction == 1
        lam = lam_ref[direction]
        neg = -lam
        softplus = jnp.maximum(neg, 0.0) + jnp.log1p(jnp.exp(-jnp.abs(neg)))
        half_decay = (-0.5 * LRU_C * LOG2_E) * softplus

        def chunk_step(step, h):
            if reverse:
                c = jnp.where(step < n_cc, n_cc - 1 - step, n_chunk - 1 - (step - n_cc))
            else:
                c = step
            t0 = pl.multiple_of(c * r, r)
            xc = xc_ref[pl.ds(t0, r), :]
            t = jnp.tanh(_dot(xc.astype(jnp.bfloat16), wg_ref[direction, 0]) + bg_ref[direction, 0])
            a = jnp.exp2(half_decay + half_decay * t[:, :wc])
            a_ref[...] = a
            v = jnp.maximum(1.0 - a * a, 0.0)
            xh = 0.5 * xc
            u_ref[...] = jnp.where(v > 0.0, v * lax.rsqrt(v), 0.0) * (xh + xh * t[:, wc:])

            def group_step(gi, h):
                g0 = (n_grp - 1 - gi) if reverse else gi
                off = pl.multiple_of(g0 * SUBLANES, SUBLANES)
                a = a_ref[pl.ds(off, SUBLANES), :]
                u = u_ref[pl.ds(off, SUBLANES), :]
                for s in (1, 2, 4):
                    if reverse:
                        keep = grow < SUBLANES - s
                        a_s = pltpu.roll(a, SUBLANES - s, 0)
                        u_s = pltpu.roll(u, SUBLANES - s, 0)
                    else:
                        keep = grow >= s
                        a_s = pltpu.roll(a, s, 0)
                        u_s = pltpu.roll(u, s, 0)
                    u = jnp.where(keep, a * u_s + u, u)
                    a = jnp.where(keep, a * a_s, a)
                hh = a * h + u
                dst = pl.ds(pl.multiple_of(t0 + off, SUBLANES), SUBLANES)
                if reverse:
                    o_ref[0, dst, :] = o_ref[0, dst, :] + hh
                    return hh[0:1, :]
                o_ref[0, dst, :] = hh
                return hh[SUBLANES - 1:SUBLANES, :]

            return lax.fori_loop(0, n_grp, group_step, h, unroll=LRU_GROUP_UNROLL)

        lax.fori_loop(0, n_chunk, chunk_step, jnp.zeros((1, wc), jnp.float32))

    scan_direction(0)
    scan_direction(1)


def _lru(p, conv_w, conv_b, wg, bg, lam, n_ctx):
    b, t, _ = p.shape
    wc = LRU_COLS
    assert t % LRU_CHUNK == 0 and n_ctx % LRU_CHUNK == 0 and n_ctx > 0 and t > n_ctx
    xcol = LRU_OFF // wc
    return pl.pallas_call(
        functools.partial(_lru_kernel, n_ctx=n_ctx),
        grid=(b, LRU_WIDTH // wc),
        in_specs=[pl.BlockSpec((1, t, wc), lambda bi, c: (bi, 0, xcol + c)),
                  pl.BlockSpec((CONV_WIDTH, wc), lambda bi, c: (0, c)),
                  pl.BlockSpec((1, wc), lambda bi, c: (0, c)),
                  pl.BlockSpec((2, 1, wc, 2 * wc), lambda bi, c: (0, c, 0, 0)),
                  pl.BlockSpec((2, 1, 1, 2 * wc), lambda bi, c: (0, c, 0, 0)),
                  pl.BlockSpec((2, 1, wc), lambda bi, c: (0, 0, c))],
        out_specs=pl.BlockSpec((1, t, wc), lambda bi, c: (bi, 0, c)),
        out_shape=jax.ShapeDtypeStruct((b, t, LRU_WIDTH), jnp.float32),
        scratch_shapes=[pltpu.VMEM((t, wc), jnp.float32),
                        pltpu.VMEM((LRU_CHUNK, wc), jnp.float32),
                        pltpu.VMEM((LRU_CHUNK, wc), jnp.float32)],
        compiler_params=_cparams("parallel", "parallel"),
        name="rglru",
    )(p, conv_w, conv_b, wg, bg, lam)


def _merge_kernel(oa_ref, ob_ref, oc_ref, gpa_ref, gpb_ref, gpc_ref, mga_ref, mgb_ref, mgc_ref,
                  wb_ref, z_ref):
    z = None
    for n, (o_ref, gp_ref, mg_ref) in enumerate(((oa_ref, gpa_ref, mga_ref),
                                                 (ob_ref, gpb_ref, mgb_ref),
                                                 (oc_ref, gpc_ref, mgc_ref))):
        gh = gp_ref[0].astype(jnp.float32)
        tkn = (o_ref[0].astype(jnp.float32) * (gh + gh * jnp.tanh(gh))).astype(jnp.bfloat16)
        y = (0.5 + 0.5 * jnp.tanh(mg_ref[0].astype(jnp.float32))) * _dot(tkn, wb_ref[n])
        z = y if z is None else z + y
    z_ref[0] = z.astype(z_ref.dtype)


def _merge(oa, ob, oc, p, wb, layer):
    b, t, _ = oa.shape
    d = wb.shape[-1]
    tm = _row_tile(t, 544)
    gpc, mgc = GP_OFF // BRANCH_W, MG_OFF // d
    assert GP_OFF % BRANCH_W == 0 and MG_OFF % d == 0
    o_spec = pl.BlockSpec((1, tm, BRANCH_W), lambda bi, i: (bi, i, 0))
    gp_specs = [pl.BlockSpec((1, tm, BRANCH_W), functools.partial(lambda bi, i, n: (bi, i, gpc + n), n=n))
                for n in range(N_BRANCH)]
    mg_specs = [pl.BlockSpec((1, tm, d), functools.partial(lambda bi, i, n: (bi, i, mgc + n), n=n))
                for n in range(N_BRANCH)]
    return pl.pallas_call(
        _merge_kernel,
        grid=(b, t // tm),
        in_specs=[o_spec, o_spec, o_spec, *gp_specs, *mg_specs,
                  pl.BlockSpec((None,) + wb.shape[1:], lambda bi, i: (layer, 0, 0, 0),
                               pipeline_mode=pl.Buffered(1))],
        out_specs=pl.BlockSpec((1, tm, d), lambda bi, i: (bi, i, 0)),
        out_shape=jax.ShapeDtypeStruct((b, t, d), jnp.bfloat16),
        compiler_params=_cparams("parallel", "parallel"),
        name="merge",
    )(oa, ob, oc, p, p, p, p, p, p, wb)


def _outproj_kernel(z_ref, w_ref, x_ref, gl_ref, gc_ref, o_ref, *, n_ctx):
    i = pl.program_id(1)
    tm = x_ref.shape[1]
    row = i * tm + lax.broadcasted_iota(jnp.int32, (tm, 1), 0)
    gate = jnp.where(row < n_ctx, gc_ref[...], gl_ref[0])
    o_ref[0] = x_ref[0] + gate * _dot(z_ref[0], w_ref[...])


def _outproj(z, w, xa, gt_l, gt_c, layer, n_ctx):
    b, t, d = xa.shape
    tm = _row_tile(t, 544)
    return pl.pallas_call(
        functools.partial(_outproj_kernel, n_ctx=n_ctx),
        grid=(b, t // tm),
        in_specs=[pl.BlockSpec((1, tm, d), lambda bi, i: (bi, i, 0)),
                  pl.BlockSpec((None, d, d), lambda bi, i: (layer, 0, 0), pipeline_mode=pl.Buffered(1)),
                  pl.BlockSpec((1, tm, d), lambda bi, i: (bi, i, 0)),
                  pl.BlockSpec((1, 1, d), lambda bi, i: (bi, 0, 0)),
                  pl.BlockSpec((1, d), lambda bi, i: (0, 0))],
        out_specs=pl.BlockSpec((1, tm, d), lambda bi, i: (bi, i, 0)),
        out_shape=jax.ShapeDtypeStruct((b, t, d), jnp.float32),
        compiler_params=_cparams("parallel", "parallel"),
        name="outproj",
    )(z, w, xa, gt_l, gt_c)


def _outproj_final_kernel(z_ref, w_ref, x_ref, gl_ref, g_ref, o_ref):
    o_ref[0] = _rms(x_ref[0] + gl_ref[0] * _dot(z_ref[0], w_ref[...]), g_ref[...])


def _outproj_final(z, w, xa, gt_l, g, layer, n_ctx):
    b, t, d = xa.shape
    s = t - n_ctx
    tm = _row_tile(int(np.gcd(s, n_ctx)), 512)
    skip = n_ctx // tm
    return pl.pallas_call(
        _outproj_final_kernel,
        grid=(b, s // tm),
        in_specs=[pl.BlockSpec((1, tm, d), lambda bi, i: (bi, i + skip, 0)),
                  pl.BlockSpec((None, d, d), lambda bi, i: (layer, 0, 0), pipeline_mode=pl.Buffered(1)),
                  pl.BlockSpec((1, tm, d), lambda bi, i: (bi, i + skip, 0)),
                  pl.BlockSpec((1, 1, d), lambda bi, i: (bi, 0, 0)),
                  pl.BlockSpec((1, d), lambda bi, i: (0, 0))],
        out_specs=pl.BlockSpec((1, tm, d), lambda bi, i: (bi, i, 0)),
        out_shape=jax.ShapeDtypeStruct((b, s, d), jnp.float32),
        compiler_params=_cparams("parallel", "parallel"),
        name="outproj_final",
    )(z, w, xa, gt_l, g)


def _cast_rows_kernel(x_ref, o_ref, *, n_scaled, scale, half_from):
    tr = x_ref.shape[0]
    row = pl.program_id(1) * tr + lax.broadcasted_iota(jnp.int32, (tr, 1), 0)
    factor = jnp.where(row < n_scaled, scale, jnp.where(row >= half_from, 0.5, 1.0))
    o_ref[...] = (x_ref[...] * factor).astype(o_ref.dtype)


def _cast_rows(w_t, start, n_scaled, scale, half_from):
    depth, n, d = w_t.shape
    tr = start
    assert n % tr == 0 and tr % (2 * SUBLANES) == 0
    return pl.pallas_call(
        functools.partial(_cast_rows_kernel, n_scaled=n_scaled, scale=scale, half_from=half_from),
        grid=(depth, n // tr - 1),
        in_specs=[pl.BlockSpec((None, tr, d), lambda l, j: (l, j + 1, 0))],
        out_specs=pl.BlockSpec((None, tr, d), lambda l, j: (l, j, 0)),
        out_shape=jax.ShapeDtypeStruct((depth, n - start, d), jnp.bfloat16),
        compiler_params=_cparams("parallel", "parallel"),
        name="w_in_cast",
    )(w_t)


def _prep_w_in(w_in):
    w_t = jnp.swapaxes(w_in, 1, 2)
    kr0 = KV_OFF + MLA_KV_LORA
    kr1 = kr0 + MLA_ROPE
    w_kr = w_t[:, kr0:kr1]
    pad = jnp.zeros((w_t.shape[0], KR_W - 2 * MLA_ROPE, w_t.shape[2]), w_t.dtype)
    w_mla = jnp.concatenate([w_t[:, :kr0], pad, w_kr, w_kr[:, ROPE_SWAP]], axis=1)
    assert NAQ_OFF == MLA_IN_W
    return w_mla.astype(jnp.bfloat16), _cast_rows(w_t, kr1, NA_W, NA_LOG2_SCALE, GP_OFF - MLA_IN_W)


def _prep_w_uq(w_uq):
    depth, r, _ = w_uq.shape
    w = w_uq.reshape(depth, r, MLA_HEADS, MLA_NOPE + MLA_ROPE)
    rope = w[..., MLA_NOPE:]
    w = jnp.concatenate([w[..., :MLA_NOPE], rope, rope[..., ROPE_SWAP]], axis=-1)
    return w.reshape(depth, r, MLA_HEADS * MLA_HEAD_W).astype(jnp.bfloat16)


def _prep_w_ukv(w_ukv):
    depth, r, _ = w_ukv.shape
    w = w_ukv.reshape(depth, r, MLA_HEADS, MLA_NOPE + MLA_V)
    k = jnp.concatenate([w[..., :MLA_NOPE], jnp.zeros_like(w[..., :MLA_HEAD_W - MLA_NOPE])], axis=-1)
    k = k.reshape(depth, r, MLA_HEADS * MLA_HEAD_W)
    v = w[..., MLA_NOPE:].reshape(depth, r, MLA_HEADS * MLA_V)
    return jnp.concatenate([k, v], axis=-1).astype(jnp.bfloat16)


def _rope_tables(n_ctx, s):
    tok = jnp.arange(s)
    row = (tok // GRID_W).astype(jnp.float32)
    col = (tok % GRID_W).astype(jnp.float32)
    half = MLA_ROPE // 2
    inv = 1.0 / (ROPE_BASE ** (jnp.arange(0, half, 2, dtype=jnp.float32) / half))
    ang_r = row[:, None] * inv
    ang_c = col[:, None] * inv
    cos = jnp.concatenate([jnp.cos(ang_r), jnp.cos(ang_r), jnp.cos(ang_c), jnp.cos(ang_c)], axis=-1)
    sin = jnp.concatenate([-jnp.sin(ang_r), jnp.sin(ang_r), -jnp.sin(ang_c), jnp.sin(ang_c)], axis=-1)
    cos = jnp.concatenate([jnp.ones((n_ctx, MLA_ROPE), jnp.float32), cos], axis=0)
    sin = jnp.concatenate([jnp.zeros((n_ctx, MLA_ROPE), jnp.float32), sin], axis=0)
    t = n_ctx + s
    tail = MLA_HEAD_W - MLA_NOPE - MLA_ROPE
    cos_t = jnp.concatenate([jnp.ones((t, MLA_NOPE), jnp.float32), cos, jnp.zeros((t, tail), jnp.float32)], axis=-1)
    sin_t = jnp.concatenate([jnp.zeros((t, MLA_NOPE), jnp.float32), sin, jnp.zeros((t, tail), jnp.float32)], axis=-1)
    return cos_t, sin_t


def _prep_lru_gates(w_gate, b_gate):
    depth = w_gate.shape[0]
    per = LRU_COLS // LRU_BLOCK
    n_col = LRU_BLOCKS // per
    eye = jnp.eye(per, dtype=w_gate.dtype)

    def block_diag(w):
        w = w.reshape(depth, 2, n_col, per, LRU_BLOCK, LRU_BLOCK)
        w = w[:, :, :, :, :, None, :] * eye[None, None, None, :, None, :, None]
        return w.reshape(depth, 2, n_col, LRU_COLS, LRU_COLS)

    wg = jnp.concatenate([block_diag(w_gate[..., :LRU_BLOCK]), block_diag(w_gate[..., LRU_BLOCK:])], axis=-1)
    bg = b_gate.reshape(depth, 2, LRU_BLOCKS, 2 * LRU_BLOCK)
    bg = jnp.concatenate([bg[..., :LRU_BLOCK].reshape(depth, 2, n_col, 1, LRU_COLS),
                          bg[..., LRU_BLOCK:].reshape(depth, 2, n_col, 1, LRU_COLS)], axis=-1)
    return (wg * 0.5).astype(jnp.bfloat16), bg * 0.5


def kernel(x, c, ctx, c_ctx, ada_w, ada_b, norm_g, w_in, mla_q_norm_g, mla_kv_norm_g, mla_w_uq, mla_w_ukv, na_rel_bias, lru_conv_w, lru_conv_b, lru_w_gate, lru_b_gate, lru_lambda, w_branch, w_out, final_norm_g):
    b, s, d = x.shape
    n_ctx = ctx.shape[1]
    depth = ada_w.shape[0]
    rows = s // GRID_W

    w_mla_p, w_rest_p = _prep_w_in(w_in)
    wq_p = _prep_w_uq(mla_w_uq)
    wkv_p = _prep_w_ukv(mla_w_ukv)
    cos_t, sin_t = _rope_tables(n_ctx, s)
    bm = _na_bias_tables(na_rel_bias, rows)
    wg_p, bg_p = _prep_lru_gates(lru_w_gate, lru_b_gate)
    wb_p = w_branch.astype(jnp.bfloat16)
    wo_p = w_out.astype(jnp.bfloat16)

    n_cond = -(-(b + 1) // SUBLANES) * SUBLANES
    cond = jnp.concatenate([c, c_ctx[None, :], jnp.zeros((n_cond - b - 1, d), c.dtype)], axis=0)
    mod = _ada_mod(cond, ada_w, ada_b)

    xa = jnp.concatenate([ctx, x], axis=1)
    for l in range(depth):
        sh_l, sc_l, gt_l = (mod[l, :b, k * d:(k + 1) * d][:, None, :] for k in range(3))
        sh_c, sc_c, gt_c = (mod[l, b:b + 1, k * d:(k + 1) * d] for k in range(3))
        p = _inproj(xa, norm_g[l][None, :], sc_l, sh_l, sc_c, sh_c, w_mla_p, w_rest_p, l, n_ctx)
        q, k, v = _mla_prep(p, mla_q_norm_g[l][None, :], mla_kv_norm_g[l][None, :],
                            wq_p[l], wkv_p[l], cos_t, sin_t)
        o_a = _mla_attn(q, k, v, n_ctx)
        o_c = _lru(p, lru_conv_w[l], lru_conv_b[l][None, :], wg_p[l], bg_p[l],
                   lru_lambda[l][:, None, :], n_ctx)
        o_b = _na_attn(p, bm, l, n_ctx)
        z = _merge(o_a, o_b, o_c, p, wb_p, l)
        if l < depth - 1:
            xa = _outproj(z, wo_p, xa, gt_l, gt_c, l, n_ctx)
    return _outproj_final(z, wo_p, xa, gt_l, final_norm_g[None, :], depth - 1, n_ctx)
```

```python
import functools

import numpy as np
import jax
import jax.numpy as jnp
from jax import lax
from jax.experimental import pallas as pl
from jax.experimental.pallas import tpu as pltpu

GRID_W = 64
MLA_HEADS = 8
MLA_Q_LORA = 512
MLA_KV_LORA = 256
MLA_NOPE = 128
MLA_ROPE = 64
MLA_V = 128
NA_HEADS = 16
NA_HEAD_DIM = 64
NA_WIN_ROWS = 8
NA_WIN_COLS = 16
LRU_WIDTH = 1024
LRU_BLOCKS = 16
LRU_BLOCK = LRU_WIDTH // LRU_BLOCKS
LRU_C = 8.0
CONV_WIDTH = 4
N_BRANCH = 3
BRANCH_W = 1024
ROPE_BASE = 10000.0
EPS = 1e-6

LANES = 128
SUBLANES = 8
VMEM_LIMIT_BYTES = 56 * 1024 * 1024

MLA_HEAD_W = 2 * LANES
Q_OFF = 0
KV_OFF = Q_OFF + MLA_Q_LORA
KR_OFF = KV_OFF + MLA_KV_LORA
KR_W = MLA_HEAD_W
MLA_IN_W = KR_OFF + KR_W
NA_W = NA_HEADS * NA_HEAD_DIM
NAQ_OFF = MLA_IN_W
NAK_OFF = NAQ_OFF + NA_W
NAV_OFF = NAK_OFF + NA_W
LRU_OFF = NAV_OFF + NA_W
GP_OFF = LRU_OFF + LRU_WIDTH
MG_OFF = GP_OFF + N_BRANCH * BRANCH_W
NEG_INF = -1e30

_Q = MLA_ROPE // 4
ROPE_SWAP = np.concatenate([np.arange(_Q, 2 * _Q), np.arange(0, _Q),
                            np.arange(3 * _Q, 4 * _Q), np.arange(2 * _Q, 3 * _Q)])

LOG2_E = float(np.log2(np.e))
MLA_LOG2_SCALE = float((MLA_NOPE + MLA_ROPE) ** -0.5 * LOG2_E)
NA_LOG2_SCALE = float(NA_HEAD_DIM ** -0.5 * LOG2_E)
MLA_HEADS_PER_STEP = 4
MLA_KEY_CHUNK = 512
NA_HEADS_PER_STEP = 8
NA_Q_ROWS = 4
NA_KEY_ROWS = NA_Q_ROWS + NA_WIN_ROWS - 1
LRU_CHUNK = 256
LRU_COLS = 512
LRU_GROUP_UNROLL = 8


def _cparams(*sem):
    return pltpu.CompilerParams(dimension_semantics=sem, vmem_limit_bytes=VMEM_LIMIT_BYTES)


def _dot(a, b):
    return jnp.dot(a, b, preferred_element_type=jnp.float32)


def _dot_nt(a, b):
    return lax.dot_general(a, b, (((1,), (1,)), ((), ())), preferred_element_type=jnp.float32)


def _rms(x, g):
    return x * lax.rsqrt(jnp.mean(x * x, axis=-1, keepdims=True) + EPS) * g


def _row_tile(t, target):
    best = None
    for cand in range(2 * SUBLANES, min(t, target) + 1, 2 * SUBLANES):
        if t % cand == 0:
            best = cand
    assert best is not None
    return best


def _ada_kernel(c_ref, w_ref, b_ref, o_ref):
    c = c_ref[...]
    s = (c * jax.nn.sigmoid(c)).astype(jnp.bfloat16)
    o_ref[0] = _dot(s, w_ref[0].astype(jnp.bfloat16)) + b_ref[0]


def _ada_mod(cond, ada_w, ada_b):
    depth, d, n = ada_w.shape
    rows = cond.shape[0]
    tn = 1024
    return pl.pallas_call(
        _ada_kernel,
        grid=(depth, n // tn),
        in_specs=[pl.BlockSpec((rows, d), lambda l, j: (0, 0)),
                  pl.BlockSpec((1, d, tn), lambda l, j: (l, 0, j)),
                  pl.BlockSpec((1, 1, tn), lambda l, j: (l, 0, j))],
        out_specs=pl.BlockSpec((1, rows, tn), lambda l, j: (l, 0, j)),
        out_shape=jax.ShapeDtypeStruct((depth, rows, n), jnp.float32),
        compiler_params=_cparams("parallel", "parallel"),
        name="ada_mod",
    )(cond, ada_w, ada_b.reshape(depth, 1, n))


def _inproj_kernel(x_hbm, g_ref, scl_ref, shl_ref, scc_ref, shc_ref, wa_ref, wb_ref, o_ref, h_ref,
                   xbuf_ref, sem_ref, *, n_ctx):
    bi = pl.program_id(0)
    i = pl.program_id(1)
    j = pl.program_id(2)
    n_i = pl.num_programs(1)
    tm = xbuf_ref.shape[1]
    blk = bi * n_i + i
    slot = lax.rem(blk, 2)

    def x_copy(block, buf):
        src = x_hbm.at[block // n_i, pl.ds(pl.multiple_of(lax.rem(block, n_i) * tm, tm), tm), :]
        return pltpu.make_async_copy(src, xbuf_ref.at[buf], sem_ref.at[buf])

    @pl.when(jnp.logical_and(blk == 0, j == 0))
    def _():
        x_copy(blk, slot).start()

    @pl.when(j == 0)
    def _():
        @pl.when(blk + 1 < pl.num_programs(0) * n_i)
        def _():
            x_copy(blk + 1, 1 - slot).start()

        x_copy(blk, slot).wait()
        y = _rms(xbuf_ref[slot], g_ref[...])
        row = i * tm + lax.broadcasted_iota(jnp.int32, (tm, 1), 0)
        is_ctx = row < n_ctx
        sc = jnp.where(is_ctx, scc_ref[...], scl_ref[0])
        sh = jnp.where(is_ctx, shc_ref[...], shl_ref[0])
        h_ref[...] = (y * (1.0 + sc) + sh).astype(h_ref.dtype)
        o_ref[0] = _dot_nt(h_ref[...], wa_ref[...]).astype(o_ref.dtype)

    @pl.when(j > 0)
    def _():
        o_ref[0] = _dot_nt(h_ref[...], wb_ref[...]).astype(o_ref.dtype)


def _inproj(xa, g, sc_l, sh_l, sc_c, sh_c, w_mla, w_rest, layer, n_ctx):
    b, t, d = xa.shape
    tn = w_mla.shape[1]
    n_rest = w_rest.shape[1]
    assert n_rest % tn == 0
    tm = _row_tile(t, 1088)
    vec = pl.BlockSpec((1, d), lambda bi, i, j: (0, 0))
    per_b = pl.BlockSpec((1, 1, d), lambda bi, i, j: (bi, 0, 0))
    return pl.pallas_call(
        functools.partial(_inproj_kernel, n_ctx=n_ctx),
        grid=(b, t // tm, 1 + n_rest // tn),
        in_specs=[pl.BlockSpec(memory_space=pl.ANY),
                  vec, per_b, per_b, vec, vec,
                  pl.BlockSpec((None, tn, d), lambda bi, i, j: (layer, 0, 0)),
                  pl.BlockSpec((None, tn, d), lambda bi, i, j: (layer, jnp.maximum(j - 1, 0), 0))],
        out_specs=pl.BlockSpec((1, tm, tn), lambda bi, i, j: (bi, i, j)),
        out_shape=jax.ShapeDtypeStruct((b, t, tn + n_rest), jnp.bfloat16),
        scratch_shapes=[pltpu.VMEM((tm, d), jnp.bfloat16),
                        pltpu.VMEM((2, tm, d), xa.dtype),
                        pltpu.SemaphoreType.DMA((2,))],
        compiler_params=_cparams("arbitrary", "arbitrary", "arbitrary"),
        name="inproj",
    )(xa, g, sc_l, sh_l, sc_c, sh_c, w_mla, w_rest)


def _mla_prep_kernel(p_ref, gq_ref, gkv_ref, wq_ref, wkv_ref, cos_ref, sin_ref,
                     q_ref, k_ref, v_ref):
    p = p_ref[0].astype(jnp.float32)
    cos = cos_ref[...]
    sin = sin_ref[...]
    shift = MLA_HEAD_W - MLA_ROPE

    def rope(a):
        return a * cos + pltpu.roll(a, shift, 1) * sin

    hq = _rms(p[:, Q_OFF:Q_OFF + MLA_Q_LORA], gq_ref[...]).astype(jnp.bfloat16)
    hkv = _rms(p[:, KV_OFF:KV_OFF + MLA_KV_LORA], gkv_ref[...]).astype(jnp.bfloat16)
    k_rope = rope(p[:, KR_OFF:KR_OFF + KR_W])
    q = _dot(hq, wq_ref[...])
    kv = _dot(hkv, wkv_ref[...])
    k_w = MLA_HEADS * MLA_HEAD_W
    for h in range(MLA_HEADS):
        sl = slice(h * MLA_HEAD_W, (h + 1) * MLA_HEAD_W)
        q_ref[0, :, sl] = (rope(q[:, sl]) * MLA_LOG2_SCALE).astype(q_ref.dtype)
        k_ref[0, :, sl] = (kv[:, sl] + k_rope).astype(k_ref.dtype)
    v_ref[0] = kv[:, k_w:].astype(v_ref.dtype)


def _mla_prep(p, gq, gkv, wq, wkv, cos_t, sin_t):
    b, t, _ = p.shape
    tm = _row_tile(t, 272)
    qk_w = MLA_HEADS * MLA_HEAD_W
    v_w = MLA_HEADS * MLA_V
    full = lambda a: pl.BlockSpec(a.shape, lambda bi, i: (0,) * a.ndim)
    return pl.pallas_call(
        _mla_prep_kernel,
        grid=(b, t // tm),
        in_specs=[pl.BlockSpec((1, tm, MLA_IN_W), lambda bi, i: (bi, i, 0)),
                  full(gq), full(gkv), full(wq), full(wkv),
                  pl.BlockSpec((tm, MLA_HEAD_W), lambda bi, i: (i, 0)),
                  pl.BlockSpec((tm, MLA_HEAD_W), lambda bi, i: (i, 0))],
        out_specs=[pl.BlockSpec((1, tm, qk_w), lambda bi, i: (bi, i, 0)),
                   pl.BlockSpec((1, tm, qk_w), lambda bi, i: (bi, i, 0)),
                   pl.BlockSpec((1, tm, v_w), lambda bi, i: (bi, i, 0))],
        out_shape=[jax.ShapeDtypeStruct((b, t, qk_w), jnp.bfloat16),
                   jax.ShapeDtypeStruct((b, t, qk_w), jnp.bfloat16),
                   jax.ShapeDtypeStruct((b, t, v_w), jnp.bfloat16)],
        compiler_params=_cparams("parallel", "parallel"),
        name="mla_prep",
    )(p, gq, gkv, wq, wkv, cos_t, sin_t)


def _mla_attend(q_ref, k_ref, vx_ref, o_ref, nk):
    hw = MLA_HEAD_W
    outs = []
    for hh in range(MLA_HEADS_PER_STEP):
        lanes = slice(hh * hw, (hh + 1) * hw)
        q = q_ref[0, :, lanes]
        m = acc = None
        for j0 in range(0, nk, MLA_KEY_CHUNK):
            j1 = min(j0 + MLA_KEY_CHUNK, nk)
            s = _dot_nt(q, k_ref[0, j0:j1, lanes])
            m_j = jnp.max(s, axis=-1, keepdims=True)
            m_new = m_j if m is None else jnp.maximum(m, m_j)
            e = jnp.exp2(s - m_new).astype(jnp.bfloat16)
            pv = _dot(e, vx_ref[hh, j0:j1, :])
            acc = pv if acc is None else acc * jnp.exp2(m - m_new) + pv
            m = m_new
        outs.append(acc[:, :MLA_V] / acc[:, MLA_V:])
    o_ref[0] = jnp.concatenate(outs, axis=1).astype(o_ref.dtype)


def _mla_attn_kernel(q_ref, k_ref, v_ref, o_ref, vx_ref, *, n_ctx):
    i = pl.program_id(2)
    tq = q_ref.shape[1]
    t = k_ref.shape[1]

    @pl.when(i == 0)
    def _():
        for hh in range(MLA_HEADS_PER_STEP):
            vx_ref[hh, :, :MLA_V] = v_ref[0, :, hh * MLA_V:(hh + 1) * MLA_V]
            vx_ref[hh, :, MLA_V:] = jnp.ones((t, vx_ref.shape[2] - MLA_V), vx_ref.dtype)

    @pl.when(i < n_ctx // tq)
    def _():
        _mla_attend(q_ref, k_ref, vx_ref, o_ref, n_ctx)

    @pl.when(i >= n_ctx // tq)
    def _():
        _mla_attend(q_ref, k_ref, vx_ref, o_ref, t)


def _mla_attn(q, k, v, n_ctx):
    b, t, _ = q.shape
    tq = 2 * LANES
    hps = MLA_HEADS_PER_STEP
    assert n_ctx % tq == 0 and t % tq == 0 and MLA_HEADS % hps == 0 and MLA_HEAD_W == 2 * MLA_V
    return pl.pallas_call(
        functools.partial(_mla_attn_kernel, n_ctx=n_ctx),
        grid=(b, MLA_HEADS // hps, t // tq),
        in_specs=[pl.BlockSpec((1, tq, hps * MLA_HEAD_W), lambda bi, h, i: (bi, i, h)),
                  pl.BlockSpec((1, t, hps * MLA_HEAD_W), lambda bi, h, i: (bi, 0, h)),
                  pl.BlockSpec((1, t, hps * MLA_V), lambda bi, h, i: (bi, 0, h))],
        out_specs=pl.BlockSpec((1, tq, hps * MLA_V), lambda bi, h, i: (bi, i, h)),
        out_shape=jax.ShapeDtypeStruct((b, t, MLA_HEADS * MLA_V), jnp.bfloat16),
        scratch_shapes=[pltpu.VMEM((hps, t, MLA_HEAD_W), jnp.bfloat16)],
        compiler_params=_cparams("parallel", "parallel", "arbitrary"),
        name="mla_attn",
    )(q, k, v)


def _na_plan(rows):
    kr = min(NA_WIN_ROWS, rows)
    assert rows % NA_Q_ROWS == 0 and rows >= NA_KEY_ROWS
    starts, patterns, var_of_block = [], [], []
    for r in range(rows // NA_Q_ROWS):
        start = int(np.clip(NA_Q_ROWS * r - kr // 2, 0, rows - NA_KEY_ROWS))
        valid = np.zeros((NA_Q_ROWS, NA_KEY_ROWS), bool)
        dr0 = np.zeros((NA_Q_ROWS,), np.int64)
        for a in range(NA_Q_ROWS):
            i = NA_Q_ROWS * r + a
            r0 = int(np.clip(i - kr // 2, 0, rows - kr))
            assert start <= r0 and r0 + kr <= start + NA_KEY_ROWS
            krow = start + np.arange(NA_KEY_ROWS)
            valid[a] = (r0 <= krow) & (krow < r0 + kr)
            dr0[a] = start - i + (NA_WIN_ROWS - 1)
        key = (valid.tobytes(), dr0.tobytes())
        for v, (pk, _, _) in enumerate(patterns):
            if pk == key:
                var_of_block.append(v)
                break
        else:
            var_of_block.append(len(patterns))
            patterns.append((key, valid, dr0))
        starts.append(start)
    valid_r = np.stack([p[1] for p in patterns])
    dr0 = np.stack([p[2] for p in patterns])
    return np.array(starts), np.array(var_of_block), valid_r, dr0


def _na_bias_tables(rel_bias, rows):
    _, _, valid_r, dr0 = _na_plan(rows)
    depth, heads, n_dr, n_dc = rel_bias.shape
    cols = np.arange(GRID_W)
    col_start = np.clip(cols - NA_WIN_COLS // 2, 0, GRID_W - NA_WIN_COLS)
    valid_c = (cols[None, :] >= col_start[:, None]) & (cols[None, :] < col_start[:, None] + NA_WIN_COLS)
    dc = cols[None, :] - cols[:, None] + (NA_WIN_COLS - 1)
    onehot = (valid_c[:, :, None] & (dc[:, :, None] == np.arange(n_dc))).astype(np.float32)
    e1 = jnp.einsum('lhrd,jcd->lhjrc', rel_bias, jnp.asarray(onehot), precision=lax.Precision.HIGHEST)
    e1 = jnp.where(jnp.asarray(valid_c)[None, None, :, None, :], e1 * LOG2_E, NEG_INF)
    flat = e1.reshape(depth, heads, GRID_W, n_dr * GRID_W)
    n_win = NA_KEY_ROWS * GRID_W
    pad = jnp.full((depth, heads, GRID_W, n_win), NEG_INF, flat.dtype)
    flat = jnp.concatenate([pad, flat, pad], axis=-1)
    assert dr0.min() > -NA_KEY_ROWS and dr0.max() < n_dr
    blocks = []
    for v in range(valid_r.shape[0]):
        for a in range(NA_Q_ROWS):
            off = n_win + int(dr0[v, a]) * GRID_W
            keep = np.repeat(valid_r[v, a], GRID_W)
            blocks.append(jnp.where(jnp.asarray(keep), flat[..., off:off + n_win], NEG_INF))
    bm = jnp.stack(blocks, axis=2)
    return bm.reshape(depth, heads, valid_r.shape[0], NA_Q_ROWS * GRID_W, n_win)


def _na_attn_kernel(var_ref, ks_ref, q_ref, k_ref, v_ref, bm_ref, o_ref, vx_ref, *, n_ctx):
    del var_ref
    i = pl.program_id(2)
    tq = q_ref.shape[1]
    n_win = bm_ref.shape[-1]
    per_group = LANES // NA_HEAD_DIM
    n_group = NA_HEADS_PER_STEP // per_group
    low = lax.broadcasted_iota(jnp.int32, (1, LANES), 1) < NA_HEAD_DIM
    zero = jnp.zeros((), q_ref.dtype)

    @pl.when(i == 0)
    def _():
        for g in range(n_group):
            vx_ref[g, :, :LANES] = v_ref[0, :, g * LANES:(g + 1) * LANES]
            vx_ref[g, :, LANES:] = jnp.ones((vx_ref.shape[1], LANES), vx_ref.dtype)

    def softmax_pv(parts):
        m = None
        for s, _ in parts:
            m_s = jnp.max(s, axis=-1, keepdims=True)
            m = m_s if m is None else jnp.maximum(m, m_s)
        pv = None
        for s, vals in parts:
            pv_s = _dot(jnp.exp2(s - m).astype(jnp.bfloat16), vals)
            pv = pv_s if pv is None else pv + pv_s
        return pv[:, :LANES] / pv[:, LANES:]

    def run(window):
        if window:
            ks = pl.multiple_of(ks_ref[i], GRID_W)
        for g in range(n_group):
            lanes = slice(g * LANES, (g + 1) * LANES)
            q = q_ref[0, :, lanes]
            k_ctx = k_ref[0, :n_ctx, lanes]
            outs = []
            for hh in range(per_group):
                qh = jnp.where(low if hh == 0 else jnp.logical_not(low), q, zero)
                parts = [(_dot_nt(qh, k_ctx), vx_ref[g, :n_ctx, :])]
                if window:
                    s_w = _dot_nt(qh, k_ref[0, pl.ds(ks, n_win), lanes]) + bm_ref[g * per_group + hh, 0]
                    parts.append((s_w, vx_ref[g, pl.ds(ks, n_win), :]))
                outs.append(softmax_pv(parts))
            o_ref[0, :, lanes] = jnp.where(low, outs[0], outs[1]).astype(o_ref.dtype)

    @pl.when(i < n_ctx // tq)
    def _():
        run(False)

    @pl.when(i >= n_ctx // tq)
    def _():
        run(True)


def _na_attn(p, bm, layer, n_ctx):
    b, t, _ = p.shape
    tq = NA_Q_ROWS * GRID_W
    hps = NA_HEADS_PER_STEP
    gw = hps * NA_HEAD_DIM
    assert n_ctx % tq == 0 and (t - n_ctx) % tq == 0 and LANES == 2 * NA_HEAD_DIM
    rows = (t - n_ctx) // GRID_W
    starts, var_of_block, _, _ = _na_plan(rows)
    n_cb = n_ctx // tq
    var_tab = jnp.asarray(np.concatenate([np.zeros(n_cb, np.int32), var_of_block]).astype(np.int32))
    ks_tab = jnp.asarray(np.concatenate([np.zeros(n_cb, np.int32), n_ctx + starts * GRID_W]).astype(np.int32))
    n_win = NA_KEY_ROWS * GRID_W
    qc, kc, vc = NAQ_OFF // gw, NAK_OFF // gw, NAV_OFF // gw
    grid_spec = pltpu.PrefetchScalarGridSpec(
        num_scalar_prefetch=2,
        grid=(b, NA_HEADS // hps, t // tq),
        in_specs=[pl.BlockSpec((1, tq, gw), lambda bi, h, i, var, ks: (bi, i, qc + h)),
                  pl.BlockSpec((1, t, gw), lambda bi, h, i, var, ks: (bi, 0, kc + h)),
                  pl.BlockSpec((1, t, gw), lambda bi, h, i, var, ks: (bi, 0, vc + h)),
                  pl.BlockSpec((None, hps, 1, tq, n_win),
                               lambda bi, h, i, var, ks: (layer, h, var[i], 0, 0))],
        out_specs=pl.BlockSpec((1, tq, gw), lambda bi, h, i, var, ks: (bi, i, h)),
        scratch_shapes=[pltpu.VMEM((gw // LANES, t, 2 * LANES), jnp.bfloat16)],
    )
    return pl.pallas_call(
        functools.partial(_na_attn_kernel, n_ctx=n_ctx),
        grid_spec=grid_spec,
        out_shape=jax.ShapeDtypeStruct((b, t, NA_W), jnp.bfloat16),
        compiler_params=_cparams("parallel", "parallel", "arbitrary"),
        name="na_attn",
    )(var_tab, ks_tab, p, p, p, bm)


def _lru_kernel(x_ref, cw_ref, cb_ref, wg_ref, bg_ref, lam_ref, o_ref, xc_ref, a_ref, u_ref, *, n_ctx):
    t, wc = xc_ref.shape
    r = LRU_CHUNK
    n_chunk = t // r
    n_cc = n_ctx // r
    left = CONV_WIDTH // 2
    row = lax.broadcasted_iota(jnp.int32, (r, wc), 0)

    def conv_chunk(c, carry):
        t0 = pl.multiple_of(c * r, r)
        first = jnp.logical_or(c == 0, c == n_cc)
        last = jnp.logical_or(c == n_cc - 1, c == n_chunk - 1)
        halo = 2 * SUBLANES
        main = x_ref[0, pl.ds(t0, r), :].astype(jnp.float32)
        prev = x_ref[0, pl.ds(pl.multiple_of(jnp.maximum(t0 - halo, 0), halo), halo), :]
        nxt = x_ref[0, pl.ds(pl.multiple_of(jnp.minimum(t0 + r, t - halo), halo), halo), :]
        prev = jnp.where(first, 0.0, prev.astype(jnp.float32))
        nxt = jnp.where(last, 0.0, nxt.astype(jnp.float32))
        y = cb_ref[...] + jnp.zeros((r, wc), jnp.float32)
        for tap in range(CONV_WIDTH):
            off = tap - left
            if off == 0:
                xs = main
            elif off < 0:
                xs = pltpu.roll(main, -off, 0)
                for k in range(-off):
                    xs = jnp.where(row == k, prev[halo + off + k:halo + off + k + 1, :], xs)
            else:
                xs = pltpu.roll(main, r - off, 0)
                for k in range(off):
                    xs = jnp.where(row == r - off + k, nxt[k:k + 1, :], xs)
            y = y + xs * cw_ref[tap:tap + 1, :]
        xc_ref[pl.ds(t0, r), :] = y
        return carry

    lax.fori_loop(0, n_chunk, conv_chunk, 0)

    grow = lax.broadcasted_iota(jnp.int32, (SUBLANES, wc), 0)
    n_grp = r // SUBLANES

    def scan_direction(direction):
        reverse = direction == 1
        lam = lam_ref[direction]
        neg = -lam
        softplus = jnp.maximum(neg, 0.0) + jnp.log1p(jnp.exp(-jnp.abs(neg)))
        half_decay = (-0.5 * LRU_C * LOG2_E) * softplus

        def chunk_step(step, h):
            if reverse:
                c = jnp.where(step < n_cc, n_cc - 1 - step, n_chunk - 1 - (step - n_cc))
            else:
                c = step
            t0 = pl.multiple_of(c * r, r)
            xc = xc_ref[pl.ds(t0, r), :]
            t = jnp.tanh(_dot(xc.astype(jnp.bfloat16), wg_ref[direction, 0]) + bg_ref[direction, 0])
            a = jnp.exp2(half_decay + half_decay * t[:, :wc])
            a_ref[...] = a
            v = jnp.maximum(1.0 - a * a, 0.0)
            xh = 0.5 * xc
            u_ref[...] = jnp.where(v > 0.0, v * lax.rsqrt(v), 0.0) * (xh + xh * t[:, wc:])

            def group_step(gi, h):
                g0 = (n_grp - 1 - gi) if reverse else gi
                off = pl.multiple_of(g0 * SUBLANES, SUBLANES)
                a = a_ref[pl.ds(off, SUBLANES), :]
                u = u_ref[pl.ds(off, SUBLANES), :]
                for s in (1, 2, 4):
                    if reverse:
                        keep = grow < SUBLANES - s
                        a_s = pltpu.roll(a, SUBLANES - s, 0)
                        u_s = pltpu.roll(u, SUBLANES - s, 0)
                    else:
                        keep = grow >= s
                        a_s = pltpu.roll(a, s, 0)
                        u_s = pltpu.roll(u, s, 0)
                    u = jnp.where(keep, a * u_s + u, u)
                    a = jnp.where(keep, a * a_s, a)
                hh = a * h + u
                dst = pl.ds(pl.multiple_of(t0 + off, SUBLANES), SUBLANES)
                if reverse:
                    o_ref[0, dst, :] = o_ref[0, dst, :] + hh
                    return hh[0:1, :]
                o_ref[0, dst, :] = hh
                return hh[SUBLANES - 1:SUBLANES, :]

            return lax.fori_loop(0, n_grp, group_step, h, unroll=LRU_GROUP_UNROLL)

        lax.fori_loop(0, n_chunk, chunk_step, jnp.zeros((1, wc), jnp.float32))

    scan_direction(0)
    scan_direction(1)


def _lru(p, conv_w, conv_b, wg, bg, lam, n_ctx):
    b, t, _ = p.shape
    wc = LRU_COLS
    assert t % LRU_CHUNK == 0 and n_ctx % LRU_CHUNK == 0 and n_ctx > 0 and t > n_ctx
    xcol = LRU_OFF // wc
    return pl.pallas_call(
        functools.partial(_lru_kernel, n_ctx=n_ctx),
        grid=(b, LRU_WIDTH // wc),
        in_specs=[pl.BlockSpec((1, t, wc), lambda bi, c: (bi, 0, xcol + c)),
                  pl.BlockSpec((CONV_WIDTH, wc), lambda bi, c: (0, c)),
                  pl.BlockSpec((1, wc), lambda bi, c: (0, c)),
                  pl.BlockSpec((2, 1, wc, 2 * wc), lambda bi, c: (0, c, 0, 0)),
                  pl.BlockSpec((2, 1, 1, 2 * wc), lambda bi, c: (0, c, 0, 0)),
                  pl.BlockSpec((2, 1, wc), lambda bi, c: (0, 0, c))],
        out_specs=pl.BlockSpec((1, t, wc), lambda bi, c: (bi, 0, c)),
        out_shape=jax.ShapeDtypeStruct((b, t, LRU_WIDTH), jnp.float32),
        scratch_shapes=[pltpu.VMEM((t, wc), jnp.float32),
                        pltpu.VMEM((LRU_CHUNK, wc), jnp.float32),
                        pltpu.VMEM((LRU_CHUNK, wc), jnp.float32)],
        compiler_params=_cparams("parallel", "parallel"),
        name="rglru",
    )(p, conv_w, conv_b, wg, bg, lam)


def _merge_kernel(oa_ref, ob_ref, oc_ref, gpa_ref, gpb_ref, gpc_ref, mga_ref, mgb_ref, mgc_ref,
                  wb_ref, z_ref):
    z = None
    for n, (o_ref, gp_ref, mg_ref) in enumerate(((oa_ref, gpa_ref, mga_ref),
                                                 (ob_ref, gpb_ref, mgb_ref),
                                                 (oc_ref, gpc_ref, mgc_ref))):
        gh = gp_ref[0].astype(jnp.float32)
        tkn = (o_ref[0].astype(jnp.float32) * (gh + gh * jnp.tanh(gh))).astype(jnp.bfloat16)
        y = (0.5 + 0.5 * jnp.tanh(mg_ref[0].astype(jnp.float32))) * _dot(tkn, wb_ref[n])
        z = y if z is None else z + y
    z_ref[0] = z.astype(z_ref.dtype)


def _merge(oa, ob, oc, p, wb, layer):
    b, t, _ = oa.shape
    d = wb.shape[-1]
    tm = _row_tile(t, 544)
    gpc, mgc = GP_OFF // BRANCH_W, MG_OFF // d
    assert GP_OFF % BRANCH_W == 0 and MG_OFF % d == 0
    o_spec = pl.BlockSpec((1, tm, BRANCH_W), lambda bi, i: (bi, i, 0))
    gp_specs = [pl.BlockSpec((1, tm, BRANCH_W), functools.partial(lambda bi, i, n: (bi, i, gpc + n), n=n))
                for n in range(N_BRANCH)]
    mg_specs = [pl.BlockSpec((1, tm, d), functools.partial(lambda bi, i, n: (bi, i, mgc + n), n=n))
                for n in range(N_BRANCH)]
    return pl.pallas_call(
        _merge_kernel,
        grid=(b, t // tm),
        in_specs=[o_spec, o_spec, o_spec, *gp_specs, *mg_specs,
                  pl.BlockSpec((None,) + wb.shape[1:], lambda bi, i: (layer, 0, 0, 0),
                               pipeline_mode=pl.Buffered(1))],
        out_specs=pl.BlockSpec((1, tm, d), lambda bi, i: (bi, i, 0)),
        out_shape=jax.ShapeDtypeStruct((b, t, d), jnp.bfloat16),
        compiler_params=_cparams("parallel", "parallel"),
        name="merge",
    )(oa, ob, oc, p, p, p, p, p, p, wb)


def _outproj_kernel(z_ref, w_ref, x_ref, gl_ref, gc_ref, o_ref, *, n_ctx):
    i = pl.program_id(1)
    tm = x_ref.shape[1]
    row = i * tm + lax.broadcasted_iota(jnp.int32, (tm, 1), 0)
    gate = jnp.where(row < n_ctx, gc_ref[...], gl_ref[0])
    o_ref[0] = x_ref[0] + gate * _dot(z_ref[0], w_ref[...])


def _outproj(z, w, xa, gt_l, gt_c, layer, n_ctx):
    b, t, d = xa.shape
    tm = _row_tile(t, 544)
    return pl.pallas_call(
        functools.partial(_outproj_kernel, n_ctx=n_ctx),
        grid=(b, t // tm),
        in_specs=[pl.BlockSpec((1, tm, d), lambda bi, i: (bi, i, 0)),
                  pl.BlockSpec((None, d, d), lambda bi, i: (layer, 0, 0), pipeline_mode=pl.Buffered(1)),
                  pl.BlockSpec((1, tm, d), lambda bi, i: (bi, i, 0)),
                  pl.BlockSpec((1, 1, d), lambda bi, i: (bi, 0, 0)),
                  pl.BlockSpec((1, d), lambda bi, i: (0, 0))],
        out_specs=pl.BlockSpec((1, tm, d), lambda bi, i: (bi, i, 0)),
        out_shape=jax.ShapeDtypeStruct((b, t, d), jnp.float32),
        compiler_params=_cparams("parallel", "parallel"),
        name="outproj",
    )(z, w, xa, gt_l, gt_c)


def _outproj_final_kernel(z_ref, w_ref, x_ref, gl_ref, g_ref, o_ref):
    o_ref[0] = _rms(x_ref[0] + gl_ref[0] * _dot(z_ref[0], w_ref[...]), g_ref[...])


def _outproj_final(z, w, xa, gt_l, g, layer, n_ctx):
    b, t, d = xa.shape
    s = t - n_ctx
    tm = _row_tile(int(np.gcd(s, n_ctx)), 512)
    skip = n_ctx // tm
    return pl.pallas_call(
        _outproj_final_kernel,
        grid=(b, s // tm),
        in_specs=[pl.BlockSpec((1, tm, d), lambda bi, i: (bi, i + skip, 0)),
                  pl.BlockSpec((None, d, d), lambda bi, i: (layer, 0, 0), pipeline_mode=pl.Buffered(1)),
                  pl.BlockSpec((1, tm, d), lambda bi, i: (bi, i + skip, 0)),
                  pl.BlockSpec((1, 1, d), lambda bi, i: (bi, 0, 0)),
                  pl.BlockSpec((1, d), lambda bi, i: (0, 0))],
        out_specs=pl.BlockSpec((1, tm, d), lambda bi, i: (bi, i, 0)),
        out_shape=jax.ShapeDtypeStruct((b, s, d), jnp.float32),
        compiler_params=_cparams("parallel", "parallel"),
        name="outproj_final",
    )(z, w, xa, gt_l, g)


def _cast_rows_kernel(x_ref, o_ref, *, n_scaled, scale, half_from):
    tr = x_ref.shape[0]
    row = pl.program_id(1) * tr + lax.broadcasted_iota(jnp.int32, (tr, 1), 0)
    factor = jnp.where(row < n_scaled, scale, jnp.where(row >= half_from, 0.5, 1.0))
    o_ref[...] = (x_ref[...] * factor).astype(o_ref.dtype)


def _cast_rows(w_t, start, n_scaled, scale, half_from):
    depth, n, d = w_t.shape
    tr = start
    assert n % tr == 0 and tr % (2 * SUBLANES) == 0
    return pl.pallas_call(
        functools.partial(_cast_rows_kernel, n_scaled=n_scaled, scale=scale, half_from=half_from),
        grid=(depth, n // tr - 1),
        in_specs=[pl.BlockSpec((None, tr, d), lambda l, j: (l, j + 1, 0))],
        out_specs=pl.BlockSpec((None, tr, d), lambda l, j: (l, j, 0)),
        out_shape=jax.ShapeDtypeStruct((depth, n - start, d), jnp.bfloat16),
        compiler_params=_cparams("parallel", "parallel"),
        name="w_in_cast",
    )(w_t)


def _prep_w_in(w_in):
    w_t = jnp.swapaxes(w_in, 1, 2)
    kr0 = KV_OFF + MLA_KV_LORA
    kr1 = kr0 + MLA_ROPE
    w_kr = w_t[:, kr0:kr1]
    pad = jnp.zeros((w_t.shape[0], KR_W - 2 * MLA_ROPE, w_t.shape[2]), w_t.dtype)
    w_mla = jnp.concatenate([w_t[:, :kr0], pad, w_kr, w_kr[:, ROPE_SWAP]], axis=1)
    assert NAQ_OFF == MLA_IN_W
    return w_mla.astype(jnp.bfloat16), _cast_rows(w_t, kr1, NA_W, NA_LOG2_SCALE, GP_OFF - MLA_IN_W)


def _prep_w_uq(w_uq):
    depth, r, _ = w_uq.shape
    w = w_uq.reshape(depth, r, MLA_HEADS, MLA_NOPE + MLA_ROPE)
    rope = w[..., MLA_NOPE:]
    w = jnp.concatenate([w[..., :MLA_NOPE], rope, rope[..., ROPE_SWAP]], axis=-1)
    return w.reshape(depth, r, MLA_HEADS * MLA_HEAD_W).astype(jnp.bfloat16)


def _prep_w_ukv(w_ukv):
    depth, r, _ = w_ukv.shape
    w = w_ukv.reshape(depth, r, MLA_HEADS, MLA_NOPE + MLA_V)
    k = jnp.concatenate([w[..., :MLA_NOPE], jnp.zeros_like(w[..., :MLA_HEAD_W - MLA_NOPE])], axis=-1)
    k = k.reshape(depth, r, MLA_HEADS * MLA_HEAD_W)
    v = w[..., MLA_NOPE:].reshape(depth, r, MLA_HEADS * MLA_V)
    return jnp.concatenate([k, v], axis=-1).astype(jnp.bfloat16)


def _rope_tables(n_ctx, s):
    tok = jnp.arange(s)
    row = (tok // GRID_W).astype(jnp.float32)
    col = (tok % GRID_W).astype(jnp.float32)
    half = MLA_ROPE // 2
    inv = 1.0 / (ROPE_BASE ** (jnp.arange(0, half, 2, dtype=jnp.float32) / half))
    ang_r = row[:, None] * inv
    ang_c = col[:, None] * inv
    cos = jnp.concatenate([jnp.cos(ang_r), jnp.cos(ang_r), jnp.cos(ang_c), jnp.cos(ang_c)], axis=-1)
    sin = jnp.concatenate([-jnp.sin(ang_r), jnp.sin(ang_r), -jnp.sin(ang_c), jnp.sin(ang_c)], axis=-1)
    cos = jnp.concatenate([jnp.ones((n_ctx, MLA_ROPE), jnp.float32), cos], axis=0)
    sin = jnp.concatenate([jnp.zeros((n_ctx, MLA_ROPE), jnp.float32), sin], axis=0)
    t = n_ctx + s
    tail = MLA_HEAD_W - MLA_NOPE - MLA_ROPE
    cos_t = jnp.concatenate([jnp.ones((t, MLA_NOPE), jnp.float32), cos, jnp.zeros((t, tail), jnp.float32)], axis=-1)
    sin_t = jnp.concatenate([jnp.zeros((t, MLA_NOPE), jnp.float32), sin, jnp.zeros((t, tail), jnp.float32)], axis=-1)
    return cos_t, sin_t


def _prep_lru_gates(w_gate, b_gate):
    depth = w_gate.shape[0]
    per = LRU_COLS // LRU_BLOCK
    n_col = LRU_BLOCKS // per
    eye = jnp.eye(per, dtype=w_gate.dtype)

    def block_diag(w):
        w = w.reshape(depth, 2, n_col, per, LRU_BLOCK, LRU_BLOCK)
        w = w[:, :, :, :, :, None, :] * eye[None, None, None, :, None, :, None]
        return w.reshape(depth, 2, n_col, LRU_COLS, LRU_COLS)

    wg = jnp.concatenate([block_diag(w_gate[..., :LRU_BLOCK]), block_diag(w_gate[..., LRU_BLOCK:])], axis=-1)
    bg = b_gate.reshape(depth, 2, LRU_BLOCKS, 2 * LRU_BLOCK)
    bg = jnp.concatenate([bg[..., :LRU_BLOCK].reshape(depth, 2, n_col, 1, LRU_COLS),
                          bg[..., LRU_BLOCK:].reshape(depth, 2, n_col, 1, LRU_COLS)], axis=-1)
    return (wg * 0.5).astype(jnp.bfloat16), bg * 0.5


def kernel(x, c, ctx, c_ctx, ada_w, ada_b, norm_g, w_in, mla_q_norm_g, mla_kv_norm_g, mla_w_uq, mla_w_ukv, na_rel_bias, lru_conv_w, lru_conv_b, lru_w_gate, lru_b_gate, lru_lambda, w_branch, w_out, final_norm_g):
    b, s, d = x.shape
    n_ctx = ctx.shape[1]
    depth = ada_w.shape[0]
    rows = s // GRID_W

    w_mla_p, w_rest_p = _prep_w_in(w_in)
    wq_p = _prep_w_uq(mla_w_uq)
    wkv_p = _prep_w_ukv(mla_w_ukv)
    cos_t, sin_t = _rope_tables(n_ctx, s)
    bm = _na_bias_tables(na_rel_bias, rows)
    wg_p, bg_p = _prep_lru_gates(lru_w_gate, lru_b_gate)
    wb_p = w_branch.astype(jnp.bfloat16)
    wo_p = w_out.astype(jnp.bfloat16)

    n_cond = -(-(b + 1) // SUBLANES) * SUBLANES
    cond = jnp.concatenate([c, c_ctx[None, :], jnp.zeros((n_cond - b - 1, d), c.dtype)], axis=0)
    mod = _ada_mod(cond, ada_w, ada_b)

    xa = jnp.concatenate([ctx, x], axis=1)
    for l in range(depth):
        sh_l, sc_l, gt_l = (mod[l, :b, k * d:(k + 1) * d][:, None, :] for k in range(3))
        sh_c, sc_c, gt_c = (mod[l, b:b + 1, k * d:(k + 1) * d] for k in range(3))
        p = _inproj(xa, norm_g[l][None, :], sc_l, sh_l, sc_c, sh_c, w_mla_p, w_rest_p, l, n_ctx)
        q, k, v = _mla_prep(p, mla_q_norm_g[l][None, :], mla_kv_norm_g[l][None, :],
                            wq_p[l], wkv_p[l], cos_t, sin_t)
        o_a = _mla_attn(q, k, v, n_ctx)
        o_c = _lru(p, lru_conv_w[l], lru_conv_b[l][None, :], wg_p[l], bg_p[l],
                   lru_lambda[l][:, None, :], n_ctx)
        o_b = _na_attn(p, bm, l, n_ctx)
        z = _merge(o_a, o_b, o_c, p, wb_p, l)
        if l < depth - 1:
            xa = _outproj(z, wo_p, xa, gt_l, gt_c, l, n_ctx)
    return _outproj_final(z, wo_p, xa, gt_l, final_norm_g[None, :], depth - 1, n_ctx)
```
